```python
import math
import jax, jax.numpy as jnp
from jax import lax
import numpy as np

D_MODEL = 2048
BATCH = 1
SEQ = 8192
DEPTH = 1
DEC_BATCH = 128
DEC_SEQ = 4
PAST_LEN = 2048
PAGE_SIZE = 128

MIX_WIDTH = D_MODEL
ATTN_WIDTH = MIX_WIDTH // 2
CONV_WIDTH = MIX_WIDTH - ATTN_WIDTH
HEAD_DIM = 128
N_HEADS = ATTN_WIDTH // HEAD_DIM
CONV_GROUPS = CONV_WIDTH // HEAD_DIM
CONV_K = 3
BLOCK_Q = 128
IN_WIDTH = 3 * ATTN_WIDTH + N_HEADS + 3 * CONV_WIDTH
N_GROUPS = 4
EXPERTS_PER_GROUP = 8
N_EXPERTS = N_GROUPS * EXPERTS_PER_GROUP
TOP_K = 2
D_EXPERT = D_MODEL // 4
ALPHA = (2.0 * DEPTH) ** 0.25
BETA = (8.0 * DEPTH) ** -0.25
LN_EPS = 1e-5
ATTN_SCALE = HEAD_DIM ** -0.5

kernel_name = "fox_shortconv_hmoe_deepnorm_step"


def _layer_norm(x, g, b):
    xf = x.astype(jnp.float32)
    mu = jnp.mean(xf, axis=-1, keepdims=True)
    var = jnp.mean(jnp.square(xf - mu), axis=-1, keepdims=True)
    return ((xf - mu) * lax.rsqrt(var + LN_EPS) * g + b).astype(x.dtype)


def _project(x, w_in, b_f):
    b, s, _ = x.shape
    z = jnp.einsum('bsd,de->bse', x, w_in)
    a, c = ATTN_WIDTH, CONV_WIDTH
    q = z[..., 0:a].reshape(b, s, N_HEADS, HEAD_DIM)
    k = z[..., a:2 * a].reshape(b, s, N_HEADS, HEAD_DIM)
    v = z[..., 2 * a:3 * a].reshape(b, s, N_HEADS, HEAD_DIM)
    o = 3 * a
    f_logit = z[..., o:o + N_HEADS].astype(jnp.float32) + b_f.astype(jnp.float32)
    logf = jax.nn.log_sigmoid(f_logit)
    o = o + N_HEADS
    gate_b = z[..., o:o + c]
    gate_c = z[..., o + c:o + 2 * c]
    h = z[..., o + 2 * c:o + 3 * c]
    return q, k, v, logf, gate_b, gate_c, h


def _fox_prompt(q, k, v, logf):
    b, s, h, d = q.shape
    nb = s // BLOCK_Q
    cum = jnp.cumsum(logf, axis=1)
    cum_k = jnp.transpose(cum, (0, 2, 1))[:, :, None, :]
    q_blocks = jnp.moveaxis(q.reshape(b, nb, BLOCK_Q, h, d), 1, 0)
    c_blocks = jnp.moveaxis(cum.reshape(b, nb, BLOCK_Q, h), 1, 0)
    k_pos = jnp.arange(s)

    def one_block(args):
        q_i, c_i, i = args
        q_pos = i * BLOCK_Q + jnp.arange(BLOCK_Q)
        logits = jnp.einsum('bqhd,bkhd->bhqk', q_i, k).astype(jnp.float32) * ATTN_SCALE
        logits = logits + jnp.transpose(c_i, (0, 2, 1))[..., None] - cum_k
        logits = jnp.where(k_pos[None, :] <= q_pos[:, None], logits, -jnp.inf)
        p = jax.nn.softmax(logits, axis=-1)
        return jnp.einsum('bhqk,bkhd->bqhd', p.astype(v.dtype), v)

    out = lax.map(one_block, (q_blocks, c_blocks, jnp.arange(nb)))
    return jnp.moveaxis(out, 0, 1).reshape(b, s, h * d)


def _fox_sample(q, k_new, v_new, logf_new, past_k, past_v, past_logf):
    b, t, h, d = q.shape
    p_len = past_k.shape[1]
    logf_all = jnp.concatenate([past_logf.astype(jnp.float32), logf_new], axis=1)
    cum = jnp.cumsum(logf_all, axis=1)
    cum_q = jnp.transpose(cum[:, p_len:], (0, 2, 1))[..., None]
    cum_k = jnp.transpose(cum, (0, 2, 1))[:, :, None, :]
    s_past = jnp.einsum('bqhd,bkhd->bhqk', q, past_k.astype(q.dtype))
    s_new = jnp.einsum('bqhd,bkhd->bhqk', q, k_new)
    logits = jnp.concatenate([s_past, s_new], axis=-1).astype(jnp.float32) * ATTN_SCALE
    logits = logits + cum_q - cum_k
    k_pos = jnp.arange(p_len + t)
    q_pos = p_len + jnp.arange(t)
    logits = jnp.where(k_pos[None, :] <= q_pos[:, None], logits, -jnp.inf)
    p = jax.nn.softmax(logits, axis=-1).astype(v_new.dtype)
    out = (jnp.einsum('bhqk,bkhd->bqhd', p[..., :p_len], past_v.astype(v_new.dtype))
           + jnp.einsum('bhqk,bkhd->bqhd', p[..., p_len:], v_new))
    return out.reshape(b, t, h * d)


def _short_conv(gate_b, gate_c, h, prev, w_conv):
    u = gate_c * h
    s = u.shape[1]
    padded = jnp.concatenate([prev.astype(u.dtype), u], axis=1)
    y = padded[:, 0:s] * w_conv[0]
    for j in range(1, CONV_K):
        y = y + padded[:, j:j + s] * w_conv[j]
    return gate_b * y, padded[:, s:]


def _hier_moe(x, w_rg, b_rg, w_re, b_re, w_gate, w_up, w_down):
    b, s, dm = x.shape
    xt = x.reshape(b * s, dm)
    lg = jnp.einsum('td,dg->tg', xt, w_rg).astype(jnp.float32) + b_rg.astype(jnp.float32)
    pg = jax.nn.softmax(lg, axis=-1)
    g_idx = jnp.argmax(lg, axis=-1)
    p_g = jnp.take_along_axis(pg, g_idx[:, None], axis=1)[:, 0]
    le = jnp.einsum('td,de->te', xt, w_re).astype(jnp.float32) + b_re.astype(jnp.float32)
    le = le.reshape(-1, N_GROUPS, EXPERTS_PER_GROUP)
    le_sel = jnp.take_along_axis(le, g_idx[:, None, None], axis=1)[:, 0]
    pe = jax.nn.softmax(le_sel, axis=-1)
    top_p, top_i = lax.top_k(pe, TOP_K)
    w = top_p / jnp.sum(top_p, axis=-1, keepdims=True) * p_g[:, None]
    ids = g_idx[:, None] * EXPERTS_PER_GROUP + top_i
    gates = jnp.sum(jax.nn.one_hot(ids, N_EXPERTS, dtype=jnp.float32) * w[..., None], axis=1)
    hg = jnp.einsum('td,edf->tef', xt, w_gate)
    hu = jnp.einsum('td,edf->tef', xt, w_up)
    act = jax.nn.silu(hg) * hu * gates.astype(hu.dtype)[..., None]
    y = jnp.einsum('tef,efd->td', act, w_down)
    return y.reshape(b, s, dm)


def _finish_layer(x, attn, conv, w_out, ln1_g, ln1_b, w_rg, b_rg, w_re, b_re,
                  w_gate, w_up, w_down, ln2_g, ln2_b):
    mixed = jnp.concatenate([attn, conv], axis=-1)
    o = jnp.einsum('bsm,md->bsd', mixed, w_out)
    x1 = _layer_norm(ALPHA * x + o, ln1_g, ln1_b)
    f = _hier_moe(x1, w_rg, b_rg, w_re, b_re, w_gate, w_up, w_down)
    return _layer_norm(ALPHA * x1 + f, ln2_g, ln2_b)


def setup_inputs(seed: int = 0) -> dict:
    key = jax.random.key(seed)
    ks = jax.random.split(key, 24)
    f32 = jnp.float32
    n_pages = PAST_LEN // PAGE_SIZE
    n_used = DEC_BATCH * n_pages
    n_phys = n_used + max(1, n_used // 4)
    nrm = lambda k, shp, sc: jax.random.normal(k, shp, f32) * sc
    perm = jax.random.permutation(ks[0], n_phys)
    page_table = perm[:n_used].reshape(DEC_BATCH, n_pages).astype(jnp.int32)
    return {
        "x_prompt": nrm(ks[1], (BATCH, SEQ, D_MODEL), 1.0),
        "x_sample": nrm(ks[2], (DEC_BATCH, DEC_SEQ, D_MODEL), 1.0),
        "cache_k": nrm(ks[3], (DEPTH, n_phys, PAGE_SIZE, N_HEADS, HEAD_DIM), 1.0),
        "cache_v": nrm(ks[4], (DEPTH, n_phys, PAGE_SIZE, N_HEADS, HEAD_DIM), 1.0),
        "cache_logf": jax.nn.log_sigmoid(2.0 + nrm(ks[5], (DEPTH, n_phys, PAGE_SIZE, N_HEADS), 1.0)),
        "state_conv": nrm(ks[6], (DEPTH, DEC_BATCH, CONV_K - 1, CONV_WIDTH), 1.0),
        "page_table": page_table,
        "w_in": nrm(ks[7], (DEPTH, D_MODEL, IN_WIDTH), D_MODEL ** -0.5),
        "b_f": nrm(ks[8], (DEPTH, N_HEADS), 0.1),
        "w_conv": nrm(ks[9], (DEPTH, CONV_K, CONV_WIDTH), CONV_K ** -0.5),
        "w_out": nrm(ks[10], (DEPTH, MIX_WIDTH, D_MODEL), MIX_WIDTH ** -0.5 * BETA),
        "ln1_g": 1.0 + nrm(ks[11], (DEPTH, D_MODEL), 0.01),
        "ln1_b": nrm(ks[12], (DEPTH, D_MODEL), 0.01),
        "w_rg": nrm(ks[13], (DEPTH, D_MODEL, N_GROUPS), D_MODEL ** -0.5),
        "b_rg": nrm(ks[14], (DEPTH, N_GROUPS), 0.01),
        "w_re": nrm(ks[15], (DEPTH, D_MODEL, N_EXPERTS), D_MODEL ** -0.5),
        "b_re": nrm(ks[16], (DEPTH, N_EXPERTS), 0.01),
        "w_gate": nrm(ks[17], (DEPTH, N_EXPERTS, D_MODEL, D_EXPERT), D_MODEL ** -0.5),
        "w_up": nrm(ks[18], (DEPTH, N_EXPERTS, D_MODEL, D_EXPERT), D_MODEL ** -0.5),
        "w_down": nrm(ks[19], (DEPTH, N_EXPERTS, D_EXPERT, D_MODEL), D_EXPERT ** -0.5 * BETA),
        "ln2_g": 1.0 + nrm(ks[20], (DEPTH, D_MODEL), 0.01),
        "ln2_b": nrm(ks[21], (DEPTH, D_MODEL), 0.01),
    }


def reference(x_prompt, x_sample, cache_k, cache_v, cache_logf, state_conv, page_table,
              w_in, b_f, w_conv, w_out, ln1_g, ln1_b, w_rg, b_rg, w_re, b_re,
              w_gate, w_up, w_down, ln2_g, ln2_b):
    xp = x_prompt
    xs = x_sample
    db, n_pages = page_table.shape
    kp_l, vp_l, fp_l, cp_l = [], [], [], []
    ks_l, vs_l, fs_l, cs_l = [], [], [], []
    for l in range(DEPTH):
        moe_args = (w_out[l], ln1_g[l], ln1_b[l], w_rg[l], b_rg[l], w_re[l], b_re[l],
                    w_gate[l], w_up[l], w_down[l], ln2_g[l], ln2_b[l])
        q, k, v, logf, gb, gc, hh = _project(xp, w_in[l], b_f[l])
        attn = _fox_prompt(q, k, v, logf)
        prev0 = jnp.zeros((xp.shape[0], CONV_K - 1, CONV_WIDTH), xp.dtype)
        conv, conv_state = _short_conv(gb, gc, hh, prev0, w_conv[l])
        kp_l.append(k); vp_l.append(v); fp_l.append(logf); cp_l.append(conv_state)
        xp = _finish_layer(xp, attn, conv, *moe_args)
        q, k, v, logf, gb, gc, hh = _project(xs, w_in[l], b_f[l])
        past_k = cache_k[l][page_table].reshape(db, n_pages * PAGE_SIZE, N_HEADS, HEAD_DIM)
        past_v = cache_v[l][page_table].reshape(db, n_pages * PAGE_SIZE, N_HEADS, HEAD_DIM)
        past_f = cache_logf[l][page_table].reshape(db, n_pages * PAGE_SIZE, N_HEADS)
        attn = _fox_sample(q, k, v, logf, past_k, past_v, past_f)
        conv, conv_state = _short_conv(gb, gc, hh, state_conv[l], w_conv[l])
        ks_l.append(k); vs_l.append(v); fs_l.append(logf); cs_l.append(conv_state)
        xs = _finish_layer(xs, attn, conv, *moe_args)
    k_prompt = jnp.stack(kp_l); v_prompt = jnp.stack(vp_l)
    logf_prompt = jnp.stack(fp_l); conv_prompt = jnp.stack(cp_l)
    k_sample = jnp.stack(ks_l); v_sample = jnp.stack(vs_l)
    logf_sample = jnp.stack(fs_l); conv_sample = jnp.stack(cs_l)
    return (xp, xs, k_prompt, v_prompt, logf_prompt, conv_prompt,
            k_sample, v_sample, logf_sample, conv_sample)
```

```python
import functools
import math

import numpy as np
import jax
import jax.numpy as jnp
from jax import lax
from jax.experimental import pallas as pl
from jax.experimental.pallas import tpu as pltpu

f32 = jnp.float32
bf16 = jnp.bfloat16

LANES = 128
HEAD_DIM = 128
LOG2E = 1.4426950408889634
LN_EPS = 1e-5
NEG = -1e30
CONV_K = 3
N_GROUPS = 4
EXPERTS_PER_GROUP = 8
N_EXPERTS = N_GROUPS * EXPERTS_PER_GROUP
VMEM_LIMIT = 56 * 1024 * 1024

TM_PROJ = 256
TQ = 512
TM_MOE = 256


def _dot(a, b):
    return jnp.dot(a, b, preferred_element_type=f32)


def _dot_nt(a, b):
    return lax.dot_general(a, b, (((1,), (1,)), ((), ())), preferred_element_type=f32)


def _lane_tile(x, n):
    return jnp.concatenate([x] * n, axis=-1)


def _split3(x):
    hi = x.astype(bf16)
    r1 = x - hi.astype(f32)
    mid = r1.astype(bf16)
    lo = (r1 - mid.astype(f32)).astype(bf16)
    return hi, mid, lo


def _params(sem=("arbitrary",)):
    return pltpu.CompilerParams(dimension_semantics=sem, vmem_limit_bytes=VMEM_LIMIT)


def _resident(shape):
    nd = len(shape)
    return pl.BlockSpec(shape, lambda *_: (0,) * nd, pipeline_mode=pl.Buffered(1))


def _qkv_kernel(x_ref, wq_ref, wk_ref, wv_ref, wf2_ref, wfh_ref, bf_ref, psel_ref,
                q_ref, ko_ref, vo_ref, kb_ref, vb_ref, lf_ref, kx_ref, carry_ref,
                *, tm, n_heads, qscale):
    i = pl.program_id(0)
    x = x_ref[...]
    xb = x.astype(bf16)
    q_ref[...] = (_dot(xb, wq_ref[...]) * qscale).astype(bf16)
    k = _dot(xb, wk_ref[...])
    kb_ref[...] = k.astype(bf16)
    v = _dot(xb, wv_ref[...])
    vb_ref[...] = v.astype(bf16)
    for h in range(n_heads):
        ko_ref[pl.ds(h, tm, stride=n_heads), :] = k[:, h * HEAD_DIM:(h + 1) * HEAD_DIM]
        vo_ref[pl.ds(h, tm, stride=n_heads), :] = v[:, h * HEAD_DIM:(h + 1) * HEAD_DIM]

    xl = (x - xb.astype(f32)).astype(bf16)
    zf2 = _dot(xb, wf2_ref[...])
    zf = zf2[:, :LANES] + zf2[:, LANES:] + _dot(xl, wfh_ref[...]) + bf_ref[...]
    lf = jnp.minimum(zf, 0.0) - jnp.log1p(jnp.exp(-jnp.abs(zf)))
    lf_ref[...] = lf

    if kx_ref is not None:
        @pl.when(i == 0)
        def _():
            carry_ref[...] = jnp.zeros(carry_ref.shape, f32)

        row = lax.broadcasted_iota(jnp.int32, (tm, tm), 0)
        col = lax.broadcasted_iota(jnp.int32, (tm, tm), 1)
        tri = jnp.where(row >= col, 1.0, 0.0).astype(bf16)
        c3 = _dot(tri, jnp.concatenate(_split3(lf), axis=-1))
        cum = (c3[:, :LANES] + c3[:, LANES:2 * LANES] + c3[:, 2 * LANES:]) + carry_ref[0:1, :]
        carry_ref[...] = jnp.broadcast_to(cum[tm - 1:tm, :], carry_ref.shape)
        parts = jnp.concatenate(_split3(cum * LOG2E), axis=-1)
        kx_ref[...] = _dot(parts, psel_ref[...]).astype(bf16)


def _qkv_proj(x, wq, wk, wv, wf2, wfh, bfp, psel, *, n_heads, qscale, with_cum):
    t, d = x.shape
    a = wq.shape[1]
    tm = min(TM_PROJ, t)
    grid = (t // tm,)
    row_blk = lambda w: pl.BlockSpec((tm, w), lambda i: (i, 0))
    in_specs = [row_blk(d), _resident(wq.shape), _resident(wk.shape), _resident(wv.shape),
                _resident(wf2.shape), _resident(wfh.shape), _resident(bfp.shape)]
    args = [x, wq, wk, wv, wf2, wfh, bfp]
    out_shape = [jax.ShapeDtypeStruct((t, a), bf16),
                 jax.ShapeDtypeStruct((t * n_heads, HEAD_DIM), f32),
                 jax.ShapeDtypeStruct((t * n_heads, HEAD_DIM), f32),
                 jax.ShapeDtypeStruct((t, a), bf16),
                 jax.ShapeDtypeStruct((t, a), bf16),
                 jax.ShapeDtypeStruct((t, LANES), f32)]
    hd_blk = pl.BlockSpec((tm * n_heads, HEAD_DIM), lambda i: (i, 0))
    out_specs = [row_blk(a), hd_blk, hd_blk, row_blk(a), row_blk(a), row_blk(LANES)]
    scratch = []
    if with_cum:
        in_specs.append(_resident(psel.shape))
        args.append(psel)
        out_shape.append(jax.ShapeDtypeStruct((t, a), bf16))
        out_specs.append(row_blk(a))
        scratch.append(pltpu.VMEM((8, LANES), f32))

        def body(*refs):
            _qkv_kernel(*refs, tm=tm, n_heads=n_heads, qscale=qscale)
    else:
        def body(x_ref, wq_ref, wk_ref, wv_ref, wf2_ref, wfh_ref, bf_ref, *outs):
            _qkv_kernel(x_ref, wq_ref, wk_ref, wv_ref, wf2_ref, wfh_ref, bf_ref, None,
                        *outs, None, None, tm=tm, n_heads=n_heads, qscale=qscale)

    return pl.pallas_call(
        body, grid=grid, in_specs=in_specs, out_specs=out_specs, out_shape=out_shape,
        scratch_shapes=scratch, compiler_params=_params(), name="qkv_proj")(*args)


def _conv_kernel(x_ref, wb_ref, wc_ref, wh_ref, wconv_ref, p1_ref, p2_ref,
                 conv_ref, u_ref, ubuf, *, tm, group):
    i = pl.program_id(0)
    xb = x_ref[...].astype(bf16)
    gate_b = _dot(xb, wb_ref[...])
    u = _dot(xb, wc_ref[...]) * _dot(xb, wh_ref[...])
    w0 = wconv_ref[0:1, :]
    w1 = wconv_ref[1:2, :]
    w2 = wconv_ref[2:3, :]
    if group is None:
        @pl.when(i == 0)
        def _():
            ubuf[0:8, :] = jnp.zeros((8, ubuf.shape[1]), f32)

        ubuf[8:8 + tm, :] = u
        u1 = ubuf[pl.ds(7, tm), :]
        u2 = ubuf[pl.ds(6, tm), :]
        ubuf[0:8, :] = u[tm - 8:tm, :]
        u_ref[...] = u[tm - 8:tm, :]
    else:
        ubuf[0:8, :] = jnp.zeros((8, ubuf.shape[1]), f32)
        ubuf[8:8 + tm, :] = u
        t_in = lax.broadcasted_iota(jnp.int32, (tm, 1), 0) % group
        u1 = jnp.where(t_in >= 1, ubuf[pl.ds(7, tm), :], p1_ref[...])
        u2 = jnp.where(t_in >= 2, ubuf[pl.ds(6, tm), :], p2_ref[...])
        u_ref[...] = u
    y = (u2 * w0 + u1 * w1) + u * w2
    conv_ref[...] = (gate_b * y).astype(bf16)


def _conv_proj(x, wb, wc, wh, wconv, p1=None, p2=None, *, group=None):
    t, d = x.shape
    c = wb.shape[1]
    tm = min(TM_PROJ, t)
    grid = (t // tm,)
    row_blk = lambda w: pl.BlockSpec((tm, w), lambda i: (i, 0))
    in_specs = [row_blk(d), _resident(wb.shape), _resident(wc.shape), _resident(wh.shape),
                _resident(wconv.shape)]
    args = [x, wb, wc, wh, wconv]
    if group is None:
        u_shape = jax.ShapeDtypeStruct((8, c), f32)
        u_spec = pl.BlockSpec((8, c), lambda i: (0, 0))

        def body(x_ref, wb_ref, wc_ref, wh_ref, wconv_ref, conv_ref, u_ref, ubuf):
            _conv_kernel(x_ref, wb_ref, wc_ref, wh_ref, wconv_ref, None, None,
                         conv_ref, u_ref, ubuf, tm=tm, group=None)
    else:
        in_specs += [row_blk(c), row_blk(c)]
        args += [p1, p2]
        u_shape = jax.ShapeDtypeStruct((t, c), f32)
        u_spec = row_blk(c)

        def body(*refs):
            _conv_kernel(*refs, tm=tm, group=group)

    return pl.pallas_call(
        body, grid=grid, in_specs=in_specs,
        out_specs=[row_blk(c), u_spec],
        out_shape=[jax.ShapeDtypeStruct((t, c), bf16), u_shape],
        scratch_shapes=[pltpu.VMEM((tm + 8, c), f32)],
        compiler_params=_params(), name="conv_proj")(*args)


def _prompt_attn_kernel(q_ref, k_ref, kx_ref, v_ref, o_ref, m_scr, acc_scr, *, tq):
    i = pl.program_id(1)
    qa = jnp.concatenate([q_ref[...], jnp.ones((tq, LANES), bf16)], axis=-1)
    ones_v = jnp.ones((tq, LANES), bf16)
    m_scr[...] = jnp.full(m_scr.shape, NEG, f32)
    acc_scr[...] = jnp.zeros(acc_scr.shape, f32)

    def block(j, masked):
        ks = pl.multiple_of(j * tq, tq)
        ka = jnp.concatenate([k_ref[pl.ds(ks, tq), :], kx_ref[pl.ds(ks, tq), :]], axis=-1)
        s = _dot_nt(qa, ka)
        if masked:
            row = lax.broadcasted_iota(jnp.int32, (tq, tq), 0)
            col = lax.broadcasted_iota(jnp.int32, (tq, tq), 1)
            s = jnp.where(col <= row, s, NEG)
        m_prev = m_scr[...]
        m_new = jnp.maximum(m_prev, jnp.max(s, axis=1, keepdims=True))
        p = jnp.exp2(s - _lane_tile(m_new, tq // LANES))
        alpha = jnp.exp2(m_prev - m_new)
        va = jnp.concatenate([v_ref[pl.ds(ks, tq), :], ones_v], axis=-1)
        acc_scr[...] = acc_scr[...] * _lane_tile(alpha, 2) + _dot(p.astype(bf16), va)
        m_scr[...] = m_new

    def body(j, c):
        block(j, False)
        return c

    lax.fori_loop(0, i, body, 0)
    block(i, True)
    acc = acc_scr[...]
    o_ref[...] = (acc[:, :LANES] / acc[:, LANES:]).astype(o_ref.dtype)


def _prompt_attn(q, k, kx, v, *, n_heads):
    s = q.shape[0]
    tq = min(TQ, s)
    grid = (n_heads, s // tq)
    q_blk = pl.BlockSpec((tq, HEAD_DIM), lambda h, i: (i, h))
    kv_blk = pl.BlockSpec((s, HEAD_DIM), lambda h, i: (0, h))
    return pl.pallas_call(
        functools.partial(_prompt_attn_kernel, tq=tq),
        grid=grid, in_specs=[q_blk, kv_blk, kv_blk, kv_blk], out_specs=q_blk,
        out_shape=jax.ShapeDtypeStruct(q.shape, bf16),
        scratch_shapes=[pltpu.VMEM((tq, LANES), f32), pltpu.VMEM((tq, 2 * LANES), f32)],
        compiler_params=_params(("arbitrary", "arbitrary")), name="prompt_attn")(q, k, kx, v)


def _sample_attn_kernel(pt_ref, q_ref, knew_hbm, vnew_hbm, lnew_hbm, ck_hbm, cv_hbm, cl_hbm,
                        psel_ref, o_ref, kbuf, vbuf, lbuf, exbuf, sbuf, sem,
                        *, n_pages, n_batch, n_heads, n_new, page):
    b = pl.program_id(0)
    slot = b % 2
    npg = n_pages + 1

    def copies(bb, sl):
        cps = []
        for p in range(n_pages):
            pg = pt_ref[bb * n_pages + p]
            cps.append(pltpu.make_async_copy(ck_hbm.at[pg], kbuf.at[sl, p], sem.at[sl, 0]))
            cps.append(pltpu.make_async_copy(cv_hbm.at[pg], vbuf.at[sl, p], sem.at[sl, 1]))
            cps.append(pltpu.make_async_copy(cl_hbm.at[pg], lbuf.at[sl, p], sem.at[sl, 2]))
        new_rows = pl.ds(0, n_new)
        cps.append(pltpu.make_async_copy(knew_hbm.at[bb], kbuf.at[sl, n_pages, new_rows], sem.at[sl, 0]))
        cps.append(pltpu.make_async_copy(vnew_hbm.at[bb], vbuf.at[sl, n_pages, new_rows], sem.at[sl, 1]))
        cps.append(pltpu.make_async_copy(lnew_hbm.at[bb], lbuf.at[sl, n_pages, new_rows], sem.at[sl, 2]))
        return cps

    @pl.when(b == 0)
    def _():
        for sl in range(2):
            kbuf[sl, n_pages] = jnp.zeros(kbuf.shape[2:], f32)
            vbuf[sl, n_pages] = jnp.zeros(vbuf.shape[2:], f32)
        lbuf[...] = jnp.zeros(lbuf.shape, f32)
        for cp in copies(0, 0):
            cp.start()

    @pl.when(b + 1 < n_batch)
    def _():
        for cp in copies(b + 1, 1 - slot):
            cp.start()

    for cp in copies(b, slot):
        cp.wait()

    row = lax.broadcasted_iota(jnp.int32, (page, page), 0)
    col = lax.broadcasted_iota(jnp.int32, (page, page), 1)
    after = jnp.where(col > row, 1.0, 0.0).astype(bf16)
    carry = jnp.zeros((1, n_heads), f32)
    for p in range(npg - 1, -1, -1):
        lf = lbuf[slot, p]
        hi, mid, lo = _split3(lf)
        suf = (_dot(after, hi) + _dot(after, mid) + _dot(after, lo)) + carry
        carry = carry + jnp.sum(lf, axis=0, keepdims=True)
        shi, smid, slo = _split3(suf * LOG2E)
        ex = _dot(shi, psel_ref[0]) + _dot(smid, psel_ref[1]) + _dot(slo, psel_ref[2])
        exbuf[p] = ex.astype(bf16)

    qb = q_ref[0].astype(bf16)
    ones_q = jnp.ones((8, LANES), bf16)
    qa = [jnp.concatenate([qb[:, h * HEAD_DIM:(h + 1) * HEAD_DIM], ones_q], axis=-1)
          for h in range(n_heads)]
    t_row = lax.broadcasted_iota(jnp.int32, (8, page), 0)
    lane = lax.broadcasted_iota(jnp.int32, (8, page), 1)
    past_len = n_pages * page

    def score_page(p, mvec):
        visible = p * page + lane <= past_len + t_row
        out = []
        for h in range(n_heads):
            kh = kbuf[slot, p, :, h, :].astype(bf16)
            ka = jnp.concatenate([kh, exbuf[p, :, h * HEAD_DIM:(h + 1) * HEAD_DIM]], axis=-1)
            s = jnp.where(visible, _dot_nt(qa[h], ka), NEG)
            sbuf[p, h] = s
            out.append(jnp.maximum(mvec[h], s))
        return tuple(out)

    mvec = lax.fori_loop(0, npg, score_page,
                         tuple(jnp.full((8, page), NEG, f32) for _ in range(n_heads)))
    m = [jnp.max(mv, axis=1, keepdims=True) for mv in mvec]

    def value_page(p, carry):
        lvec, acc = carry
        lout, aout = [], []
        for h in range(n_heads):
            pr = jnp.exp2(sbuf[p, h] - m[h])
            vh = vbuf[slot, p, :, h, :].astype(bf16)
            lout.append(lvec[h] + pr)
            aout.append(acc[h] + _dot(pr.astype(bf16), vh))
        return tuple(lout), tuple(aout)

    zeros = tuple(jnp.zeros((8, page), f32) for _ in range(n_heads))
    lvec, acc = lax.fori_loop(0, npg, value_page, (zeros, zeros))
    for h in range(n_heads):
        l = jnp.sum(lvec[h], axis=1, keepdims=True)
        o_ref[0, :, h * HEAD_DIM:(h + 1) * HEAD_DIM] = acc[h] / l


def _sample_attn(page_table, q8, knew, vnew, lnew, ck, cv, cl, psel3, *, n_heads):
    n_batch, n_pages = page_table.shape
    n_new = knew.shape[1]
    page = ck.shape[1]
    assert page == HEAD_DIM == LANES
    a = q8.shape[-1]
    any_spec = pl.BlockSpec(memory_space=pl.ANY)
    q_blk = pl.BlockSpec((1, 8, a), lambda b, pt: (b, 0, 0))
    grid_spec = pltpu.PrefetchScalarGridSpec(
        num_scalar_prefetch=1, grid=(n_batch,),
        in_specs=[q_blk, any_spec, any_spec, any_spec, any_spec, any_spec, any_spec,
                  pl.BlockSpec(psel3.shape, lambda b, pt: (0, 0, 0))],
        out_specs=q_blk,
        scratch_shapes=[
            pltpu.VMEM((2, n_pages + 1, page, n_heads, HEAD_DIM), f32),
            pltpu.VMEM((2, n_pages + 1, page, n_heads, HEAD_DIM), f32),
            pltpu.VMEM((2, n_pages + 1, page, n_heads), f32),
            pltpu.VMEM((n_pages + 1, page, a), bf16),
            pltpu.VMEM((n_pages + 1, n_heads, 8, page), f32),
            pltpu.SemaphoreType.DMA((2, 3)),
        ])
    return pl.pallas_call(
        functools.partial(_sample_attn_kernel, n_pages=n_pages, n_batch=n_batch,
                          n_heads=n_heads, n_new=n_new, page=page),
        grid_spec=grid_spec, out_shape=jax.ShapeDtypeStruct(q8.shape, f32),
        compiler_params=_params(), name="sample_attn")(
            page_table.reshape(-1), q8, knew, vnew, lnew, ck, cv, cl, psel3)


def _layer_norm(y, g, b):
    mu = jnp.mean(y, axis=-1, keepdims=True)
    yc = y - mu
    var = jnp.mean(yc * yc, axis=-1, keepdims=True)
    return yc * lax.rsqrt(var + LN_EPS) * g + b


def _finish1_kernel(ap_ref, cp_ref, xp_ref, as_ref, cs_ref, xs_ref,
                    woa_ref, woc_ref, g_ref, b_ref, wr2_ref, wrh_ref, br_ref,
                    x1_ref, ids_ref, gw_ref, *, alpha, nb_prompt):
    is_prompt = pl.program_id(0) < nb_prompt
    attn = jnp.where(is_prompt, ap_ref[...], as_ref[...])
    conv = jnp.where(is_prompt, cp_ref[...], cs_ref[...])
    x = jnp.where(is_prompt, xp_ref[...], xs_ref[...])
    o = _dot(attn, woa_ref[...]) + _dot(conv, woc_ref[...])
    x1 = _layer_norm(alpha * x + o, g_ref[...], b_ref[...])
    x1_ref[...] = x1

    x1b = x1.astype(bf16)
    x1l = (x1 - x1b.astype(f32)).astype(bf16)
    r2 = _dot(x1b, wr2_ref[...])
    lr = r2[:, :LANES] + r2[:, LANES:] + _dot(x1l, wrh_ref[...]) + br_ref[...]
    lane = lax.broadcasted_iota(jnp.int32, lr.shape, 1)

    def first_max(mask):
        vals = jnp.where(mask, lr, -jnp.inf)
        mx = jnp.max(vals, axis=-1, keepdims=True)
        idx = jnp.min(jnp.where(mask & (vals == mx), lane, LANES), axis=-1, keepdims=True)
        return mx, idx

    is_group = lane < N_GROUPS
    gmax, gidx = first_max(is_group)
    p_group = 1.0 / jnp.sum(jnp.where(is_group, jnp.exp(lr - gmax), 0.0), axis=-1, keepdims=True)
    lo = N_GROUPS + gidx * EXPERTS_PER_GROUP
    in_group = (lane >= lo) & (lane < lo + EXPERTS_PER_GROUP)
    e1, i1 = first_max(in_group)
    e2, i2 = first_max(in_group & (lane != i1))
    r = jnp.exp(e2 - e1)
    w1 = p_group / (1.0 + r)
    w2 = p_group * r / (1.0 + r)
    ids_ref[...] = jnp.where(lane == 0, i1 - N_GROUPS, jnp.where(lane == 1, i2 - N_GROUPS, 0))
    gw_ref[...] = jnp.where(lane == 0, w1, jnp.where(lane == 1, w2, 0.0))


def _finish1(attn_p, conv_p, x_p, attn_s, conv_s, x_s, wo, g, b, wr2, wrh, br, *, alpha):
    (t_p, d), t_s = x_p.shape, x_s.shape[0]
    a = attn_p.shape[1]
    assert wo.shape[0] == 2 * a
    tm = min(TM_PROJ, t_s)
    nb_p, nb_s = t_p // tm, t_s // tm
    t_all = t_p + t_s
    p_blk = lambda w: pl.BlockSpec((tm, w), lambda i: (jnp.minimum(i, nb_p - 1), 0))
    s_blk = lambda w: pl.BlockSpec((tm, w), lambda i: (jnp.maximum(i - nb_p, 0), 0))
    out_blk = lambda w: pl.BlockSpec((tm, w), lambda i: (i, 0))
    in_specs = [p_blk(a), p_blk(a), p_blk(d), s_blk(a), s_blk(a), s_blk(d),
                pl.BlockSpec((a, d), lambda i: (0, 0), pipeline_mode=pl.Buffered(1)),
                pl.BlockSpec((a, d), lambda i: (1, 0), pipeline_mode=pl.Buffered(1)),
                _resident(g.shape), _resident(b.shape), _resident(wr2.shape),
                _resident(wrh.shape), _resident(br.shape)]
    return pl.pallas_call(
        functools.partial(_finish1_kernel, alpha=alpha, nb_prompt=nb_p),
        grid=(nb_p + nb_s,), in_specs=in_specs,
        out_specs=[out_blk(d), out_blk(LANES), out_blk(LANES)],
        out_shape=[jax.ShapeDtypeStruct((t_all, d), f32),
                   jax.ShapeDtypeStruct((t_all, LANES), jnp.int32),
                   jax.ShapeDtypeStruct((t_all, LANES), f32)],
        compiler_params=_params(), name="finish1")(
            attn_p, conv_p, x_p, attn_s, conv_s, x_s, wo, wo, g, b, wr2, wrh, br)


def _moe_kernel(te_ref, nu_ref, idx_ref, x_hbm, wg_ref, wu_ref, wd_ref, y_ref,
                xbuf, wgb, wub, wdb, sem, *, tm):
    s = pl.program_id(0)
    n_used = nu_ref[0]

    @pl.when(s < n_used)
    def _():
        slot = s % 2

        def issue(r, c):
            tok = idx_ref[0, 0, r]
            pltpu.make_async_copy(x_hbm.at[pl.ds(tok, 1)], xbuf.at[slot, pl.ds(r, 1)],
                                  sem.at[slot]).start()
            return c

        lax.fori_loop(0, tm, issue, 0, unroll=8)

    t = s - 1

    @pl.when((t >= 0) & (t < n_used))
    def _():
        slot = t % 2
        pltpu.make_async_copy(x_hbm.at[pl.ds(0, tm)], xbuf.at[slot], sem.at[slot]).wait()
        new_expert = (t == 0) | (te_ref[t] != te_ref[jnp.maximum(t - 1, 0)])

        @pl.when(new_expert)
        def _():
            wgb[...] = wg_ref[...].astype(bf16)
            wub[...] = wu_ref[...].astype(bf16)
            wdb[...] = wd_ref[...].astype(bf16)

        xb = xbuf[slot].astype(bf16)
        hg = _dot(xb, wgb[...])
        hu = _dot(xb, wub[...])
        act = hg * (1.0 / (1.0 + jnp.exp(-hg))) * hu
        y_ref[...] = _dot(act.astype(bf16), wdb[...])

    @pl.when(t >= n_used)
    def _():
        y_ref[...] = jnp.zeros(y_ref.shape, f32)


def _moe(x1, tok_of, tile_expert, n_used, wg, wu, wd):
    n_tiles, _, tm = tok_of.shape
    _, d, f = wg.shape
    last = n_tiles - 1

    def w_idx(s, te, nu):
        t = jnp.clip(s - 1, 0, jnp.maximum(nu[0] - 1, 0))
        return (te[t], 0, 0)

    def y_idx(s, te, nu):
        return (jnp.maximum(s - 1, 0), 0)

    grid_spec = pltpu.PrefetchScalarGridSpec(
        num_scalar_prefetch=2, grid=(n_tiles + 1,),
        in_specs=[pl.BlockSpec((1, 1, tm), lambda s, te, nu: (jnp.minimum(s, last), 0, 0),
                               memory_space=pltpu.SMEM),
                  pl.BlockSpec(memory_space=pl.ANY),
                  pl.BlockSpec((None, d, f), w_idx),
                  pl.BlockSpec((None, d, f), w_idx),
                  pl.BlockSpec((None, f, d), w_idx)],
        out_specs=pl.BlockSpec((tm, d), y_idx),
        scratch_shapes=[pltpu.VMEM((2, tm, d), f32),
                        pltpu.VMEM((d, f), bf16), pltpu.VMEM((d, f), bf16), pltpu.VMEM((f, d), bf16),
                        pltpu.SemaphoreType.DMA((2,))])
    return pl.pallas_call(
        functools.partial(_moe_kernel, tm=tm),
        grid_spec=grid_spec, out_shape=jax.ShapeDtypeStruct((n_tiles * tm, d), f32),
        compiler_params=_params(), name="moe")(tile_expert, n_used, tok_of, x1, wg, wu, wd)


def _finish2_kernel(idx_ref, ys_hbm, x1_ref, gw_ref, g_ref, b_ref, o_ref, gbuf, sem,
                    *, tm, n_steps, alpha):
    s = pl.program_id(0)

    @pl.when(s < n_steps)
    def _():
        slot = s % 2

        def issue(r, c):
            row = idx_ref[0, 0, r]
            pltpu.make_async_copy(ys_hbm.at[pl.ds(row, 1)], gbuf.at[slot, pl.ds(r, 1)],
                                  sem.at[slot]).start()
            return c

        lax.fori_loop(0, 2 * tm, issue, 0, unroll=8)

    @pl.when(s >= 1)
    def _():
        slot = (s - 1) % 2
        pltpu.make_async_copy(ys_hbm.at[pl.ds(0, 2 * tm)], gbuf.at[slot], sem.at[slot]).wait()
        gw = gw_ref[...]
        f = gw[:, 0:1] * gbuf[slot, 0:tm, :] + gw[:, 1:2] * gbuf[slot, tm:2 * tm, :]
        o_ref[...] = _layer_norm(alpha * x1_ref[...] + f, g_ref[...], b_ref[...])


def _finish2(ys, pos, x1, gw, g, b, *, alpha, row0, t):
    n_steps, _, two_tm = pos.shape
    tm = two_tm // 2
    d = x1.shape[1]
    blk0 = row0 // tm
    last = n_steps - 1
    in_blk = lambda w: pl.BlockSpec((tm, w), lambda s: (jnp.maximum(s - 1, 0) + blk0, 0))
    return pl.pallas_call(
        functools.partial(_finish2_kernel, tm=tm, n_steps=n_steps, alpha=alpha),
        grid=(n_steps + 1,),
        in_specs=[pl.BlockSpec((1, 1, two_tm), lambda s: (jnp.minimum(s, last), 0, 0),
                               memory_space=pltpu.SMEM),
                  pl.BlockSpec(memory_space=pl.ANY),
                  in_blk(d), in_blk(LANES), _resident(g.shape), _resident(b.shape)],
        out_specs=pl.BlockSpec((tm, d), lambda s: (jnp.maximum(s - 1, 0), 0)),
        out_shape=jax.ShapeDtypeStruct((t, d), f32),
        scratch_shapes=[pltpu.VMEM((2, two_tm, d), f32), pltpu.SemaphoreType.DMA((2,))],
        compiler_params=_params(), name="finish2")(pos, ys, x1, gw, g, b)


def _bias_selectors(n_heads, sign):
    sel = np.zeros((3, n_heads, n_heads * HEAD_DIM), np.float32)
    for p in range(3):
        for h in range(n_heads):
            sel[p, h, h * HEAD_DIM + p] = sign
    return sel


def _route_tables(ids, n_tok, tm):
    e = ids.reshape(-1)
    n_pairs = e.shape[0]
    onehot = (e[:, None] == jnp.arange(N_EXPERTS, dtype=jnp.int32)[None, :]).astype(jnp.int32)
    csum = jnp.cumsum(onehot, axis=0)
    rank = jnp.take_along_axis(csum, e[:, None], axis=1)[:, 0] - 1
    counts = csum[-1]
    padded = ((counts + tm - 1) // tm) * tm
    ends = jnp.cumsum(padded)
    pos = (ends - padded)[e] + rank
    n_tiles = (n_pairs + N_EXPERTS * (tm - 1)) // tm
    rows = n_tiles * tm
    tok_of = jnp.zeros((rows,), jnp.int32).at[pos].set(jnp.arange(n_pairs, dtype=jnp.int32) // 2)
    tile_expert = jnp.minimum(
        jnp.searchsorted(ends, jnp.arange(n_tiles, dtype=jnp.int32) * tm, side="right"),
        N_EXPERTS - 1).astype(jnp.int32)
    n_used = (ends[-1] // tm).astype(jnp.int32).reshape(1)
    return pos.reshape(n_tok, 2), tok_of.reshape(n_tiles, 1, tm), tile_expert, n_used


def _tile_pos(pos, tm):
    t = pos.shape[0]
    return pos.reshape(t // tm, tm, 2).transpose(0, 2, 1).reshape(t // tm, 1, 2 * tm)


def kernel(x_prompt, x_sample, cache_k, cache_v, cache_logf, state_conv, page_table, w_in, b_f,
           w_conv, w_out, ln1_g, ln1_b, w_rg, b_rg, w_re, b_re, w_gate, w_up, w_down, ln2_g, ln2_b):
    depth, d_model, _ = w_in.shape
    n_batch_p, seq, _ = x_prompt.shape
    db, dseq, _ = x_sample.shape
    n_heads = cache_k.shape[3]
    a = n_heads * HEAD_DIM
    c = w_conv.shape[-1]
    assert n_batch_p == 1 and dseq <= 8 and w_conv.shape[1] == CONV_K
    alpha = (2.0 * depth) ** 0.25
    qscale = HEAD_DIM ** -0.5 * LOG2E
    t_p, t_s = seq, db * dseq
    t_all = t_p + t_s

    sel_neg = _bias_selectors(n_heads, -1.0)
    psel_k = jnp.asarray(np.concatenate(
        [np.pad(sel_neg[p], ((0, LANES - n_heads), (0, 0))) for p in range(3)], axis=0), bf16)
    psel_s = jnp.asarray(_bias_selectors(n_heads, 1.0), bf16)

    xp = x_prompt.reshape(t_p, d_model)
    xs = x_sample.reshape(t_s, d_model)
    outs = {k: [] for k in ("kp", "vp", "fp", "cp", "ks", "vs", "fs", "cs")}
    for l in range(depth):
        wl = w_in[l]
        wq, wk, wv = (wl[:, j * a:(j + 1) * a].astype(bf16) for j in range(3))
        o = 3 * a
        wf = jnp.pad(wl[:, o:o + n_heads], ((0, 0), (0, LANES - n_heads)))
        wfh = wf.astype(bf16)
        wf2 = jnp.concatenate([wfh, (wf - wfh.astype(f32)).astype(bf16)], axis=1)
        bfp = jnp.pad(b_f[l], (0, LANES - n_heads)).reshape(1, LANES)
        o += n_heads
        wb, wc, wh = (wl[:, o + j * c:o + (j + 1) * c].astype(bf16) for j in range(3))
        wo = w_out[l].astype(bf16)
        wr = jnp.pad(jnp.concatenate([w_rg[l], w_re[l]], axis=1),
                     ((0, 0), (0, LANES - N_GROUPS - N_EXPERTS)))
        wrh = wr.astype(bf16)
        wr2 = jnp.concatenate([wrh, (wr - wrh.astype(f32)).astype(bf16)], axis=1)
        br = jnp.pad(jnp.concatenate([b_rg[l], b_re[l]]), (0, LANES - N_GROUPS - N_EXPERTS)).reshape(1, LANES)
        g1, b1 = ln1_g[l].reshape(1, -1), ln1_b[l].reshape(1, -1)
        g2, b2 = ln2_g[l].reshape(1, -1), ln2_b[l].reshape(1, -1)

        q, ko, vo, kb, vb, lf, kx = _qkv_proj(xp, wq, wk, wv, wf2, wfh, bfp, psel_k,
                                              n_heads=n_heads, qscale=qscale, with_cum=True)
        conv, utail = _conv_proj(xp, wb, wc, wh, w_conv[l])
        attn = _prompt_attn(q, kb, kx, vb, n_heads=n_heads)
        outs["kp"].append(ko.reshape(1, t_p, n_heads, HEAD_DIM))
        outs["vp"].append(vo.reshape(1, t_p, n_heads, HEAD_DIM))
        outs["fp"].append(lf[:, :n_heads].reshape(1, t_p, n_heads))
        outs["cp"].append(utail[8 - (CONV_K - 1):].reshape(1, CONV_K - 1, c))

        qs, kos, vos, _, _, lfs = _qkv_proj(xs, wq, wk, wv, wf2, wfh, bfp, None,
                                            n_heads=n_heads, qscale=qscale, with_cum=False)
        st = state_conv[l]
        zero = jnp.zeros((db, dseq, c), f32)
        p1 = zero.at[:, 0].set(st[:, 1]).reshape(t_s, c)
        p2 = zero.at[:, 0].set(st[:, 0]).at[:, 1].set(st[:, 1]).reshape(t_s, c)
        conv_s, u_s = _conv_proj(xs, wb, wc, wh, w_conv[l], p1, p2, group=dseq)
        q8 = jnp.pad(qs.reshape(db, dseq, a), ((0, 0), (0, 8 - dseq), (0, 0))).astype(f32)
        attn_s = _sample_attn(page_table, q8,
                              kos.reshape(db, dseq, n_heads, HEAD_DIM),
                              vos.reshape(db, dseq, n_heads, HEAD_DIM),
                              lfs[:, :n_heads].reshape(db, dseq, n_heads),
                              cache_k[l], cache_v[l], cache_logf[l], psel_s, n_heads=n_heads)
        attn_s = attn_s[:, :dseq].reshape(t_s, a).astype(bf16)
        outs["ks"].append(kos.reshape(db, dseq, n_heads, HEAD_DIM))
        outs["vs"].append(vos.reshape(db, dseq, n_heads, HEAD_DIM))
        outs["fs"].append(lfs[:, :n_heads].reshape(db, dseq, n_heads))
        outs["cs"].append(u_s.reshape(db, dseq, c)[:, dseq - (CONV_K - 1):])
        x1, ids, gw = _finish1(attn, conv, xp, attn_s, conv_s, xs, wo, g1, b1, wr2, wrh, br,
                               alpha=alpha)

        pos, tok_of, tile_expert, n_used = _route_tables(ids[:, :2], t_all, TM_MOE)
        ys = _moe(x1, tok_of, tile_expert, n_used, w_gate[l], w_up[l], w_down[l])
        tm2 = min(TM_PROJ, t_s)
        xp = _finish2(ys, _tile_pos(pos[:t_p], TM_PROJ), x1, gw, g2, b2, alpha=alpha, row0=0, t=t_p)
        xs = _finish2(ys, _tile_pos(pos[t_p:], tm2), x1, gw, g2, b2, alpha=alpha, row0=t_p, t=t_s)

    stack = lambda k: jnp.stack(outs[k])
    return (xp.reshape(x_prompt.shape), xs.reshape(x_sample.shape),
            stack("kp"), stack("vp"), stack("fp"), stack("cp"),
            stack("ks"), stack("vs"), stack("fs"), stack("cs"))
```

```python
import functools
import math

import numpy as np
import jax
import jax.numpy as jnp
from jax import lax
from jax.experimental import pallas as pl
from jax.experimental.pallas import tpu as pltpu

f32 = jnp.float32
bf16 = jnp.bfloat16

LANES = 128
HEAD_DIM = 128
LOG2E = 1.4426950408889634
LN_EPS = 1e-5
NEG = -1e30
CONV_K = 3
N_GROUPS = 4
EXPERTS_PER_GROUP = 8
N_EXPERTS = N_GROUPS * EXPERTS_PER_GROUP
VMEM_LIMIT = 56 * 1024 * 1024

TM_PROJ = 256
TQ = 512
N_SUB = 2
PAGE_UNROLL = 4
TM_MOE = 256


def _dot(a, b):
    return jnp.dot(a, b, preferred_element_type=f32)


def _dot_nt(a, b):
    return lax.dot_general(a, b, (((1,), (1,)), ((), ())), preferred_element_type=f32)


def _lane_tile(x, n):
    return jnp.concatenate([x] * n, axis=-1)


def _split3(x):
    hi = x.astype(bf16)
    r1 = x - hi.astype(f32)
    mid = r1.astype(bf16)
    lo = (r1 - mid.astype(f32)).astype(bf16)
    return hi, mid, lo


def _params(sem=("arbitrary",)):
    return pltpu.CompilerParams(dimension_semantics=sem, vmem_limit_bytes=VMEM_LIMIT)


def _resident(shape):
    nd = len(shape)
    return pl.BlockSpec(shape, lambda *_: (0,) * nd, pipeline_mode=pl.Buffered(1))


def _qkv_kernel(x_ref, wq_ref, wk_ref, wv_ref, wf2_ref, wfh_ref, bf_ref, psel_ref,
                q_ref, ko_ref, vo_ref, kb_ref, vb_ref, lf_ref, kx_ref, carry_ref,
                *, tm, n_heads, qscale):
    i = pl.program_id(0)
    x = x_ref[...]
    xb = x.astype(bf16)
    q_ref[...] = (_dot_nt(xb, wq_ref[...]) * qscale).astype(bf16)
    k = _dot_nt(xb, wk_ref[...])
    kb_ref[...] = k.astype(bf16)
    v = _dot_nt(xb, wv_ref[...])
    vb_ref[...] = v.astype(bf16)
    for h in range(n_heads):
        ko_ref[pl.ds(h, tm, stride=n_heads), :] = k[:, h * HEAD_DIM:(h + 1) * HEAD_DIM]
        vo_ref[pl.ds(h, tm, stride=n_heads), :] = v[:, h * HEAD_DIM:(h + 1) * HEAD_DIM]

    xl = (x - xb.astype(f32)).astype(bf16)
    zf2 = _dot_nt(xb, wf2_ref[...])
    zf = zf2[:, :LANES] + zf2[:, LANES:] + _dot_nt(xl, wfh_ref[...]) + bf_ref[...]
    lf = jnp.minimum(zf, 0.0) - jnp.log1p(jnp.exp(-jnp.abs(zf)))
    lf_ref[...] = lf

    if kx_ref is not None:
        @pl.when(i == 0)
        def _():
            carry_ref[...] = jnp.zeros(carry_ref.shape, f32)

        row = lax.broadcasted_iota(jnp.int32, (tm, tm), 0)
        col = lax.broadcasted_iota(jnp.int32, (tm, tm), 1)
        tri = jnp.where(row >= col, 1.0, 0.0).astype(bf16)
        c3 = _dot(tri, jnp.concatenate(_split3(lf), axis=-1))
        cum = (c3[:, :LANES] + c3[:, LANES:2 * LANES] + c3[:, 2 * LANES:]) + carry_ref[0:1, :]
        carry_ref[...] = jnp.broadcast_to(cum[tm - 1:tm, :], carry_ref.shape)
        parts = jnp.concatenate(_split3(cum * LOG2E), axis=-1)
        kx_ref[...] = _dot(parts, psel_ref[...]).astype(bf16)


def _qkv_proj(x, wq, wk, wv, wf2, wfh, bfp, psel, *, n_heads, qscale, with_cum):
    t, d = x.shape
    a = wq.shape[0]
    tm = min(TM_PROJ, t)
    grid = (t // tm,)
    row_blk = lambda w: pl.BlockSpec((tm, w), lambda i: (i, 0))
    in_specs = [row_blk(d), _resident(wq.shape), _resident(wk.shape), _resident(wv.shape),
                _resident(wf2.shape), _resident(wfh.shape), _resident(bfp.shape)]
    args = [x, wq, wk, wv, wf2, wfh, bfp]
    out_shape = [jax.ShapeDtypeStruct((t, a), bf16),
                 jax.ShapeDtypeStruct((t * n_heads, HEAD_DIM), f32),
                 jax.ShapeDtypeStruct((t * n_heads, HEAD_DIM), f32),
                 jax.ShapeDtypeStruct((t, a), bf16),
                 jax.ShapeDtypeStruct((t, a), bf16),
                 jax.ShapeDtypeStruct((t, LANES), f32)]
    hd_blk = pl.BlockSpec((tm * n_heads, HEAD_DIM), lambda i: (i, 0))
    out_specs = [row_blk(a), hd_blk, hd_blk, row_blk(a), row_blk(a), row_blk(LANES)]
    scratch = []
    if with_cum:
        in_specs.append(_resident(psel.shape))
        args.append(psel)
        out_shape.append(jax.ShapeDtypeStruct((t, a), bf16))
        out_specs.append(row_blk(a))
        scratch.append(pltpu.VMEM((8, LANES), f32))

        def body(*refs):
            _qkv_kernel(*refs, tm=tm, n_heads=n_heads, qscale=qscale)
    else:
        def body(x_ref, wq_ref, wk_ref, wv_ref, wf2_ref, wfh_ref, bf_ref, *outs):
            _qkv_kernel(x_ref, wq_ref, wk_ref, wv_ref, wf2_ref, wfh_ref, bf_ref, None,
                        *outs, None, None, tm=tm, n_heads=n_heads, qscale=qscale)

    return pl.pallas_call(
        body, grid=grid, in_specs=in_specs, out_specs=out_specs, out_shape=out_shape,
        scratch_shapes=scratch, compiler_params=_params(), name="qkv_proj")(*args)


def _conv_kernel(x_ref, wb_ref, wc_ref, wh_ref, wconv_ref, p1_ref, p2_ref,
                 conv_ref, u_ref, ubuf, *, tm, group):
    i = pl.program_id(0)
    xb = x_ref[...].astype(bf16)
    gate_b = _dot_nt(xb, wb_ref[...])
    u = _dot_nt(xb, wc_ref[...]) * _dot_nt(xb, wh_ref[...])
    w0 = wconv_ref[0:1, :]
    w1 = wconv_ref[1:2, :]
    w2 = wconv_ref[2:3, :]
    if group is None:
        @pl.when(i == 0)
        def _():
            ubuf[0:8, :] = jnp.zeros((8, ubuf.shape[1]), f32)

        ubuf[8:8 + tm, :] = u
        u1 = ubuf[pl.ds(7, tm), :]
        u2 = ubuf[pl.ds(6, tm), :]
        ubuf[0:8, :] = u[tm - 8:tm, :]
        u_ref[...] = u[tm - 8:tm, :]
    else:
        ubuf[0:8, :] = jnp.zeros((8, ubuf.shape[1]), f32)
        ubuf[8:8 + tm, :] = u
        t_in = lax.broadcasted_iota(jnp.int32, (tm, 1), 0) % group
        u1 = jnp.where(t_in >= 1, ubuf[pl.ds(7, tm), :], p1_ref[...])
        u2 = jnp.where(t_in >= 2, ubuf[pl.ds(6, tm), :], p2_ref[...])
        u_ref[...] = u
    y = (u2 * w0 + u1 * w1) + u * w2
    conv_ref[...] = (gate_b * y).astype(bf16)


def _conv_proj(x, wb, wc, wh, wconv, p1=None, p2=None, *, group=None):
    t, d = x.shape
    c = wb.shape[0]
    tm = min(TM_PROJ, t)
    grid = (t // tm,)
    row_blk = lambda w: pl.BlockSpec((tm, w), lambda i: (i, 0))
    in_specs = [row_blk(d), _resident(wb.shape), _resident(wc.shape), _resident(wh.shape),
                _resident(wconv.shape)]
    args = [x, wb, wc, wh, wconv]
    if group is None:
        u_shape = jax.ShapeDtypeStruct((8, c), f32)
        u_spec = pl.BlockSpec((8, c), lambda i: (0, 0))

        def body(x_ref, wb_ref, wc_ref, wh_ref, wconv_ref, conv_ref, u_ref, ubuf):
            _conv_kernel(x_ref, wb_ref, wc_ref, wh_ref, wconv_ref, None, None,
                         conv_ref, u_ref, ubuf, tm=tm, group=None)
    else:
        in_specs += [row_blk(c), row_blk(c)]
        args += [p1, p2]
        u_shape = jax.ShapeDtypeStruct((t, c), f32)
        u_spec = row_blk(c)

        def body(*refs):
            _conv_kernel(*refs, tm=tm, group=group)

    return pl.pallas_call(
        body, grid=grid, in_specs=in_specs,
        out_specs=[row_blk(c), u_spec],
        out_shape=[jax.ShapeDtypeStruct((t, c), bf16), u_shape],
        scratch_shapes=[pltpu.VMEM((tm + 8, c), f32)],
        compiler_params=_params(), name="conv_proj")(*args)


def _prompt_attn_kernel(q_ref, k_ref, kx_ref, v_ref, o_ref, *scratch, tq, n_sub):
    m_scr, acc_scr = scratch[:n_sub], scratch[n_sub:2 * n_sub]
    s_a, s_b = scratch[2 * n_sub:3 * n_sub], scratch[3 * n_sub:]
    i = pl.program_id(1)
    ts = tq // n_sub
    ones_q = jnp.ones((ts, LANES), bf16)
    qa = [jnp.concatenate([q_ref[u * ts:(u + 1) * ts, :], ones_q], axis=-1) for u in range(n_sub)]
    ones_v = jnp.ones((tq, LANES), bf16)
    for u in range(n_sub):
        m_scr[u][...] = jnp.full(m_scr[u].shape, NEG, f32)
        acc_scr[u][...] = jnp.zeros(acc_scr[u].shape, f32)

    def scores(j, s_buf):
        ks = pl.multiple_of(j * tq, tq)
        ka = jnp.concatenate([k_ref[pl.ds(ks, tq), :], kx_ref[pl.ds(ks, tq), :]], axis=-1)
        for u in range(n_sub):
            s_buf[u][...] = _dot_nt(qa[u], ka)

    def accumulate(j, s_buf, masked):
        ks = pl.multiple_of(j * tq, tq)
        va = jnp.concatenate([v_ref[pl.ds(ks, tq), :], ones_v], axis=-1)
        for u in range(n_sub):
            s = s_buf[u][...]
            if masked:
                row = lax.broadcasted_iota(jnp.int32, (ts, tq), 0) + u * ts
                col = lax.broadcasted_iota(jnp.int32, (ts, tq), 1)
                s = jnp.where(col <= row, s, NEG)
            m_prev = m_scr[u][...]
            m_new = jnp.maximum(m_prev, jnp.max(s, axis=1, keepdims=True))
            p = jnp.exp2(s - _lane_tile(m_new, tq // LANES))
            alpha = jnp.exp2(m_prev - m_new)
            acc_scr[u][...] = acc_scr[u][...] * _lane_tile(alpha, 2) + _dot(p.astype(bf16), va)
            m_scr[u][...] = m_new

    scores(0, s_a)

    def body(jj, c):
        j = 2 * jj
        scores(j + 1, s_b)
        accumulate(j, s_a, False)
        scores(j + 2, s_a)
        accumulate(j + 1, s_b, False)
        return c

    lax.fori_loop(0, i // 2, body, 0)

    @pl.when(i % 2 == 0)
    def _():
        accumulate(i, s_a, True)

    @pl.when(i % 2 == 1)
    def _():
        scores(i, s_b)
        accumulate(i - 1, s_a, False)
        accumulate(i, s_b, True)

    for u in range(n_sub):
        acc = acc_scr[u][...]
        o_ref[u * ts:(u + 1) * ts, :] = (acc[:, :LANES] / acc[:, LANES:]).astype(o_ref.dtype)


def _prompt_attn(q, k, kx, v, *, n_heads):
    s = q.shape[0]
    tq = min(TQ, s)
    grid = (n_heads, s // tq)
    q_blk = pl.BlockSpec((tq, HEAD_DIM), lambda h, i: (i, h))
    kv_blk = pl.BlockSpec((s, HEAD_DIM), lambda h, i: (0, h))
    return pl.pallas_call(
        functools.partial(_prompt_attn_kernel, tq=tq, n_sub=N_SUB),
        grid=grid, in_specs=[q_blk, kv_blk, kv_blk, kv_blk], out_specs=q_blk,
        out_shape=jax.ShapeDtypeStruct(q.shape, bf16),
        scratch_shapes=([pltpu.VMEM((tq // N_SUB, LANES), f32)] * N_SUB
                        + [pltpu.VMEM((tq // N_SUB, 2 * LANES), f32)] * N_SUB
                        + [pltpu.VMEM((tq // N_SUB, tq), f32)] * (2 * N_SUB)),
        compiler_params=_params(("arbitrary", "arbitrary")), name="prompt_attn")(q, k, kx, v)


def _sample_attn_kernel(pt_ref, q_ref, knew_hbm, vnew_hbm, lnew_hbm, ck_hbm, cv_hbm, cl_hbm,
                        o_ref, kbuf, vbuf, lbuf, bias_scr, sbuf, sem,
                        *, n_pages, n_batch, n_heads, n_new, page):
    b = pl.program_id(0)
    slot = b % 2
    npg = n_pages + 1
    rows = page * n_heads
    new_rows = n_new * n_heads

    def copies(bb, sl):
        cps = []
        for p in range(n_pages):
            pg = pt_ref[bb * n_pages + p]
            src = pl.ds(pl.multiple_of(pg * rows, rows), rows)
            dst = pl.ds(p * rows, rows)
            cps.append(pltpu.make_async_copy(ck_hbm.at[src], kbuf.at[sl, dst], sem.at[sl, 0]))
            cps.append(pltpu.make_async_copy(cv_hbm.at[src], vbuf.at[sl, dst], sem.at[sl, 1]))
            cps.append(pltpu.make_async_copy(cl_hbm.at[pg], lbuf.at[sl, p], sem.at[sl, 2]))
        src = pl.ds(pl.multiple_of(bb * new_rows, new_rows), new_rows)
        dst = pl.ds(n_pages * rows, new_rows)
        cps.append(pltpu.make_async_copy(knew_hbm.at[src], kbuf.at[sl, dst], sem.at[sl, 0]))
        cps.append(pltpu.make_async_copy(vnew_hbm.at[src], vbuf.at[sl, dst], sem.at[sl, 1]))
        cps.append(pltpu.make_async_copy(lnew_hbm.at[bb], lbuf.at[sl, n_pages], sem.at[sl, 2]))
        return cps

    @pl.when(b == 0)
    def _():
        for sl in range(2):
            kbuf[sl, pl.ds(n_pages * rows, rows), :] = jnp.zeros((rows, HEAD_DIM), f32)
            vbuf[sl, pl.ds(n_pages * rows, rows), :] = jnp.zeros((rows, HEAD_DIM), f32)
        for cp in copies(0, 0):
            cp.start()

    @pl.when(b + 1 < n_batch)
    def _():
        for cp in copies(b + 1, 1 - slot):
            cp.start()

    for cp in copies(b, slot):
        cp.wait()

    row = lax.broadcasted_iota(jnp.int32, (page, page), 0)
    col = lax.broadcasted_iota(jnp.int32, (page, page), 1)
    after = jnp.where(row > col, 1.0, 0.0).astype(bf16)
    lf_all = lbuf[slot].reshape(npg * n_heads, page)
    hi, mid, lo = _split3(lf_all)
    suf = (_dot(hi, after) + _dot(mid, after)) + _dot(lo, after)
    tot = jnp.sum(lf_all, axis=1, keepdims=True)
    carry = jnp.zeros((n_heads, 1), f32)
    for p in range(npg - 1, -1, -1):
        rs = slice(p * n_heads, (p + 1) * n_heads)
        bias_scr[p] = (suf[rs, :] + carry) * LOG2E
        carry = carry + tot[rs, :]

    qb = q_ref[0].astype(bf16)
    qh = [qb[:, h * HEAD_DIM:(h + 1) * HEAD_DIM] for h in range(n_heads)]
    t_row = lax.broadcasted_iota(jnp.int32, (8, page), 0)
    lane = lax.broadcasted_iota(jnp.int32, (8, page), 1)

    def head_rows(buf, p, h):
        return buf[slot, pl.ds(p * rows + h, page, stride=n_heads), :].astype(bf16)

    def score_page(p, mvec, masked):
        out = []
        for h in range(n_heads):
            s = _dot_nt(qh[h], head_rows(kbuf, p, h)) + bias_scr[p, pl.ds(h, 1), :]
            if masked:
                s = jnp.where(lane <= t_row, s, NEG)
            sbuf[p, h] = s
            out.append(jnp.maximum(mvec[h], s))
        return tuple(out)

    unroll = math.gcd(n_pages, PAGE_UNROLL)
    mvec = lax.fori_loop(0, n_pages, functools.partial(score_page, masked=False),
                         tuple(jnp.full((8, page), NEG, f32) for _ in range(n_heads)),
                         unroll=unroll)
    mvec = score_page(n_pages, mvec, True)
    m = [jnp.max(mv, axis=1, keepdims=True) for mv in mvec]

    def value_page(p, carry):
        lvec, acc = carry
        lout, aout = [], []
        for h in range(n_heads):
            pr = jnp.exp2(sbuf[p, h] - m[h])
            lout.append(lvec[h] + pr)
            aout.append(acc[h] + _dot(pr.astype(bf16), head_rows(vbuf, p, h)))
        return tuple(lout), tuple(aout)

    zeros = tuple(jnp.zeros((8, page), f32) for _ in range(n_heads))
    lvec, acc = lax.fori_loop(0, n_pages, value_page, (zeros, zeros), unroll=unroll)
    lvec, acc = value_page(n_pages, (lvec, acc))
    for h in range(n_heads):
        l = jnp.sum(lvec[h], axis=1, keepdims=True)
        o_ref[0, :, h * HEAD_DIM:(h + 1) * HEAD_DIM] = acc[h] / l


def _sample_attn(page_table, q8, knew, vnew, lnew, ck, cv, cl, *, n_heads, n_new):
    n_batch, n_pages = page_table.shape
    page = cl.shape[-1]
    assert page == HEAD_DIM == LANES
    a = q8.shape[-1]
    rows = (n_pages + 1) * page * n_heads
    any_spec = pl.BlockSpec(memory_space=pl.ANY)
    q_blk = pl.BlockSpec((1, 8, a), lambda b, pt: (b, 0, 0))
    grid_spec = pltpu.PrefetchScalarGridSpec(
        num_scalar_prefetch=1, grid=(n_batch,),
        in_specs=[q_blk, any_spec, any_spec, any_spec, any_spec, any_spec, any_spec],
        out_specs=q_blk,
        scratch_shapes=[
            pltpu.VMEM((2, rows, HEAD_DIM), f32),
            pltpu.VMEM((2, rows, HEAD_DIM), f32),
            pltpu.VMEM((2, n_pages + 1, n_heads, page), f32),
            pltpu.VMEM((n_pages + 1, n_heads, page), f32),
            pltpu.VMEM((n_pages + 1, n_heads, 8, page), f32),
            pltpu.SemaphoreType.DMA((2, 3)),
        ])
    return pl.pallas_call(
        functools.partial(_sample_attn_kernel, n_pages=n_pages, n_batch=n_batch,
                          n_heads=n_heads, n_new=n_new, page=page),
        grid_spec=grid_spec, out_shape=jax.ShapeDtypeStruct(q8.shape, f32),
        compiler_params=_params(), name="sample_attn")(
            page_table.reshape(-1), q8, knew, vnew, lnew, ck, cv, cl)


def _layer_norm(y, g, b):
    mu = jnp.mean(y, axis=-1, keepdims=True)
    yc = y - mu
    var = jnp.mean(yc * yc, axis=-1, keepdims=True)
    return yc * lax.rsqrt(var + LN_EPS) * g + b


def _finish1_kernel(ap_ref, cp_ref, xp_ref, as_ref, cs_ref, xs_ref,
                    woa_ref, woc_ref, g_ref, b_ref, wr2_ref, wrh_ref, br_ref,
                    x1_ref, x1g_ref, ids_ref, gw_ref, *, alpha, nb_prompt, tm):
    is_prompt = pl.program_id(0) < nb_prompt
    attn = jnp.where(is_prompt, ap_ref[...], as_ref[...])
    conv = jnp.where(is_prompt, cp_ref[...], cs_ref[...])
    x = jnp.where(is_prompt, xp_ref[...], xs_ref[...])
    o = _dot(attn, woa_ref[...]) + _dot(conv, woc_ref[...])
    x1 = _layer_norm(alpha * x + o, g_ref[...], b_ref[...])
    x1_ref[...] = x1
    n_chunks = x1.shape[1] // LANES
    for c in range(n_chunks):
        x1g_ref[pl.ds(c, tm, stride=n_chunks), :] = x1[:, c * LANES:(c + 1) * LANES]

    x1b = x1.astype(bf16)
    x1l = (x1 - x1b.astype(f32)).astype(bf16)
    r2 = _dot(x1b, wr2_ref[...])
    lr = r2[:, :LANES] + r2[:, LANES:] + _dot(x1l, wrh_ref[...]) + br_ref[...]
    lane = lax.broadcasted_iota(jnp.int32, lr.shape, 1)

    def first_max(mask):
        vals = jnp.where(mask, lr, -jnp.inf)
        mx = jnp.max(vals, axis=-1, keepdims=True)
        idx = jnp.min(jnp.where(mask & (vals == mx), lane, LANES), axis=-1, keepdims=True)
        return mx, idx

    is_group = lane < N_GROUPS
    gmax, gidx = first_max(is_group)
    p_group = 1.0 / jnp.sum(jnp.where(is_group, jnp.exp(lr - gmax), 0.0), axis=-1, keepdims=True)
    lo = N_GROUPS + gidx * EXPERTS_PER_GROUP
    in_group = (lane >= lo) & (lane < lo + EXPERTS_PER_GROUP)
    e1, i1 = first_max(in_group)
    e2, i2 = first_max(in_group & (lane != i1))
    r = jnp.exp(e2 - e1)
    w1 = p_group / (1.0 + r)
    w2 = p_group * r / (1.0 + r)
    ids_ref[...] = jnp.where(lane == 0, i1 - N_GROUPS, jnp.where(lane == 1, i2 - N_GROUPS, 0))
    gw_ref[...] = jnp.where(lane == 0, w1, jnp.where(lane == 1, w2, 0.0))


def _finish1(attn_p, conv_p, x_p, attn_s, conv_s, x_s, wo, g, b, wr2, wrh, br, *, alpha):
    (t_p, d), t_s = x_p.shape, x_s.shape[0]
    a = attn_p.shape[1]
    assert wo.shape[0] == 2 * a
    tm = min(TM_PROJ, t_s)
    nb_p, nb_s = t_p // tm, t_s // tm
    t_all = t_p + t_s
    p_blk = lambda w: pl.BlockSpec((tm, w), lambda i: (jnp.minimum(i, nb_p - 1), 0))
    s_blk = lambda w: pl.BlockSpec((tm, w), lambda i: (jnp.maximum(i - nb_p, 0), 0))
    out_blk = lambda w: pl.BlockSpec((tm, w), lambda i: (i, 0))
    in_specs = [p_blk(a), p_blk(a), p_blk(d), s_blk(a), s_blk(a), s_blk(d),
                pl.BlockSpec((a, d), lambda i: (0, 0), pipeline_mode=pl.Buffered(1)),
                pl.BlockSpec((a, d), lambda i: (1, 0), pipeline_mode=pl.Buffered(1)),
                _resident(g.shape), _resident(b.shape), _resident(wr2.shape),
                _resident(wrh.shape), _resident(br.shape)]
    n_chunks = d // LANES
    return pl.pallas_call(
        functools.partial(_finish1_kernel, alpha=alpha, nb_prompt=nb_p, tm=tm),
        grid=(nb_p + nb_s,), in_specs=in_specs,
        out_specs=[out_blk(d), pl.BlockSpec((tm * n_chunks, LANES), lambda i: (i, 0)),
                   out_blk(LANES), out_blk(LANES)],
        out_shape=[jax.ShapeDtypeStruct((t_all, d), f32),
                   jax.ShapeDtypeStruct((t_all * n_chunks, LANES), f32),
                   jax.ShapeDtypeStruct((t_all, LANES), jnp.int32),
                   jax.ShapeDtypeStruct((t_all, LANES), f32)],
        compiler_params=_params(), name="finish1")(
            attn_p, conv_p, x_p, attn_s, conv_s, x_s, wo, wo, g, b, wr2, wrh, br)


def _moe_kernel(te_ref, nu_ref, idx_ref, x_hbm, wg_ref, wu_ref, wd_ref, y_ref,
                xbuf, wgb, wub, wdb, sem, *, tm):
    s = pl.program_id(0)
    n_used = nu_ref[0]

    @pl.when(s < n_used)
    def _():
        slot = s % 2

        def issue(r, c):
            tok = idx_ref[0, 0, r]
            pltpu.make_async_copy(x_hbm.at[tok], xbuf.at[slot, :, r, :], sem.at[slot]).start()
            return c

        lax.fori_loop(0, tm, issue, 0, unroll=8)

    t = s - 1

    @pl.when((t >= 0) & (t < n_used))
    def _():
        slot = t % 2
        pltpu.make_async_copy(xbuf.at[slot], xbuf.at[slot], sem.at[slot]).wait()
        new_expert = (t == 0) | (te_ref[t] != te_ref[jnp.maximum(t - 1, 0)])

        @pl.when(new_expert)
        def _():
            wgb[...] = wg_ref[...].astype(bf16)
            wub[...] = wu_ref[...].astype(bf16)
            wdb[...] = wd_ref[...].astype(bf16)

        xb = jnp.concatenate([xbuf[slot, c] for c in range(xbuf.shape[1])], axis=-1).astype(bf16)
        hg = _dot(xb, wgb[...])
        hu = _dot(xb, wub[...])
        act = hg * (1.0 / (1.0 + jnp.exp(-hg))) * hu
        y_ref[...] = _dot(act.astype(bf16), wdb[...])

    @pl.when(t >= n_used)
    def _():
        y_ref[...] = jnp.zeros(y_ref.shape, f32)


def _moe(x1g, tok_of, tile_expert, n_used, wg, wu, wd):
    n_tiles, _, tm = tok_of.shape
    _, d, f = wg.shape
    last = n_tiles - 1

    def w_idx(s, te, nu):
        t = jnp.clip(s - 1, 0, jnp.maximum(nu[0] - 1, 0))
        return (te[t], 0, 0)

    def y_idx(s, te, nu):
        return (jnp.maximum(s - 1, 0), 0)

    grid_spec = pltpu.PrefetchScalarGridSpec(
        num_scalar_prefetch=2, grid=(n_tiles + 1,),
        in_specs=[pl.BlockSpec((1, 1, tm), lambda s, te, nu: (jnp.minimum(s, last), 0, 0),
                               memory_space=pltpu.SMEM),
                  pl.BlockSpec(memory_space=pl.ANY),
                  pl.BlockSpec((None, d, f), w_idx),
                  pl.BlockSpec((None, d, f), w_idx),
                  pl.BlockSpec((None, f, d), w_idx)],
        out_specs=pl.BlockSpec((tm, d), y_idx),
        scratch_shapes=[pltpu.VMEM((2, d // LANES, tm, LANES), f32),
                        pltpu.VMEM((d, f), bf16), pltpu.VMEM((d, f), bf16), pltpu.VMEM((f, d), bf16),
                        pltpu.SemaphoreType.DMA((2,))])
    return pl.pallas_call(
        functools.partial(_moe_kernel, tm=tm),
        grid_spec=grid_spec, out_shape=jax.ShapeDtypeStruct((n_tiles * tm, d), f32),
        compiler_params=_params(), name="moe")(tile_expert, n_used, tok_of, x1g, wg, wu, wd)


def _finish2_kernel(idx_ref, ys_hbm, x1_ref, gw_ref, g_ref, b_ref, o_ref, gbuf, sem,
                    *, tm, n_steps, alpha):
    s = pl.program_id(0)

    @pl.when(s < n_steps)
    def _():
        slot = s % 2

        def issue(r, c):
            row = idx_ref[0, 0, r]
            pltpu.make_async_copy(ys_hbm.at[pl.ds(row, 1)], gbuf.at[slot, pl.ds(r, 1)],
                                  sem.at[slot]).start()
            return c

        lax.fori_loop(0, 2 * tm, issue, 0, unroll=8)

    @pl.when(s >= 1)
    def _():
        slot = (s - 1) % 2
        pltpu.make_async_copy(ys_hbm.at[pl.ds(0, 2 * tm)], gbuf.at[slot], sem.at[slot]).wait()
        gw = gw_ref[...]
        f = gw[:, 0:1] * gbuf[slot, 0:tm, :] + gw[:, 1:2] * gbuf[slot, tm:2 * tm, :]
        o_ref[...] = _layer_norm(alpha * x1_ref[...] + f, g_ref[...], b_ref[...])


def _finish2(ys, pos, x1, gw, g, b, *, alpha, row0, t):
    n_steps, _, two_tm = pos.shape
    tm = two_tm // 2
    d = x1.shape[1]
    blk0 = row0 // tm
    last = n_steps - 1
    in_blk = lambda w: pl.BlockSpec((tm, w), lambda s: (jnp.maximum(s - 1, 0) + blk0, 0))
    return pl.pallas_call(
        functools.partial(_finish2_kernel, tm=tm, n_steps=n_steps, alpha=alpha),
        grid=(n_steps + 1,),
        in_specs=[pl.BlockSpec((1, 1, two_tm), lambda s: (jnp.minimum(s, last), 0, 0),
                               memory_space=pltpu.SMEM),
                  pl.BlockSpec(memory_space=pl.ANY),
                  in_blk(d), in_blk(LANES), _resident(g.shape), _resident(b.shape)],
        out_specs=pl.BlockSpec((tm, d), lambda s: (jnp.maximum(s - 1, 0), 0)),
        out_shape=jax.ShapeDtypeStruct((t, d), f32),
        scratch_shapes=[pltpu.VMEM((2, two_tm, d), f32), pltpu.SemaphoreType.DMA((2,))],
        compiler_params=_params(), name="finish2")(pos, ys, x1, gw, g, b)


def _bias_selectors(n_heads, sign):
    sel = np.zeros((3, n_heads, n_heads * HEAD_DIM), np.float32)
    for p in range(3):
        for h in range(n_heads):
            sel[p, h, h * HEAD_DIM + p] = sign
    return sel


def _route_tables(ids, n_tok, tm):
    e = ids.reshape(-1)
    n_pairs = e.shape[0]
    onehot = (e[:, None] == jnp.arange(N_EXPERTS, dtype=jnp.int32)[None, :]).astype(jnp.int32)
    csum = jnp.cumsum(onehot, axis=0)
    rank = jnp.take_along_axis(csum, e[:, None], axis=1)[:, 0] - 1
    counts = csum[-1]
    padded = ((counts + tm - 1) // tm) * tm
    ends = jnp.cumsum(padded)
    pos = (ends - padded)[e] + rank
    n_tiles = (n_pairs + N_EXPERTS * (tm - 1)) // tm
    rows = n_tiles * tm
    tok_of = jnp.zeros((rows,), jnp.int32).at[pos].set(
        jnp.arange(n_pairs, dtype=jnp.int32) // 2, unique_indices=True, mode="promise_in_bounds")
    tile_start = jnp.arange(n_tiles, dtype=jnp.int32) * tm
    tile_expert = jnp.minimum(
        jnp.sum((ends[None, :] <= tile_start[:, None]).astype(jnp.int32), axis=1), N_EXPERTS - 1)
    n_used = (ends[-1] // tm).astype(jnp.int32).reshape(1)
    return pos.reshape(n_tok, 2), tok_of.reshape(n_tiles, 1, tm), tile_expert, n_used


def _tile_pos(pos, tm):
    t = pos.shape[0]
    return pos.reshape(t // tm, tm, 2).transpose(0, 2, 1).reshape(t // tm, 1, 2 * tm)


def kernel(x_prompt, x_sample, cache_k, cache_v, cache_logf, state_conv, page_table, w_in, b_f,
           w_conv, w_out, ln1_g, ln1_b, w_rg, b_rg, w_re, b_re, w_gate, w_up, w_down, ln2_g, ln2_b):
    depth, d_model, _ = w_in.shape
    n_batch_p, seq, _ = x_prompt.shape
    db, dseq, _ = x_sample.shape
    n_heads = cache_k.shape[3]
    a = n_heads * HEAD_DIM
    c = w_conv.shape[-1]
    assert n_batch_p == 1 and dseq <= 8 and w_conv.shape[1] == CONV_K
    alpha = (2.0 * depth) ** 0.25
    qscale = HEAD_DIM ** -0.5 * LOG2E
    t_p, t_s = seq, db * dseq
    t_all = t_p + t_s

    sel_neg = _bias_selectors(n_heads, -1.0)
    psel_k = jnp.asarray(np.concatenate(
        [np.pad(sel_neg[p], ((0, LANES - n_heads), (0, 0))) for p in range(3)], axis=0), bf16)

    xp = x_prompt.reshape(t_p, d_model)
    xs = x_sample.reshape(t_s, d_model)
    outs = {k: [] for k in ("kp", "vp", "fp", "cp", "ks", "vs", "fs", "cs")}
    for l in range(depth):
        wl = jnp.swapaxes(w_in[l], 0, 1)
        wq, wk, wv = (wl[j * a:(j + 1) * a].astype(bf16) for j in range(3))
        o = 3 * a
        wf = jnp.pad(wl[o:o + n_heads], ((0, LANES - n_heads), (0, 0)))
        wfh = wf.astype(bf16)
        wf2 = jnp.concatenate([wfh, (wf - wfh.astype(f32)).astype(bf16)], axis=0)
        bfp = jnp.pad(b_f[l], (0, LANES - n_heads)).reshape(1, LANES)
        o += n_heads
        wb, wc, wh = (wl[o + j * c:o + (j + 1) * c].astype(bf16) for j in range(3))
        wo = w_out[l].astype(bf16)
        wr = jnp.pad(jnp.concatenate([w_rg[l], w_re[l]], axis=1),
                     ((0, 0), (0, LANES - N_GROUPS - N_EXPERTS)))
        wrh = wr.astype(bf16)
        wr2 = jnp.concatenate([wrh, (wr - wrh.astype(f32)).astype(bf16)], axis=1)
        br = jnp.pad(jnp.concatenate([b_rg[l], b_re[l]]), (0, LANES - N_GROUPS - N_EXPERTS)).reshape(1, LANES)
        g1, b1 = ln1_g[l].reshape(1, -1), ln1_b[l].reshape(1, -1)
        g2, b2 = ln2_g[l].reshape(1, -1), ln2_b[l].reshape(1, -1)

        q, ko, vo, kb, vb, lf, kx = _qkv_proj(xp, wq, wk, wv, wf2, wfh, bfp, psel_k,
                                              n_heads=n_heads, qscale=qscale, with_cum=True)
        conv, utail = _conv_proj(xp, wb, wc, wh, w_conv[l])
        attn = _prompt_attn(q, kb, kx, vb, n_heads=n_heads)
        outs["kp"].append(ko.reshape(1, t_p, n_heads, HEAD_DIM))
        outs["vp"].append(vo.reshape(1, t_p, n_heads, HEAD_DIM))
        outs["fp"].append(lf[:, :n_heads].reshape(1, t_p, n_heads))
        outs["cp"].append(utail[8 - (CONV_K - 1):].reshape(1, CONV_K - 1, c))

        qs, kos, vos, _, _, lfs = _qkv_proj(xs, wq, wk, wv, wf2, wfh, bfp, None,
                                            n_heads=n_heads, qscale=qscale, with_cum=False)
        st = state_conv[l]
        zero = jnp.zeros((db, dseq, c), f32)
        p1 = zero.at[:, 0].set(st[:, 1]).reshape(t_s, c)
        p2 = zero.at[:, 0].set(st[:, 0]).at[:, 1].set(st[:, 1]).reshape(t_s, c)
        conv_s, u_s = _conv_proj(xs, wb, wc, wh, w_conv[l], p1, p2, group=dseq)
        q8 = jnp.pad(qs.reshape(db, dseq, a), ((0, 0), (0, 8 - dseq), (0, 0))).astype(f32)
        page = cache_k.shape[2]
        lnew = jnp.pad(jnp.swapaxes(lfs[:, :n_heads].reshape(db, dseq, n_heads), 1, 2),
                       ((0, 0), (0, 0), (0, page - dseq)))
        attn_s = _sample_attn(page_table, q8, kos, vos, lnew,
                              cache_k[l].reshape(-1, HEAD_DIM), cache_v[l].reshape(-1, HEAD_DIM),
                              jnp.swapaxes(cache_logf[l], 1, 2), n_heads=n_heads, n_new=dseq)
        attn_s = attn_s[:, :dseq].reshape(t_s, a).astype(bf16)
        outs["ks"].append(kos.reshape(db, dseq, n_heads, HEAD_DIM))
        outs["vs"].append(vos.reshape(db, dseq, n_heads, HEAD_DIM))
        outs["fs"].append(lfs[:, :n_heads].reshape(db, dseq, n_heads))
        outs["cs"].append(u_s.reshape(db, dseq, c)[:, dseq - (CONV_K - 1):])
        x1, x1g, ids, gw = _finish1(attn, conv, xp, attn_s, conv_s, xs, wo, g1, b1, wr2, wrh, br,
                                    alpha=alpha)

        pos, tok_of, tile_expert, n_used = _route_tables(ids[:, :2], t_all, TM_MOE)
        ys = _moe(x1g.reshape(t_all, d_model // LANES, LANES), tok_of, tile_expert, n_used,
                  w_gate[l], w_up[l], w_down[l])
        tm2 = min(TM_PROJ, t_s)
        xp = _finish2(ys, _tile_pos(pos[:t_p], TM_PROJ), x1, gw, g2, b2, alpha=alpha, row0=0, t=t_p)
        xs = _finish2(ys, _tile_pos(pos[t_p:], tm2), x1, gw, g2, b2, alpha=alpha, row0=t_p, t=t_s)

    stack = lambda k: jnp.stack(outs[k])
    return (xp.reshape(x_prompt.shape), xs.reshape(x_sample.shape),
            stack("kp"), stack("vp"), stack("fp"), stack("cp"),
            stack("ks"), stack("vs"), stack("fs"), stack("cs"))
```

```python
import functools
import math

import numpy as np
import jax
import jax.numpy as jnp
from jax import lax
from jax.experimental import pallas as pl
from jax.experimental.pallas import tpu as pltpu

f32 = jnp.float32
bf16 = jnp.bfloat16

LANES = 128
HEAD_DIM = 128
LOG2E = 1.4426950408889634
LN_EPS = 1e-5
NEG = -1e30
CONV_K = 3
N_GROUPS = 4
EXPERTS_PER_GROUP = 8
N_EXPERTS = N_GROUPS * EXPERTS_PER_GROUP
VMEM_LIMIT = 56 * 1024 * 1024

TM_PROJ = 256
TQ = 512
N_SUB = 2
PAGE_UNROLL = 4
TM_MOE = 256


def _dot(a, b):
    return jnp.dot(a, b, preferred_element_type=f32)


def _dot_nt(a, b):
    return lax.dot_general(a, b, (((1,), (1,)), ((), ())), preferred_element_type=f32)


def _lane_tile(x, n):
    return jnp.concatenate([x] * n, axis=-1)


def _split3(x):
    hi = x.astype(bf16)
    r1 = x - hi.astype(f32)
    mid = r1.astype(bf16)
    lo = (r1 - mid.astype(f32)).astype(bf16)
    return hi, mid, lo


def _params(sem=("arbitrary",)):
    return pltpu.CompilerParams(dimension_semantics=sem, vmem_limit_bytes=VMEM_LIMIT)


def _resident(shape):
    nd = len(shape)
    return pl.BlockSpec(shape, lambda *_: (0,) * nd, pipeline_mode=pl.Buffered(1))


def _qkv_kernel(x_ref, wq_ref, wk_ref, wv_ref, wf2_ref, wfh_ref, bf_ref, psel_ref,
                q_ref, ko_ref, vo_ref, kb_ref, vb_ref, lf_ref, kx_ref, carry_ref,
                *, tm, n_heads, qscale):
    i = pl.program_id(0)
    x = x_ref[...]
    xb = x.astype(bf16)
    q_ref[...] = (_dot_nt(xb, wq_ref[...]) * qscale).astype(bf16)
    k = _dot_nt(xb, wk_ref[...])
    kb_ref[...] = k.astype(bf16)
    v = _dot_nt(xb, wv_ref[...])
    vb_ref[...] = v.astype(bf16)
    for h in range(n_heads):
        ko_ref[pl.ds(h, tm, stride=n_heads), :] = k[:, h * HEAD_DIM:(h + 1) * HEAD_DIM]
        vo_ref[pl.ds(h, tm, stride=n_heads), :] = v[:, h * HEAD_DIM:(h + 1) * HEAD_DIM]

    xl = (x - xb.astype(f32)).astype(bf16)
    zf2 = _dot_nt(xb, wf2_ref[...])
    zf = zf2[:, :LANES] + zf2[:, LANES:] + _dot_nt(xl, wfh_ref[...]) + bf_ref[...]
    lf = jnp.minimum(zf, 0.0) - jnp.log1p(jnp.exp(-jnp.abs(zf)))
    lf_ref[...] = lf

    if kx_ref is not None:
        @pl.when(i == 0)
        def _():
            carry_ref[...] = jnp.zeros(carry_ref.shape, f32)

        row = lax.broadcasted_iota(jnp.int32, (tm, tm), 0)
        col = lax.broadcasted_iota(jnp.int32, (tm, tm), 1)
        tri = jnp.where(row >= col, 1.0, 0.0).astype(bf16)
        c3 = _dot(tri, jnp.concatenate(_split3(lf), axis=-1))
        cum = (c3[:, :LANES] + c3[:, LANES:2 * LANES] + c3[:, 2 * LANES:]) + carry_ref[0:1, :]
        carry_ref[...] = jnp.broadcast_to(cum[tm - 1:tm, :], carry_ref.shape)
        parts = jnp.concatenate(_split3(cum * LOG2E), axis=-1)
        kx_ref[...] = _dot(parts, psel_ref[...]).astype(bf16)


def _qkv_proj(x, wq, wk, wv, wf2, wfh, bfp, psel, *, n_heads, qscale, with_cum):
    t, d = x.shape
    a = wq.shape[0]
    tm = min(TM_PROJ, t)
    grid = (t // tm,)
    row_blk = lambda w: pl.BlockSpec((tm, w), lambda i: (i, 0))
    in_specs = [row_blk(d), _resident(wq.shape), _resident(wk.shape), _resident(wv.shape),
                _resident(wf2.shape), _resident(wfh.shape), _resident(bfp.shape)]
    args = [x, wq, wk, wv, wf2, wfh, bfp]
    out_shape = [jax.ShapeDtypeStruct((t, a), bf16),
                 jax.ShapeDtypeStruct((t * n_heads, HEAD_DIM), f32),
                 jax.ShapeDtypeStruct((t * n_heads, HEAD_DIM), f32),
                 jax.ShapeDtypeStruct((t, a), bf16),
                 jax.ShapeDtypeStruct((t, a), bf16),
                 jax.ShapeDtypeStruct((t, LANES), f32)]
    hd_blk = pl.BlockSpec((tm * n_heads, HEAD_DIM), lambda i: (i, 0))
    out_specs = [row_blk(a), hd_blk, hd_blk, row_blk(a), row_blk(a), row_blk(LANES)]
    scratch = []
    if with_cum:
        in_specs.append(_resident(psel.shape))
        args.append(psel)
        out_shape.append(jax.ShapeDtypeStruct((t, a), bf16))
        out_specs.append(row_blk(a))
        scratch.append(pltpu.VMEM((8, LANES), f32))

        def body(*refs):
            _qkv_kernel(*refs, tm=tm, n_heads=n_heads, qscale=qscale)
    else:
        def body(x_ref, wq_ref, wk_ref, wv_ref, wf2_ref, wfh_ref, bf_ref, *outs):
            _qkv_kernel(x_ref, wq_ref, wk_ref, wv_ref, wf2_ref, wfh_ref, bf_ref, None,
                        *outs, None, None, tm=tm, n_heads=n_heads, qscale=qscale)

    return pl.pallas_call(
        body, grid=grid, in_specs=in_specs, out_specs=out_specs, out_shape=out_shape,
        scratch_shapes=scratch, compiler_params=_params(), name="qkv_proj")(*args)


def _conv_kernel(x_ref, wb_ref, wc_ref, wh_ref, wconv_ref, p1_ref, p2_ref,
                 conv_ref, u_ref, ubuf, *, tm, group):
    i = pl.program_id(0)
    xb = x_ref[...].astype(bf16)
    gate_b = _dot_nt(xb, wb_ref[...])
    u = _dot_nt(xb, wc_ref[...]) * _dot_nt(xb, wh_ref[...])
    w0 = wconv_ref[0:1, :]
    w1 = wconv_ref[1:2, :]
    w2 = wconv_ref[2:3, :]
    if group is None:
        @pl.when(i == 0)
        def _():
            ubuf[0:8, :] = jnp.zeros((8, ubuf.shape[1]), f32)

        ubuf[8:8 + tm, :] = u
        u1 = ubuf[pl.ds(7, tm), :]
        u2 = ubuf[pl.ds(6, tm), :]
        ubuf[0:8, :] = u[tm - 8:tm, :]
        u_ref[...] = u[tm - 8:tm, :]
    else:
        ubuf[0:8, :] = jnp.zeros((8, ubuf.shape[1]), f32)
        ubuf[8:8 + tm, :] = u
        t_in = lax.broadcasted_iota(jnp.int32, (tm, 1), 0) % group
        u1 = jnp.where(t_in >= 1, ubuf[pl.ds(7, tm), :], p1_ref[...])
        u2 = jnp.where(t_in >= 2, ubuf[pl.ds(6, tm), :], p2_ref[...])
        u_ref[...] = u
    y = (u2 * w0 + u1 * w1) + u * w2
    conv_ref[...] = (gate_b * y).astype(bf16)


def _conv_proj(x, wb, wc, wh, wconv, p1=None, p2=None, *, group=None):
    t, d = x.shape
    c = wb.shape[0]
    tm = min(TM_PROJ, t)
    grid = (t // tm,)
    row_blk = lambda w: pl.BlockSpec((tm, w), lambda i: (i, 0))
    in_specs = [row_blk(d), _resident(wb.shape), _resident(wc.shape), _resident(wh.shape),
                _resident(wconv.shape)]
    args = [x, wb, wc, wh, wconv]
    if group is None:
        u_shape = jax.ShapeDtypeStruct((8, c), f32)
        u_spec = pl.BlockSpec((8, c), lambda i: (0, 0))

        def body(x_ref, wb_ref, wc_ref, wh_ref, wconv_ref, conv_ref, u_ref, ubuf):
            _conv_kernel(x_ref, wb_ref, wc_ref, wh_ref, wconv_ref, None, None,
                         conv_ref, u_ref, ubuf, tm=tm, group=None)
    else:
        in_specs += [row_blk(c), row_blk(c)]
        args += [p1, p2]
        u_shape = jax.ShapeDtypeStruct((t, c), f32)
        u_spec = row_blk(c)

        def body(*refs):
            _conv_kernel(*refs, tm=tm, group=group)

    return pl.pallas_call(
        body, grid=grid, in_specs=in_specs,
        out_specs=[row_blk(c), u_spec],
        out_shape=[jax.ShapeDtypeStruct((t, c), bf16), u_shape],
        scratch_shapes=[pltpu.VMEM((tm + 8, c), f32)],
        compiler_params=_params(), name="conv_proj")(*args)


def _prompt_attn_kernel(q_ref, k_ref, kx_ref, v_ref, o_ref, *scratch, tq, n_sub):
    m_scr, acc_scr = scratch[:n_sub], scratch[n_sub:2 * n_sub]
    s_a, s_b = scratch[2 * n_sub:3 * n_sub], scratch[3 * n_sub:]
    i = pl.program_id(1)
    ts = tq // n_sub
    ones_q = jnp.ones((ts, LANES), bf16)
    qa = [jnp.concatenate([q_ref[u * ts:(u + 1) * ts, :], ones_q], axis=-1) for u in range(n_sub)]
    ones_v = jnp.ones((tq, LANES), bf16)
    for u in range(n_sub):
        m_scr[u][...] = jnp.full(m_scr[u].shape, NEG, f32)
        acc_scr[u][...] = jnp.zeros(acc_scr[u].shape, f32)

    def scores(j, s_buf):
        ks = pl.multiple_of(j * tq, tq)
        ka = jnp.concatenate([k_ref[pl.ds(ks, tq), :], kx_ref[pl.ds(ks, tq), :]], axis=-1)
        for u in range(n_sub):
            s_buf[u][...] = _dot_nt(qa[u], ka)

    def accumulate(j, s_buf, masked):
        ks = pl.multiple_of(j * tq, tq)
        va = jnp.concatenate([v_ref[pl.ds(ks, tq), :], ones_v], axis=-1)
        for u in range(n_sub):
            s = s_buf[u][...]
            if masked:
                row = lax.broadcasted_iota(jnp.int32, (ts, tq), 0) + u * ts
                col = lax.broadcasted_iota(jnp.int32, (ts, tq), 1)
                s = jnp.where(col <= row, s, NEG)
            m_prev = m_scr[u][...]
            m_new = jnp.maximum(m_prev, jnp.max(s, axis=1, keepdims=True))
            p = jnp.exp2(s - _lane_tile(m_new, tq // LANES))
            alpha = jnp.exp2(m_prev - m_new)
            acc_scr[u][...] = acc_scr[u][...] * _lane_tile(alpha, 2) + _dot(p.astype(bf16), va)
            m_scr[u][...] = m_new

    scores(0, s_a)

    def body(jj, c):
        j = 2 * jj
        scores(j + 1, s_b)
        accumulate(j, s_a, False)
        scores(j + 2, s_a)
        accumulate(j + 1, s_b, False)
        return c

    lax.fori_loop(0, i // 2, body, 0)

    @pl.when(i % 2 == 0)
    def _():
        accumulate(i, s_a, True)

    @pl.when(i % 2 == 1)
    def _():
        scores(i, s_b)
        accumulate(i - 1, s_a, False)
        accumulate(i, s_b, True)

    for u in range(n_sub):
        acc = acc_scr[u][...]
        o_ref[u * ts:(u + 1) * ts, :] = (acc[:, :LANES] / acc[:, LANES:]).astype(o_ref.dtype)


def _prompt_attn(q, k, kx, v, *, n_heads):
    s = q.shape[0]
    tq = min(TQ, s)
    grid = (n_heads, s // tq)
    q_blk = pl.BlockSpec((tq, HEAD_DIM), lambda h, i: (i, h))
    kv_blk = pl.BlockSpec((s, HEAD_DIM), lambda h, i: (0, h))
    return pl.pallas_call(
        functools.partial(_prompt_attn_kernel, tq=tq, n_sub=N_SUB),
        grid=grid, in_specs=[q_blk, kv_blk, kv_blk, kv_blk], out_specs=q_blk,
        out_shape=jax.ShapeDtypeStruct(q.shape, bf16),
        scratch_shapes=([pltpu.VMEM((tq // N_SUB, LANES), f32)] * N_SUB
                        + [pltpu.VMEM((tq // N_SUB, 2 * LANES), f32)] * N_SUB
                        + [pltpu.VMEM((tq // N_SUB, tq), f32)] * (2 * N_SUB)),
        compiler_params=_params(("arbitrary", "arbitrary")), name="prompt_attn")(q, k, kx, v)


def _sample_attn_kernel(pt_ref, q_ref, knew_hbm, vnew_hbm, lnew_hbm, ck_hbm, cv_hbm, cl_hbm,
                        o_ref, kbuf, vbuf, lbuf, bias_scr, sbuf, sem,
                        *, n_pages, n_batch, n_heads, n_new, page):
    b = pl.program_id(0)
    slot = b % 2
    npg = n_pages + 1
    rows = page * n_heads
    new_rows = n_new * n_heads

    def copies(bb, sl):
        cps = []
        for p in range(n_pages):
            pg = pt_ref[bb * n_pages + p]
            src = pl.ds(pl.multiple_of(pg * rows, rows), rows)
            dst = pl.ds(p * rows, rows)
            cps.append(pltpu.make_async_copy(ck_hbm.at[src], kbuf.at[sl, dst], sem.at[sl, 0]))
            cps.append(pltpu.make_async_copy(cv_hbm.at[src], vbuf.at[sl, dst], sem.at[sl, 1]))
            cps.append(pltpu.make_async_copy(cl_hbm.at[pg], lbuf.at[sl, p], sem.at[sl, 2]))
        src = pl.ds(pl.multiple_of(bb * new_rows, new_rows), new_rows)
        dst = pl.ds(n_pages * rows, new_rows)
        cps.append(pltpu.make_async_copy(knew_hbm.at[src], kbuf.at[sl, dst], sem.at[sl, 0]))
        cps.append(pltpu.make_async_copy(vnew_hbm.at[src], vbuf.at[sl, dst], sem.at[sl, 1]))
        cps.append(pltpu.make_async_copy(lnew_hbm.at[bb], lbuf.at[sl, n_pages], sem.at[sl, 2]))
        return cps

    @pl.when(b == 0)
    def _():
        for sl in range(2):
            kbuf[sl, pl.ds(n_pages * rows, rows), :] = jnp.zeros((rows, HEAD_DIM), f32)
            vbuf[sl, pl.ds(n_pages * rows, rows), :] = jnp.zeros((rows, HEAD_DIM), f32)
        for cp in copies(0, 0):
            cp.start()

    @pl.when(b + 1 < n_batch)
    def _():
        for cp in copies(b + 1, 1 - slot):
            cp.start()

    for cp in copies(b, slot):
        cp.wait()

    row = lax.broadcasted_iota(jnp.int32, (page, page), 0)
    col = lax.broadcasted_iota(jnp.int32, (page, page), 1)
    after = jnp.where(row > col, 1.0, 0.0).astype(bf16)
    lf_all = lbuf[slot].reshape(npg * n_heads, page)
    hi, mid, lo = _split3(lf_all)
    suf = (_dot(hi, after) + _dot(mid, after)) + _dot(lo, after)
    tot = jnp.sum(lf_all, axis=1, keepdims=True)
    carry = jnp.zeros((n_heads, 1), f32)
    for p in range(npg - 1, -1, -1):
        rs = slice(p * n_heads, (p + 1) * n_heads)
        bias_scr[p] = (suf[rs, :] + carry) * LOG2E
        carry = carry + tot[rs, :]

    qb = q_ref[0].astype(bf16)
    qh = [qb[:, h * HEAD_DIM:(h + 1) * HEAD_DIM] for h in range(n_heads)]
    t_row = lax.broadcasted_iota(jnp.int32, (8, page), 0)
    lane = lax.broadcasted_iota(jnp.int32, (8, page), 1)

    def head_rows(buf, p, h):
        return buf[slot, pl.ds(p * rows + h, page, stride=n_heads), :].astype(bf16)

    def score_page(p, mvec, masked):
        out = []
        for h in range(n_heads):
            s = _dot_nt(qh[h], head_rows(kbuf, p, h)) + bias_scr[p, pl.ds(h, 1), :]
            if masked:
                s = jnp.where(lane <= t_row, s, NEG)
            sbuf[p, h] = s
            out.append(jnp.maximum(mvec[h], s))
        return tuple(out)

    unroll = math.gcd(n_pages, PAGE_UNROLL)
    mvec = lax.fori_loop(0, n_pages, functools.partial(score_page, masked=False),
                         tuple(jnp.full((8, page), NEG, f32) for _ in range(n_heads)),
                         unroll=unroll)
    mvec = score_page(n_pages, mvec, True)
    m = [jnp.max(mv, axis=1, keepdims=True) for mv in mvec]

    def value_page(p, carry):
        lvec, acc = carry
        lout, aout = [], []
        for h in range(n_heads):
            pr = jnp.exp2(sbuf[p, h] - m[h])
            lout.append(lvec[h] + pr)
            aout.append(acc[h] + _dot(pr.astype(bf16), head_rows(vbuf, p, h)))
        return tuple(lout), tuple(aout)

    zeros = tuple(jnp.zeros((8, page), f32) for _ in range(n_heads))
    lvec, acc = lax.fori_loop(0, n_pages, value_page, (zeros, zeros), unroll=unroll)
    lvec, acc = value_page(n_pages, (lvec, acc))
    for h in range(n_heads):
        l = jnp.sum(lvec[h], axis=1, keepdims=True)
        o_ref[0, :, h * HEAD_DIM:(h + 1) * HEAD_DIM] = acc[h] / l


def _sample_attn(page_table, q8, knew, vnew, lnew, ck, cv, cl, *, n_heads, n_new):
    n_batch, n_pages = page_table.shape
    page = cl.shape[-1]
    assert page == HEAD_DIM == LANES
    a = q8.shape[-1]
    rows = (n_pages + 1) * page * n_heads
    any_spec = pl.BlockSpec(memory_space=pl.ANY)
    q_blk = pl.BlockSpec((1, 8, a), lambda b, pt: (b, 0, 0))
    grid_spec = pltpu.PrefetchScalarGridSpec(
        num_scalar_prefetch=1, grid=(n_batch,),
        in_specs=[q_blk, any_spec, any_spec, any_spec, any_spec, any_spec, any_spec],
        out_specs=q_blk,
        scratch_shapes=[
            pltpu.VMEM((2, rows, HEAD_DIM), f32),
            pltpu.VMEM((2, rows, HEAD_DIM), f32),
            pltpu.VMEM((2, n_pages + 1, n_heads, page), f32),
            pltpu.VMEM((n_pages + 1, n_heads, page), f32),
            pltpu.VMEM((n_pages + 1, n_heads, 8, page), f32),
            pltpu.SemaphoreType.DMA((2, 3)),
        ])
    return pl.pallas_call(
        functools.partial(_sample_attn_kernel, n_pages=n_pages, n_batch=n_batch,
                          n_heads=n_heads, n_new=n_new, page=page),
        grid_spec=grid_spec, out_shape=jax.ShapeDtypeStruct(q8.shape, f32),
        compiler_params=_params(), name="sample_attn")(
            page_table.reshape(-1), q8, knew, vnew, lnew, ck, cv, cl)


def _layer_norm(y, g, b):
    mu = jnp.mean(y, axis=-1, keepdims=True)
    yc = y - mu
    var = jnp.mean(yc * yc, axis=-1, keepdims=True)
    return yc * lax.rsqrt(var + LN_EPS) * g + b


def _finish1_kernel(ap_ref, cp_ref, xp_ref, as_ref, cs_ref, xs_ref,
                    woa_ref, woc_ref, g_ref, b_ref, wr2_ref, wrh_ref, br_ref,
                    x1_ref, x1g_ref, meta_ref, cnt_ref, gw_ref, cnt_scr, *, alpha, nb_prompt, tm):
    is_prompt = pl.program_id(0) < nb_prompt
    attn = jnp.where(is_prompt, ap_ref[...], as_ref[...])
    conv = jnp.where(is_prompt, cp_ref[...], cs_ref[...])
    x = jnp.where(is_prompt, xp_ref[...], xs_ref[...])
    o = _dot(attn, woa_ref[...]) + _dot(conv, woc_ref[...])
    x1 = _layer_norm(alpha * x + o, g_ref[...], b_ref[...])
    x1_ref[...] = x1
    n_chunks = x1.shape[1] // LANES
    for c in range(n_chunks):
        x1g_ref[pl.ds(c, tm, stride=n_chunks), :] = x1[:, c * LANES:(c + 1) * LANES]

    x1b = x1.astype(bf16)
    x1l = (x1 - x1b.astype(f32)).astype(bf16)
    r2 = _dot(x1b, wr2_ref[...])
    lr = r2[:, :LANES] + r2[:, LANES:] + _dot(x1l, wrh_ref[...]) + br_ref[...]
    lane = lax.broadcasted_iota(jnp.int32, lr.shape, 1)

    def first_max(mask):
        vals = jnp.where(mask, lr, -jnp.inf)
        mx = jnp.max(vals, axis=-1, keepdims=True)
        idx = jnp.min(jnp.where(mask & (vals == mx), lane, LANES), axis=-1, keepdims=True)
        return mx, idx

    is_group = lane < N_GROUPS
    gmax, gidx = first_max(is_group)
    p_group = 1.0 / jnp.sum(jnp.where(is_group, jnp.exp(lr - gmax), 0.0), axis=-1, keepdims=True)
    lo = N_GROUPS + gidx * EXPERTS_PER_GROUP
    in_group = (lane >= lo) & (lane < lo + EXPERTS_PER_GROUP)
    e1, i1 = first_max(in_group)
    e2, i2 = first_max(in_group & (lane != i1))
    r = jnp.exp(e2 - e1)
    w1 = p_group / (1.0 + r)
    w2 = p_group * r / (1.0 + r)
    gw_ref[...] = jnp.where(lane == 0, w1, jnp.where(lane == 1, w2, 0.0))

    @pl.when(pl.program_id(0) == 0)
    def _():
        cnt_scr[...] = jnp.zeros(cnt_scr.shape, f32)

    id1, id2 = i1 - N_GROUPS, i2 - N_GROUPS
    sel1 = jnp.where(lane == id1, 1.0, 0.0)
    sel2 = jnp.where(lane == id2, 1.0, 0.0)
    row = lax.broadcasted_iota(jnp.int32, (tm, tm), 0)
    col = lax.broadcasted_iota(jnp.int32, (tm, tm), 1)
    before = jnp.where(col < row, 1.0, 0.0).astype(bf16)
    seen = cnt_scr[0:1, :]
    tot1 = jnp.sum(sel1, axis=0, keepdims=True)
    rank1 = jnp.sum(sel1 * (seen + _dot(before, sel1.astype(bf16))), axis=-1, keepdims=True)
    rank2 = jnp.sum(sel2 * ((seen + tot1) + _dot(before, sel2.astype(bf16))), axis=-1, keepdims=True)
    seen = (seen + tot1) + jnp.sum(sel2, axis=0, keepdims=True)
    cnt_scr[...] = jnp.broadcast_to(seen, cnt_scr.shape)
    cnt_ref[...] = jnp.broadcast_to(seen, cnt_ref.shape)
    meta = jnp.where(lane == 0, id1.astype(f32), jnp.where(lane == 1, id2.astype(f32),
                     jnp.where(lane == 2, rank1, jnp.where(lane == 3, rank2, 0.0))))
    meta_ref[...] = meta.T[0:8, :]


def _finish1(attn_p, conv_p, x_p, attn_s, conv_s, x_s, wo, g, b, wr2, wrh, br, *, alpha):
    (t_p, d), t_s = x_p.shape, x_s.shape[0]
    a = attn_p.shape[1]
    assert wo.shape[0] == 2 * a
    tm = min(TM_PROJ, t_s)
    nb_p, nb_s = t_p // tm, t_s // tm
    t_all = t_p + t_s
    p_blk = lambda w: pl.BlockSpec((tm, w), lambda i: (jnp.minimum(i, nb_p - 1), 0))
    s_blk = lambda w: pl.BlockSpec((tm, w), lambda i: (jnp.maximum(i - nb_p, 0), 0))
    out_blk = lambda w: pl.BlockSpec((tm, w), lambda i: (i, 0))
    in_specs = [p_blk(a), p_blk(a), p_blk(d), s_blk(a), s_blk(a), s_blk(d),
                pl.BlockSpec((a, d), lambda i: (0, 0), pipeline_mode=pl.Buffered(1)),
                pl.BlockSpec((a, d), lambda i: (1, 0), pipeline_mode=pl.Buffered(1)),
                _resident(g.shape), _resident(b.shape), _resident(wr2.shape),
                _resident(wrh.shape), _resident(br.shape)]
    n_chunks = d // LANES
    return pl.pallas_call(
        functools.partial(_finish1_kernel, alpha=alpha, nb_prompt=nb_p, tm=tm),
        grid=(nb_p + nb_s,), in_specs=in_specs,
        out_specs=[out_blk(d), pl.BlockSpec((tm * n_chunks, LANES), lambda i: (i, 0)),
                   pl.BlockSpec((8, tm), lambda i: (0, i)),
                   pl.BlockSpec((8, LANES), lambda i: (0, 0)), out_blk(LANES)],
        out_shape=[jax.ShapeDtypeStruct((t_all, d), f32),
                   jax.ShapeDtypeStruct((t_all * n_chunks, LANES), f32),
                   jax.ShapeDtypeStruct((8, t_all), f32),
                   jax.ShapeDtypeStruct((8, LANES), f32),
                   jax.ShapeDtypeStruct((t_all, LANES), f32)],
        scratch_shapes=[pltpu.VMEM((8, LANES), f32)],
        compiler_params=_params(), name="finish1")(
            attn_p, conv_p, x_p, attn_s, conv_s, x_s, wo, wo, g, b, wr2, wrh, br)


def _moe_kernel(te_ref, ord_ref, nxt_ref, nu_ref, idx_ref, x_hbm, wg_hbm, wu_hbm, wd_hbm, y_ref,
                xbuf, wgf, wuf, wdf, wgb, wub, wdb, sem, wsem, *, tm):
    s = pl.program_id(0)
    n_used = nu_ref[0]

    def weight_copies(e, wslot):
        return (pltpu.make_async_copy(wg_hbm.at[e], wgf.at[wslot], wsem.at[wslot, 0]),
                pltpu.make_async_copy(wu_hbm.at[e], wuf.at[wslot], wsem.at[wslot, 1]),
                pltpu.make_async_copy(wd_hbm.at[e], wdf.at[wslot], wsem.at[wslot, 2]))

    @pl.when(s == 0)
    def _():
        for cp in weight_copies(te_ref[0], 0):
            cp.start(priority=1)

    @pl.when(s < n_used)
    def _():
        slot = s % 2

        def issue(r, c):
            tok = idx_ref[0, 0, r]
            pltpu.make_async_copy(x_hbm.at[tok], xbuf.at[slot, :, r, :], sem.at[slot]).start()
            return c

        lax.fori_loop(0, tm, issue, 0, unroll=8)

    t = s - 1

    @pl.when((t >= 0) & (t < n_used))
    def _():
        slot = t % 2
        pltpu.make_async_copy(xbuf.at[slot], xbuf.at[slot], sem.at[slot]).wait()
        e = te_ref[t]
        new_expert = (t == 0) | (e != te_ref[jnp.maximum(t - 1, 0)])

        @pl.when(new_expert)
        def _():
            wslot = ord_ref[t] % 2
            for cp in weight_copies(e, wslot):
                cp.wait()
            nxt = nxt_ref[e]

            @pl.when(nxt >= 0)
            def _():
                for cp in weight_copies(nxt, 1 - wslot):
                    cp.start(priority=1)

            wgb[...] = wgf[wslot].astype(bf16)
            wub[...] = wuf[wslot].astype(bf16)
            wdb[...] = wdf[wslot].astype(bf16)

        xb = jnp.concatenate([xbuf[slot, c] for c in range(xbuf.shape[1])], axis=-1).astype(bf16)
        hg = _dot(xb, wgb[...])
        hu = _dot(xb, wub[...])
        act = hg * (1.0 / (1.0 + jnp.exp(-hg))) * hu
        y_ref[...] = _dot(act.astype(bf16), wdb[...])

    @pl.when(t >= n_used)
    def _():
        y_ref[...] = jnp.zeros(y_ref.shape, f32)


def _moe(x1g, tok_of, tile_expert, tile_ord, next_expert, n_used, wg, wu, wd):
    n_tiles, _, tm = tok_of.shape
    _, d, f = wg.shape
    last = n_tiles - 1
    any_spec = pl.BlockSpec(memory_space=pl.ANY)
    grid_spec = pltpu.PrefetchScalarGridSpec(
        num_scalar_prefetch=4, grid=(n_tiles + 1,),
        in_specs=[pl.BlockSpec((1, 1, tm), lambda s, *_: (jnp.minimum(s, last), 0, 0),
                               memory_space=pltpu.SMEM),
                  any_spec, any_spec, any_spec, any_spec],
        out_specs=pl.BlockSpec((tm, d), lambda s, *_: (jnp.maximum(s - 1, 0), 0)),
        scratch_shapes=[pltpu.VMEM((2, d // LANES, tm, LANES), f32),
                        pltpu.VMEM((2, d, f), f32), pltpu.VMEM((2, d, f), f32),
                        pltpu.VMEM((2, f, d), f32),
                        pltpu.VMEM((d, f), bf16), pltpu.VMEM((d, f), bf16), pltpu.VMEM((f, d), bf16),
                        pltpu.SemaphoreType.DMA((2,)), pltpu.SemaphoreType.DMA((2, 3))])
    return pl.pallas_call(
        functools.partial(_moe_kernel, tm=tm),
        grid_spec=grid_spec, out_shape=jax.ShapeDtypeStruct((n_tiles * tm, d), f32),
        compiler_params=_params(), name="moe")(
            tile_expert, tile_ord, next_expert, n_used, tok_of, x1g, wg, wu, wd)


def _finish2_kernel(idx_ref, ys_hbm, x1_ref, gw_ref, g_ref, b_ref, o_ref, gbuf, sem,
                    *, tm, n_steps, alpha):
    s = pl.program_id(0)

    @pl.when(s < n_steps)
    def _():
        slot = s % 2

        def issue(r, c):
            row = idx_ref[0, 0, r]
            pltpu.make_async_copy(ys_hbm.at[pl.ds(row, 1)], gbuf.at[slot, pl.ds(r, 1)],
                                  sem.at[slot]).start()
            return c

        lax.fori_loop(0, 2 * tm, issue, 0, unroll=8)

    @pl.when(s >= 1)
    def _():
        slot = (s - 1) % 2
        pltpu.make_async_copy(ys_hbm.at[pl.ds(0, 2 * tm)], gbuf.at[slot], sem.at[slot]).wait()
        gw = gw_ref[...]
        f = gw[:, 0:1] * gbuf[slot, 0:tm, :] + gw[:, 1:2] * gbuf[slot, tm:2 * tm, :]
        o_ref[...] = _layer_norm(alpha * x1_ref[...] + f, g_ref[...], b_ref[...])


def _finish2(ys, pos, x1, gw, g, b, *, alpha, row0, t):
    n_steps, _, two_tm = pos.shape
    tm = two_tm // 2
    d = x1.shape[1]
    blk0 = row0 // tm
    last = n_steps - 1
    in_blk = lambda w: pl.BlockSpec((tm, w), lambda s: (jnp.maximum(s - 1, 0) + blk0, 0))
    return pl.pallas_call(
        functools.partial(_finish2_kernel, tm=tm, n_steps=n_steps, alpha=alpha),
        grid=(n_steps + 1,),
        in_specs=[pl.BlockSpec((1, 1, two_tm), lambda s: (jnp.minimum(s, last), 0, 0),
                               memory_space=pltpu.SMEM),
                  pl.BlockSpec(memory_space=pl.ANY),
                  in_blk(d), in_blk(LANES), _resident(g.shape), _resident(b.shape)],
        out_specs=pl.BlockSpec((tm, d), lambda s: (jnp.maximum(s - 1, 0), 0)),
        out_shape=jax.ShapeDtypeStruct((t, d), f32),
        scratch_shapes=[pltpu.VMEM((2, two_tm, d), f32), pltpu.SemaphoreType.DMA((2,))],
        compiler_params=_params(), name="finish2")(pos, ys, x1, gw, g, b)


def _bias_selectors(n_heads, sign):
    sel = np.zeros((3, n_heads, n_heads * HEAD_DIM), np.float32)
    for p in range(3):
        for h in range(n_heads):
            sel[p, h, h * HEAD_DIM + p] = sign
    return sel


def _route_tables(meta, counts, tm):
    n_tok = meta.shape[1]
    e = meta[0:2].astype(jnp.int32)
    rank = meta[2:4].astype(jnp.int32)
    counts = counts[0, :N_EXPERTS].astype(jnp.int32)
    n_pairs = 2 * n_tok
    padded = ((counts + tm - 1) // tm) * tm
    ends = jnp.cumsum(padded)
    pos = jnp.take(ends - padded, e) + rank
    n_tiles = (n_pairs + N_EXPERTS * (tm - 1)) // tm
    rows = n_tiles * tm
    tok = jnp.broadcast_to(jnp.arange(n_tok, dtype=jnp.int32)[None, :], (2, n_tok))
    tok_of = jnp.zeros((rows,), jnp.int32).at[pos.reshape(-1)].set(
        tok.reshape(-1), unique_indices=True, mode="promise_in_bounds")
    tile_start = jnp.arange(n_tiles, dtype=jnp.int32) * tm
    tile_expert = jnp.minimum(
        jnp.sum((ends[None, :] <= tile_start[:, None]).astype(jnp.int32), axis=1), N_EXPERTS - 1)
    n_used = (ends[-1] // tm).astype(jnp.int32).reshape(1)
    experts = jnp.arange(N_EXPERTS, dtype=jnp.int32)
    nonempty = counts > 0
    expert_ord = jnp.cumsum(nonempty.astype(jnp.int32)) - 1
    later = nonempty[None, :] & (experts[None, :] > experts[:, None])
    next_expert = jnp.min(jnp.where(later, experts[None, :], N_EXPERTS), axis=1)
    next_expert = jnp.where(next_expert < N_EXPERTS, next_expert, -1).astype(jnp.int32)
    tile_ord = expert_ord[tile_expert].astype(jnp.int32)
    return pos, tok_of.reshape(n_tiles, 1, tm), tile_expert, tile_ord, next_expert, n_used


def _tile_pos(pos, tm):
    t = pos.shape[1]
    return jnp.concatenate([pos[0].reshape(t // tm, 1, tm), pos[1].reshape(t // tm, 1, tm)], axis=2)


def kernel(x_prompt, x_sample, cache_k, cache_v, cache_logf, state_conv, page_table, w_in, b_f,
           w_conv, w_out, ln1_g, ln1_b, w_rg, b_rg, w_re, b_re, w_gate, w_up, w_down, ln2_g, ln2_b):
    depth, d_model, _ = w_in.shape
    n_batch_p, seq, _ = x_prompt.shape
    db, dseq, _ = x_sample.shape
    n_heads = cache_k.shape[3]
    a = n_heads * HEAD_DIM
    c = w_conv.shape[-1]
    assert n_batch_p == 1 and dseq <= 8 and w_conv.shape[1] == CONV_K
    alpha = (2.0 * depth) ** 0.25
    qscale = HEAD_DIM ** -0.5 * LOG2E
    t_p, t_s = seq, db * dseq
    t_all = t_p + t_s

    sel_neg = _bias_selectors(n_heads, -1.0)
    psel_k = jnp.asarray(np.concatenate(
        [np.pad(sel_neg[p], ((0, LANES - n_heads), (0, 0))) for p in range(3)], axis=0), bf16)

    xp = x_prompt.reshape(t_p, d_model)
    xs = x_sample.reshape(t_s, d_model)
    outs = {k: [] for k in ("kp", "vp", "fp", "cp", "ks", "vs", "fs", "cs")}
    for l in range(depth):
        wl = jnp.swapaxes(w_in[l], 0, 1)
        wq, wk, wv = (wl[j * a:(j + 1) * a].astype(bf16) for j in range(3))
        o = 3 * a
        wf = jnp.pad(wl[o:o + n_heads], ((0, LANES - n_heads), (0, 0)))
        wfh = wf.astype(bf16)
        wf2 = jnp.concatenate([wfh, (wf - wfh.astype(f32)).astype(bf16)], axis=0)
        bfp = jnp.pad(b_f[l], (0, LANES - n_heads)).reshape(1, LANES)
        o += n_heads
        wb, wc, wh = (wl[o + j * c:o + (j + 1) * c].astype(bf16) for j in range(3))
        wo = w_out[l].astype(bf16)
        wr = jnp.pad(jnp.concatenate([w_rg[l], w_re[l]], axis=1),
                     ((0, 0), (0, LANES - N_GROUPS - N_EXPERTS)))
        wrh = wr.astype(bf16)
        wr2 = jnp.concatenate([wrh, (wr - wrh.astype(f32)).astype(bf16)], axis=1)
        br = jnp.pad(jnp.concatenate([b_rg[l], b_re[l]]), (0, LANES - N_GROUPS - N_EXPERTS)).reshape(1, LANES)
        g1, b1 = ln1_g[l].reshape(1, -1), ln1_b[l].reshape(1, -1)
        g2, b2 = ln2_g[l].reshape(1, -1), ln2_b[l].reshape(1, -1)

        q, ko, vo, kb, vb, lf, kx = _qkv_proj(xp, wq, wk, wv, wf2, wfh, bfp, psel_k,
                                              n_heads=n_heads, qscale=qscale, with_cum=True)
        conv, utail = _conv_proj(xp, wb, wc, wh, w_conv[l])
        attn = _prompt_attn(q, kb, kx, vb, n_heads=n_heads)
        outs["kp"].append(ko.reshape(1, t_p, n_heads, HEAD_DIM))
        outs["vp"].append(vo.reshape(1, t_p, n_heads, HEAD_DIM))
        outs["fp"].append(lf[:, :n_heads].reshape(1, t_p, n_heads))
        outs["cp"].append(utail[8 - (CONV_K - 1):].reshape(1, CONV_K - 1, c))

        qs, kos, vos, _, _, lfs = _qkv_proj(xs, wq, wk, wv, wf2, wfh, bfp, None,
                                            n_heads=n_heads, qscale=qscale, with_cum=False)
        st = state_conv[l]
        zero = jnp.zeros((db, dseq, c), f32)
        p1 = zero.at[:, 0].set(st[:, 1]).reshape(t_s, c)
        p2 = zero.at[:, 0].set(st[:, 0]).at[:, 1].set(st[:, 1]).reshape(t_s, c)
        conv_s, u_s = _conv_proj(xs, wb, wc, wh, w_conv[l], p1, p2, group=dseq)
        q8 = jnp.pad(qs.reshape(db, dseq, a), ((0, 0), (0, 8 - dseq), (0, 0))).astype(f32)
        page = cache_k.shape[2]
        lnew = jnp.pad(jnp.swapaxes(lfs[:, :n_heads].reshape(db, dseq, n_heads), 1, 2),
                       ((0, 0), (0, 0), (0, page - dseq)))
        attn_s = _sample_attn(page_table, q8, kos, vos, lnew,
                              cache_k[l].reshape(-1, HEAD_DIM), cache_v[l].reshape(-1, HEAD_DIM),
                              jnp.swapaxes(cache_logf[l], 1, 2), n_heads=n_heads, n_new=dseq)
        attn_s = attn_s[:, :dseq].reshape(t_s, a).astype(bf16)
        outs["ks"].append(kos.reshape(db, dseq, n_heads, HEAD_DIM))
        outs["vs"].append(vos.reshape(db, dseq, n_heads, HEAD_DIM))
        outs["fs"].append(lfs[:, :n_heads].reshape(db, dseq, n_heads))
        outs["cs"].append(u_s.reshape(db, dseq, c)[:, dseq - (CONV_K - 1):])
        x1, x1g, meta, counts, gw = _finish1(attn, conv, xp, attn_s, conv_s, xs, wo, g1, b1,
                                             wr2, wrh, br, alpha=alpha)

        pos, tok_of, tile_expert, tile_ord, next_expert, n_used = _route_tables(
            meta, counts, TM_MOE)
        ys = _moe(x1g.reshape(t_all, d_model // LANES, LANES), tok_of, tile_expert, tile_ord,
                  next_expert, n_used, w_gate[l], w_up[l], w_down[l])
        tm2 = min(TM_PROJ, t_s)
        xp = _finish2(ys, _tile_pos(pos[:, :t_p], TM_PROJ), x1, gw, g2, b2, alpha=alpha, row0=0, t=t_p)
        xs = _finish2(ys, _tile_pos(pos[:, t_p:], tm2), x1, gw, g2, b2, alpha=alpha, row0=t_p, t=t_s)

    stack = lambda k: jnp.stack(outs[k])
    return (xp.reshape(x_prompt.shape), xs.reshape(x_sample.shape),
            stack("kp"), stack("vp"), stack("fp"), stack("cp"),
            stack("ks"), stack("vs"), stack("fs"), stack("cs"))
```

```python
import functools
import math

import numpy as np
import jax
import jax.numpy as jnp
from jax import lax
from jax.experimental import pallas as pl
from jax.experimental.pallas import tpu as pltpu

f32 = jnp.float32
bf16 = jnp.bfloat16

LANES = 128
HEAD_DIM = 128
LOG2E = 1.4426950408889634
LN_EPS = 1e-5
NEG = -1e30
CONV_K = 3
N_GROUPS = 4
EXPERTS_PER_GROUP = 8
N_EXPERTS = N_GROUPS * EXPERTS_PER_GROUP
VMEM_LIMIT = 56 * 1024 * 1024

TM_PROJ = 256
TQ = 512
N_SUB = 2
PAGE_UNROLL = 4
TM_MOE = 512


def _dot(a, b):
    return jnp.dot(a, b, preferred_element_type=f32)


def _dot_nt(a, b):
    return lax.dot_general(a, b, (((1,), (1,)), ((), ())), preferred_element_type=f32)


def _lane_tile(x, n):
    return jnp.concatenate([x] * n, axis=-1)


def _split3(x):
    hi = x.astype(bf16)
    r1 = x - hi.astype(f32)
    mid = r1.astype(bf16)
    lo = (r1 - mid.astype(f32)).astype(bf16)
    return hi, mid, lo


def _params(sem=("arbitrary",)):
    return pltpu.CompilerParams(dimension_semantics=sem, vmem_limit_bytes=VMEM_LIMIT)


def _resident(shape):
    nd = len(shape)
    return pl.BlockSpec(shape, lambda *_: (0,) * nd, pipeline_mode=pl.Buffered(1))


def _qkv_kernel(x_ref, wq_ref, wk_ref, wv_ref, wf2_ref, wfh_ref, bf_ref, psel_ref,
                q_ref, ko_ref, vo_ref, kb_ref, vb_ref, lf_ref, kx_ref, carry_ref,
                *, tm, n_heads, qscale):
    i = pl.program_id(0)
    x = x_ref[...]
    xb = x.astype(bf16)
    q_ref[...] = (_dot_nt(xb, wq_ref[...]) * qscale).astype(bf16)
    k = _dot_nt(xb, wk_ref[...])
    kb_ref[...] = k.astype(bf16)
    v = _dot_nt(xb, wv_ref[...])
    vb_ref[...] = v.astype(bf16)
    for h in range(n_heads):
        ko_ref[pl.ds(h, tm, stride=n_heads), :] = k[:, h * HEAD_DIM:(h + 1) * HEAD_DIM]
        vo_ref[pl.ds(h, tm, stride=n_heads), :] = v[:, h * HEAD_DIM:(h + 1) * HEAD_DIM]

    xl = (x - xb.astype(f32)).astype(bf16)
    zf2 = _dot_nt(xb, wf2_ref[...])
    zf = zf2[:, :LANES] + zf2[:, LANES:] + _dot_nt(xl, wfh_ref[...]) + bf_ref[...]
    lf = jnp.minimum(zf, 0.0) - jnp.log1p(jnp.exp(-jnp.abs(zf)))
    lf_ref[...] = lf

    if kx_ref is not None:
        @pl.when(i == 0)
        def _():
            carry_ref[...] = jnp.zeros(carry_ref.shape, f32)

        row = lax.broadcasted_iota(jnp.int32, (tm, tm), 0)
        col = lax.broadcasted_iota(jnp.int32, (tm, tm), 1)
        tri = jnp.where(row >= col, 1.0, 0.0).astype(bf16)
        c3 = _dot(tri, jnp.concatenate(_split3(lf), axis=-1))
        cum = (c3[:, :LANES] + c3[:, LANES:2 * LANES] + c3[:, 2 * LANES:]) + carry_ref[0:1, :]
        carry_ref[...] = jnp.broadcast_to(cum[tm - 1:tm, :], carry_ref.shape)
        parts = jnp.concatenate(_split3(cum * LOG2E), axis=-1)
        kx_ref[...] = _dot(parts, psel_ref[...]).astype(bf16)


def _qkv_proj(x, wq, wk, wv, wf2, wfh, bfp, psel, *, n_heads, qscale, with_cum):
    t, d = x.shape
    a = wq.shape[0]
    tm = min(TM_PROJ, t)
    grid = (t // tm,)
    row_blk = lambda w: pl.BlockSpec((tm, w), lambda i: (i, 0))
    in_specs = [row_blk(d), _resident(wq.shape), _resident(wk.shape), _resident(wv.shape),
                _resident(wf2.shape), _resident(wfh.shape), _resident(bfp.shape)]
    args = [x, wq, wk, wv, wf2, wfh, bfp]
    out_shape = [jax.ShapeDtypeStruct((t, a), bf16),
                 jax.ShapeDtypeStruct((t * n_heads, HEAD_DIM), f32),
                 jax.ShapeDtypeStruct((t * n_heads, HEAD_DIM), f32),
                 jax.ShapeDtypeStruct((t, a), bf16),
                 jax.ShapeDtypeStruct((t, a), bf16),
                 jax.ShapeDtypeStruct((t, LANES), f32)]
    hd_blk = pl.BlockSpec((tm * n_heads, HEAD_DIM), lambda i: (i, 0))
    out_specs = [row_blk(a), hd_blk, hd_blk, row_blk(a), row_blk(a), row_blk(LANES)]
    scratch = []
    if with_cum:
        in_specs.append(_resident(psel.shape))
        args.append(psel)
        out_shape.append(jax.ShapeDtypeStruct((t, a), bf16))
        out_specs.append(row_blk(a))
        scratch.append(pltpu.VMEM((8, LANES), f32))

        def body(*refs):
            _qkv_kernel(*refs, tm=tm, n_heads=n_heads, qscale=qscale)
    else:
        def body(x_ref, wq_ref, wk_ref, wv_ref, wf2_ref, wfh_ref, bf_ref, *outs):
            _qkv_kernel(x_ref, wq_ref, wk_ref, wv_ref, wf2_ref, wfh_ref, bf_ref, None,
                        *outs, None, None, tm=tm, n_heads=n_heads, qscale=qscale)

    return pl.pallas_call(
        body, grid=grid, in_specs=in_specs, out_specs=out_specs, out_shape=out_shape,
        scratch_shapes=scratch, compiler_params=_params(), name="qkv_proj")(*args)


def _conv_kernel(x_ref, wb_ref, wc_ref, wh_ref, wconv_ref, p1_ref, p2_ref,
                 conv_ref, u_ref, ubuf, *, tm, group):
    i = pl.program_id(0)
    xb = x_ref[...].astype(bf16)
    gate_b = _dot_nt(xb, wb_ref[...])
    u = _dot_nt(xb, wc_ref[...]) * _dot_nt(xb, wh_ref[...])
    w0 = wconv_ref[0:1, :]
    w1 = wconv_ref[1:2, :]
    w2 = wconv_ref[2:3, :]
    if group is None:
        @pl.when(i == 0)
        def _():
            ubuf[0:8, :] = jnp.zeros((8, ubuf.shape[1]), f32)

        ubuf[8:8 + tm, :] = u
        u1 = ubuf[pl.ds(7, tm), :]
        u2 = ubuf[pl.ds(6, tm), :]
        ubuf[0:8, :] = u[tm - 8:tm, :]
        u_ref[...] = u[tm - 8:tm, :]
    else:
        ubuf[0:8, :] = jnp.zeros((8, ubuf.shape[1]), f32)
        ubuf[8:8 + tm, :] = u
        t_in = lax.broadcasted_iota(jnp.int32, (tm, 1), 0) % group
        u1 = jnp.where(t_in >= 1, ubuf[pl.ds(7, tm), :], p1_ref[...])
        u2 = jnp.where(t_in >= 2, ubuf[pl.ds(6, tm), :], p2_ref[...])
        u_ref[...] = u
    y = (u2 * w0 + u1 * w1) + u * w2
    conv_ref[...] = (gate_b * y).astype(bf16)


def _conv_proj(x, wb, wc, wh, wconv, p1=None, p2=None, *, group=None):
    t, d = x.shape
    c = wb.shape[0]
    tm = min(TM_PROJ, t)
    grid = (t // tm,)
    row_blk = lambda w: pl.BlockSpec((tm, w), lambda i: (i, 0))
    in_specs = [row_blk(d), _resident(wb.shape), _resident(wc.shape), _resident(wh.shape),
                _resident(wconv.shape)]
    args = [x, wb, wc, wh, wconv]
    if group is None:
        u_shape = jax.ShapeDtypeStruct((8, c), f32)
        u_spec = pl.BlockSpec((8, c), lambda i: (0, 0))

        def body(x_ref, wb_ref, wc_ref, wh_ref, wconv_ref, conv_ref, u_ref, ubuf):
            _conv_kernel(x_ref, wb_ref, wc_ref, wh_ref, wconv_ref, None, None,
                         conv_ref, u_ref, ubuf, tm=tm, group=None)
    else:
        in_specs += [row_blk(c), row_blk(c)]
        args += [p1, p2]
        u_shape = jax.ShapeDtypeStruct((t, c), f32)
        u_spec = row_blk(c)

        def body(*refs):
            _conv_kernel(*refs, tm=tm, group=group)

    return pl.pallas_call(
        body, grid=grid, in_specs=in_specs,
        out_specs=[row_blk(c), u_spec],
        out_shape=[jax.ShapeDtypeStruct((t, c), bf16), u_shape],
        scratch_shapes=[pltpu.VMEM((tm + 8, c), f32)],
        compiler_params=_params(), name="conv_proj")(*args)


def _prompt_attn_kernel(q_ref, k_ref, kx_ref, v_ref, o_ref, *scratch, tq, n_sub):
    m_scr, acc_scr = scratch[:n_sub], scratch[n_sub:2 * n_sub]
    s_a, s_b = scratch[2 * n_sub:3 * n_sub], scratch[3 * n_sub:]
    i = pl.program_id(1)
    ts = tq // n_sub
    ones_q = jnp.ones((ts, LANES), bf16)
    qa = [jnp.concatenate([q_ref[u * ts:(u + 1) * ts, :], ones_q], axis=-1) for u in range(n_sub)]
    ones_v = jnp.ones((tq, LANES), bf16)
    for u in range(n_sub):
        m_scr[u][...] = jnp.full(m_scr[u].shape, NEG, f32)
        acc_scr[u][...] = jnp.zeros(acc_scr[u].shape, f32)

    def scores(j, s_buf):
        ks = pl.multiple_of(j * tq, tq)
        ka = jnp.concatenate([k_ref[pl.ds(ks, tq), :], kx_ref[pl.ds(ks, tq), :]], axis=-1)
        for u in range(n_sub):
            s_buf[u][...] = _dot_nt(qa[u], ka)

    def accumulate(j, s_buf, masked):
        ks = pl.multiple_of(j * tq, tq)
        va = jnp.concatenate([v_ref[pl.ds(ks, tq), :], ones_v], axis=-1)
        for u in range(n_sub):
            s = s_buf[u][...]
            if masked:
                row = lax.broadcasted_iota(jnp.int32, (ts, tq), 0) + u * ts
                col = lax.broadcasted_iota(jnp.int32, (ts, tq), 1)
                s = jnp.where(col <= row, s, NEG)
            m_prev = m_scr[u][...]
            m_new = jnp.maximum(m_prev, jnp.max(s, axis=1, keepdims=True))
            p = jnp.exp2(s - _lane_tile(m_new, tq // LANES))
            alpha = jnp.exp2(m_prev - m_new)
            acc_scr[u][...] = acc_scr[u][...] * _lane_tile(alpha, 2) + _dot(p.astype(bf16), va)
            m_scr[u][...] = m_new

    scores(0, s_a)

    def body(jj, c):
        j = 2 * jj
        scores(j + 1, s_b)
        accumulate(j, s_a, False)
        scores(j + 2, s_a)
        accumulate(j + 1, s_b, False)
        return c

    lax.fori_loop(0, i // 2, body, 0)

    @pl.when(i % 2 == 0)
    def _():
        accumulate(i, s_a, True)

    @pl.when(i % 2 == 1)
    def _():
        scores(i, s_b)
        accumulate(i - 1, s_a, False)
        accumulate(i, s_b, True)

    for u in range(n_sub):
        acc = acc_scr[u][...]
        o_ref[u * ts:(u + 1) * ts, :] = (acc[:, :LANES] / acc[:, LANES:]).astype(o_ref.dtype)


def _prompt_attn(q, k, kx, v, *, n_heads):
    s = q.shape[0]
    tq = min(TQ, s)
    grid = (n_heads, s // tq)
    q_blk = pl.BlockSpec((tq, HEAD_DIM), lambda h, i: (i, h))
    kv_blk = pl.BlockSpec((s, HEAD_DIM), lambda h, i: (0, h))
    return pl.pallas_call(
        functools.partial(_prompt_attn_kernel, tq=tq, n_sub=N_SUB),
        grid=grid, in_specs=[q_blk, kv_blk, kv_blk, kv_blk], out_specs=q_blk,
        out_shape=jax.ShapeDtypeStruct(q.shape, bf16),
        scratch_shapes=([pltpu.VMEM((tq // N_SUB, LANES), f32)] * N_SUB
                        + [pltpu.VMEM((tq // N_SUB, 2 * LANES), f32)] * N_SUB
                        + [pltpu.VMEM((tq // N_SUB, tq), f32)] * (2 * N_SUB)),
        compiler_params=_params(("arbitrary", "arbitrary")), name="prompt_attn")(q, k, kx, v)


def _sample_attn_kernel(pt_ref, q_ref, knew_hbm, vnew_hbm, lnew_hbm, ck_hbm, cv_hbm, cl_hbm,
                        o_ref, kbuf, vbuf, lbuf, bias_scr, sbuf, sem,
                        *, n_pages, n_batch, n_heads, n_new, page):
    b = pl.program_id(0)
    slot = b % 2
    npg = n_pages + 1
    rows = page * n_heads
    new_rows = n_new * n_heads

    def copies(bb, sl):
        cps = []
        for p in range(n_pages):
            pg = pt_ref[bb * n_pages + p]
            src = pl.ds(pl.multiple_of(pg * rows, rows), rows)
            dst = pl.ds(p * rows, rows)
            cps.append(pltpu.make_async_copy(ck_hbm.at[src], kbuf.at[sl, dst], sem.at[sl, 0]))
            cps.append(pltpu.make_async_copy(cv_hbm.at[src], vbuf.at[sl, dst], sem.at[sl, 1]))
            cps.append(pltpu.make_async_copy(cl_hbm.at[pg], lbuf.at[sl, p], sem.at[sl, 2]))
        src = pl.ds(pl.multiple_of(bb * new_rows, new_rows), new_rows)
        dst = pl.ds(n_pages * rows, new_rows)
        cps.append(pltpu.make_async_copy(knew_hbm.at[src], kbuf.at[sl, dst], sem.at[sl, 0]))
        cps.append(pltpu.make_async_copy(vnew_hbm.at[src], vbuf.at[sl, dst], sem.at[sl, 1]))
        cps.append(pltpu.make_async_copy(lnew_hbm.at[bb], lbuf.at[sl, n_pages], sem.at[sl, 2]))
        return cps

    @pl.when(b == 0)
    def _():
        for sl in range(2):
            kbuf[sl, pl.ds(n_pages * rows, rows), :] = jnp.zeros((rows, HEAD_DIM), f32)
            vbuf[sl, pl.ds(n_pages * rows, rows), :] = jnp.zeros((rows, HEAD_DIM), f32)
        for cp in copies(0, 0):
            cp.start()

    @pl.when(b + 1 < n_batch)
    def _():
        for cp in copies(b + 1, 1 - slot):
            cp.start()

    for cp in copies(b, slot):
        cp.wait()

    row = lax.broadcasted_iota(jnp.int32, (page, page), 0)
    col = lax.broadcasted_iota(jnp.int32, (page, page), 1)
    after = jnp.where(row > col, 1.0, 0.0).astype(bf16)
    lf_all = lbuf[slot].reshape(npg * n_heads, page)
    hi, mid, lo = _split3(lf_all)
    suf = (_dot(hi, after) + _dot(mid, after)) + _dot(lo, after)
    tot = jnp.sum(lf_all, axis=1, keepdims=True)
    carry = jnp.zeros((n_heads, 1), f32)
    for p in range(npg - 1, -1, -1):
        rs = slice(p * n_heads, (p + 1) * n_heads)
        bias_scr[p] = (suf[rs, :] + carry) * LOG2E
        carry = carry + tot[rs, :]

    qb = q_ref[0].astype(bf16)
    qh = [qb[:, h * HEAD_DIM:(h + 1) * HEAD_DIM] for h in range(n_heads)]
    t_row = lax.broadcasted_iota(jnp.int32, (8, page), 0)
    lane = lax.broadcasted_iota(jnp.int32, (8, page), 1)

    def head_rows(buf, p, h):
        return buf[slot, pl.ds(p * rows + h, page, stride=n_heads), :].astype(bf16)

    def score_page(p, mvec, masked):
        out = []
        for h in range(n_heads):
            s = _dot_nt(qh[h], head_rows(kbuf, p, h)) + bias_scr[p, pl.ds(h, 1), :]
            if masked:
                s = jnp.where(lane <= t_row, s, NEG)
            sbuf[p, h] = s
            out.append(jnp.maximum(mvec[h], s))
        return tuple(out)

    unroll = math.gcd(n_pages, PAGE_UNROLL)
    mvec = lax.fori_loop(0, n_pages, functools.partial(score_page, masked=False),
                         tuple(jnp.full((8, page), NEG, f32) for _ in range(n_heads)),
                         unroll=unroll)
    mvec = score_page(n_pages, mvec, True)
    m = [jnp.max(mv, axis=1, keepdims=True) for mv in mvec]

    def value_page(p, carry):
        lvec, acc = carry
        lout, aout = [], []
        for h in range(n_heads):
            pr = jnp.exp2(sbuf[p, h] - m[h])
            lout.append(lvec[h] + pr)
            aout.append(acc[h] + _dot(pr.astype(bf16), head_rows(vbuf, p, h)))
        return tuple(lout), tuple(aout)

    zeros = tuple(jnp.zeros((8, page), f32) for _ in range(n_heads))
    lvec, acc = lax.fori_loop(0, n_pages, value_page, (zeros, zeros), unroll=unroll)
    lvec, acc = value_page(n_pages, (lvec, acc))
    for h in range(n_heads):
        l = jnp.sum(lvec[h], axis=1, keepdims=True)
        o_ref[0, :, h * HEAD_DIM:(h + 1) * HEAD_DIM] = acc[h] / l


def _sample_attn(page_table, q8, knew, vnew, lnew, ck, cv, cl, *, n_heads, n_new):
    n_batch, n_pages = page_table.shape
    page = cl.shape[-1]
    assert page == HEAD_DIM == LANES
    a = q8.shape[-1]
    rows = (n_pages + 1) * page * n_heads
    any_spec = pl.BlockSpec(memory_space=pl.ANY)
    q_blk = pl.BlockSpec((1, 8, a), lambda b, pt: (b, 0, 0))
    grid_spec = pltpu.PrefetchScalarGridSpec(
        num_scalar_prefetch=1, grid=(n_batch,),
        in_specs=[q_blk, any_spec, any_spec, any_spec, any_spec, any_spec, any_spec],
        out_specs=q_blk,
        scratch_shapes=[
            pltpu.VMEM((2, rows, HEAD_DIM), f32),
            pltpu.VMEM((2, rows, HEAD_DIM), f32),
            pltpu.VMEM((2, n_pages + 1, n_heads, page), f32),
            pltpu.VMEM((n_pages + 1, n_heads, page), f32),
            pltpu.VMEM((n_pages + 1, n_heads, 8, page), f32),
            pltpu.SemaphoreType.DMA((2, 3)),
        ])
    return pl.pallas_call(
        functools.partial(_sample_attn_kernel, n_pages=n_pages, n_batch=n_batch,
                          n_heads=n_heads, n_new=n_new, page=page),
        grid_spec=grid_spec, out_shape=jax.ShapeDtypeStruct(q8.shape, f32),
        compiler_params=_params(), name="sample_attn")(
            page_table.reshape(-1), q8, knew, vnew, lnew, ck, cv, cl)


def _layer_norm(y, g, b):
    mu = jnp.mean(y, axis=-1, keepdims=True)
    yc = y - mu
    var = jnp.mean(yc * yc, axis=-1, keepdims=True)
    return yc * lax.rsqrt(var + LN_EPS) * g + b


def _finish1_kernel(ap_ref, cp_ref, xp_ref, as_ref, cs_ref, xs_ref,
                    woa_ref, woc_ref, g_ref, b_ref, wr2_ref, wrh_ref, br_ref,
                    x1_ref, x1g_ref, meta_ref, cnt_ref, gw_ref, cnt_scr, *, alpha, nb_prompt, tm):
    is_prompt = pl.program_id(0) < nb_prompt
    attn = jnp.where(is_prompt, ap_ref[...], as_ref[...])
    conv = jnp.where(is_prompt, cp_ref[...], cs_ref[...])
    x = jnp.where(is_prompt, xp_ref[...], xs_ref[...])
    o = _dot(attn, woa_ref[...]) + _dot(conv, woc_ref[...])
    x1 = _layer_norm(alpha * x + o, g_ref[...], b_ref[...])
    x1_ref[...] = x1
    n_chunks = x1.shape[1] // LANES
    for c in range(n_chunks):
        x1g_ref[pl.ds(c, tm, stride=n_chunks), :] = x1[:, c * LANES:(c + 1) * LANES]

    x1b = x1.astype(bf16)
    x1l = (x1 - x1b.astype(f32)).astype(bf16)
    r2 = _dot(x1b, wr2_ref[...])
    lr = r2[:, :LANES] + r2[:, LANES:] + _dot(x1l, wrh_ref[...]) + br_ref[...]
    lane = lax.broadcasted_iota(jnp.int32, lr.shape, 1)

    def first_max(mask):
        vals = jnp.where(mask, lr, -jnp.inf)
        mx = jnp.max(vals, axis=-1, keepdims=True)
        idx = jnp.min(jnp.where(mask & (vals == mx), lane, LANES), axis=-1, keepdims=True)
        return mx, idx

    is_group = lane < N_GROUPS
    gmax, gidx = first_max(is_group)
    p_group = 1.0 / jnp.sum(jnp.where(is_group, jnp.exp(lr - gmax), 0.0), axis=-1, keepdims=True)
    lo = N_GROUPS + gidx * EXPERTS_PER_GROUP
    in_group = (lane >= lo) & (lane < lo + EXPERTS_PER_GROUP)
    e1, i1 = first_max(in_group)
    e2, i2 = first_max(in_group & (lane != i1))
    r = jnp.exp(e2 - e1)
    w1 = p_group / (1.0 + r)
    w2 = p_group * r / (1.0 + r)
    gw_ref[...] = jnp.where(lane == 0, w1, jnp.where(lane == 1, w2, 0.0))

    @pl.when(pl.program_id(0) == 0)
    def _():
        cnt_scr[...] = jnp.zeros(cnt_scr.shape, f32)

    id1, id2 = i1 - N_GROUPS, i2 - N_GROUPS
    sel1 = jnp.where(lane == id1, 1.0, 0.0)
    sel2 = jnp.where(lane == id2, 1.0, 0.0)
    row = lax.broadcasted_iota(jnp.int32, (tm, tm), 0)
    col = lax.broadcasted_iota(jnp.int32, (tm, tm), 1)
    before = jnp.where(col < row, 1.0, 0.0).astype(bf16)
    seen = cnt_scr[0:1, :]
    tot1 = jnp.sum(sel1, axis=0, keepdims=True)
    rank1 = jnp.sum(sel1 * (seen + _dot(before, sel1.astype(bf16))), axis=-1, keepdims=True)
    rank2 = jnp.sum(sel2 * ((seen + tot1) + _dot(before, sel2.astype(bf16))), axis=-1, keepdims=True)
    seen = (seen + tot1) + jnp.sum(sel2, axis=0, keepdims=True)
    cnt_scr[...] = jnp.broadcast_to(seen, cnt_scr.shape)
    cnt_ref[...] = jnp.broadcast_to(seen, cnt_ref.shape)
    meta = jnp.where(lane == 0, id1.astype(f32), jnp.where(lane == 1, id2.astype(f32),
                     jnp.where(lane == 2, rank1, jnp.where(lane == 3, rank2, 0.0))))
    meta_ref[...] = meta.T[0:8, :]


def _finish1(attn_p, conv_p, x_p, attn_s, conv_s, x_s, wo, g, b, wr2, wrh, br, *, alpha):
    (t_p, d), t_s = x_p.shape, x_s.shape[0]
    a = attn_p.shape[1]
    assert wo.shape[0] == 2 * a
    tm = min(TM_PROJ, t_s)
    nb_p, nb_s = t_p // tm, t_s // tm
    t_all = t_p + t_s
    p_blk = lambda w: pl.BlockSpec((tm, w), lambda i: (jnp.minimum(i, nb_p - 1), 0))
    s_blk = lambda w: pl.BlockSpec((tm, w), lambda i: (jnp.maximum(i - nb_p, 0), 0))
    out_blk = lambda w: pl.BlockSpec((tm, w), lambda i: (i, 0))
    in_specs = [p_blk(a), p_blk(a), p_blk(d), s_blk(a), s_blk(a), s_blk(d),
                pl.BlockSpec((a, d), lambda i: (0, 0), pipeline_mode=pl.Buffered(1)),
                pl.BlockSpec((a, d), lambda i: (1, 0), pipeline_mode=pl.Buffered(1)),
                _resident(g.shape), _resident(b.shape), _resident(wr2.shape),
                _resident(wrh.shape), _resident(br.shape)]
    n_chunks = d // LANES
    return pl.pallas_call(
        functools.partial(_finish1_kernel, alpha=alpha, nb_prompt=nb_p, tm=tm),
        grid=(nb_p + nb_s,), in_specs=in_specs,
        out_specs=[out_blk(d), pl.BlockSpec((tm * n_chunks, LANES), lambda i: (i, 0)),
                   pl.BlockSpec((8, tm), lambda i: (0, i)),
                   pl.BlockSpec((8, LANES), lambda i: (0, 0)), out_blk(LANES)],
        out_shape=[jax.ShapeDtypeStruct((t_all, d), f32),
                   jax.ShapeDtypeStruct((t_all * n_chunks, LANES), f32),
                   jax.ShapeDtypeStruct((8, t_all), f32),
                   jax.ShapeDtypeStruct((8, LANES), f32),
                   jax.ShapeDtypeStruct((t_all, LANES), f32)],
        scratch_shapes=[pltpu.VMEM((8, LANES), f32)],
        compiler_params=_params(), name="finish1")(
            attn_p, conv_p, x_p, attn_s, conv_s, x_s, wo, wo, g, b, wr2, wrh, br)


def _moe_kernel(te_ref, ord_ref, nxt_ref, nu_ref, idx_ref, x_hbm, wg_hbm, wu_hbm, wd_hbm, y_ref,
                xbuf, wgf, wuf, wdf, wgb, wub, wdb, sem, wsem, *, tm):
    s = pl.program_id(0)
    n_used = nu_ref[0]

    def weight_copies(e, wslot):
        return (pltpu.make_async_copy(wg_hbm.at[e], wgf.at[wslot], wsem.at[wslot, 0]),
                pltpu.make_async_copy(wu_hbm.at[e], wuf.at[wslot], wsem.at[wslot, 1]),
                pltpu.make_async_copy(wd_hbm.at[e], wdf.at[wslot], wsem.at[wslot, 2]))

    @pl.when(s == 0)
    def _():
        for cp in weight_copies(te_ref[0], 0):
            cp.start(priority=1)

    @pl.when(s < n_used)
    def _():
        slot = s % 2

        def issue(r, c):
            tok = idx_ref[0, 0, r]
            pltpu.make_async_copy(x_hbm.at[tok], xbuf.at[slot, :, r, :], sem.at[slot]).start()
            return c

        lax.fori_loop(0, tm, issue, 0, unroll=8)

    t = s - 1

    @pl.when((t >= 0) & (t < n_used))
    def _():
        slot = t % 2
        pltpu.make_async_copy(xbuf.at[slot], xbuf.at[slot], sem.at[slot]).wait()
        e = te_ref[t]
        new_expert = (t == 0) | (e != te_ref[jnp.maximum(t - 1, 0)])

        @pl.when(new_expert)
        def _():
            wslot = ord_ref[t] % 2
            for cp in weight_copies(e, wslot):
                cp.wait()
            nxt = nxt_ref[e]

            @pl.when(nxt >= 0)
            def _():
                for cp in weight_copies(nxt, 1 - wslot):
                    cp.start(priority=1)

            wgb[...] = wgf[wslot].astype(bf16)
            wub[...] = wuf[wslot].astype(bf16)
            wdb[...] = wdf[wslot].astype(bf16)

        xb = jnp.concatenate([xbuf[slot, c] for c in range(xbuf.shape[1])], axis=-1).astype(bf16)
        hg = _dot(xb, wgb[...])
        hu = _dot(xb, wub[...])
        act = hg * (1.0 / (1.0 + jnp.exp(-hg))) * hu
        y_ref[...] = _dot(act.astype(bf16), wdb[...])

    @pl.when(t >= n_used)
    def _():
        y_ref[...] = jnp.zeros(y_ref.shape, f32)


def _moe(x1g, tok_of, tile_expert, tile_ord, next_expert, n_used, wg, wu, wd):
    n_tiles, _, tm = tok_of.shape
    _, d, f = wg.shape
    last = n_tiles - 1
    any_spec = pl.BlockSpec(memory_space=pl.ANY)
    grid_spec = pltpu.PrefetchScalarGridSpec(
        num_scalar_prefetch=4, grid=(n_tiles + 1,),
        in_specs=[pl.BlockSpec((1, 1, tm), lambda s, *_: (jnp.minimum(s, last), 0, 0),
                               memory_space=pltpu.SMEM),
                  any_spec, any_spec, any_spec, any_spec],
        out_specs=pl.BlockSpec((tm, d), lambda s, *_: (jnp.maximum(s - 1, 0), 0)),
        scratch_shapes=[pltpu.VMEM((2, d // LANES, tm, LANES), f32),
                        pltpu.VMEM((2, d, f), f32), pltpu.VMEM((2, d, f), f32),
                        pltpu.VMEM((2, f, d), f32),
                        pltpu.VMEM((d, f), bf16), pltpu.VMEM((d, f), bf16), pltpu.VMEM((f, d), bf16),
                        pltpu.SemaphoreType.DMA((2,)), pltpu.SemaphoreType.DMA((2, 3))])
    return pl.pallas_call(
        functools.partial(_moe_kernel, tm=tm),
        grid_spec=grid_spec, out_shape=jax.ShapeDtypeStruct((n_tiles * tm, d), f32),
        compiler_params=_params(), name="moe")(
            tile_expert, tile_ord, next_expert, n_used, tok_of, x1g, wg, wu, wd)


def _finish2_kernel(idx_ref, ys_hbm, x1_ref, gw_ref, g_ref, b_ref, o_ref, gbuf, sem,
                    *, tm, n_steps, alpha):
    s = pl.program_id(0)

    @pl.when(s < n_steps)
    def _():
        slot = s % 2

        def issue(r, c):
            row = idx_ref[0, 0, r]
            pltpu.make_async_copy(ys_hbm.at[pl.ds(row, 1)], gbuf.at[slot, pl.ds(r, 1)],
                                  sem.at[slot]).start()
            return c

        lax.fori_loop(0, 2 * tm, issue, 0, unroll=8)

    @pl.when(s >= 1)
    def _():
        slot = (s - 1) % 2
        pltpu.make_async_copy(ys_hbm.at[pl.ds(0, 2 * tm)], gbuf.at[slot], sem.at[slot]).wait()
        gw = gw_ref[...]
        f = gw[:, 0:1] * gbuf[slot, 0:tm, :] + gw[:, 1:2] * gbuf[slot, tm:2 * tm, :]
        o_ref[...] = _layer_norm(alpha * x1_ref[...] + f, g_ref[...], b_ref[...])


def _finish2(ys, pos, x1, gw, g, b, *, alpha, row0, t):
    n_steps, _, two_tm = pos.shape
    tm = two_tm // 2
    d = x1.shape[1]
    blk0 = row0 // tm
    last = n_steps - 1
    in_blk = lambda w: pl.BlockSpec((tm, w), lambda s: (jnp.maximum(s - 1, 0) + blk0, 0))
    return pl.pallas_call(
        functools.partial(_finish2_kernel, tm=tm, n_steps=n_steps, alpha=alpha),
        grid=(n_steps + 1,),
        in_specs=[pl.BlockSpec((1, 1, two_tm), lambda s: (jnp.minimum(s, last), 0, 0),
                               memory_space=pltpu.SMEM),
                  pl.BlockSpec(memory_space=pl.ANY),
                  in_blk(d), in_blk(LANES), _resident(g.shape), _resident(b.shape)],
        out_specs=pl.BlockSpec((tm, d), lambda s: (jnp.maximum(s - 1, 0), 0)),
        out_shape=jax.ShapeDtypeStruct((t, d), f32),
        scratch_shapes=[pltpu.VMEM((2, two_tm, d), f32), pltpu.SemaphoreType.DMA((2,))],
        compiler_params=_params(), name="finish2")(pos, ys, x1, gw, g, b)


def _bias_selectors(n_heads, sign):
    sel = np.zeros((3, n_heads, n_heads * HEAD_DIM), np.float32)
    for p in range(3):
        for h in range(n_heads):
            sel[p, h, h * HEAD_DIM + p] = sign
    return sel


def _route_tables(meta, counts, tm):
    n_tok = meta.shape[1]
    e = meta[0:2].astype(jnp.int32)
    rank = meta[2:4].astype(jnp.int32)
    counts = counts[0, :N_EXPERTS].astype(jnp.int32)
    n_pairs = 2 * n_tok
    padded = ((counts + tm - 1) // tm) * tm
    ends = jnp.cumsum(padded)
    experts = jnp.arange(N_EXPERTS, dtype=jnp.int32)
    first_row = jnp.sum(jnp.where(e[..., None] == experts, ends - padded, 0), axis=-1)
    pos = first_row + rank
    n_tiles = (n_pairs + N_EXPERTS * (tm - 1)) // tm
    rows = n_tiles * tm
    tok = jnp.broadcast_to(jnp.arange(n_tok, dtype=jnp.int32)[None, :], (2, n_tok))
    tok_of = jnp.zeros((rows,), jnp.int32).at[pos.reshape(-1)].set(
        tok.reshape(-1), unique_indices=True, mode="promise_in_bounds")
    tile_start = jnp.arange(n_tiles, dtype=jnp.int32) * tm
    tile_expert = jnp.minimum(
        jnp.sum((ends[None, :] <= tile_start[:, None]).astype(jnp.int32), axis=1), N_EXPERTS - 1)
    n_used = (ends[-1] // tm).astype(jnp.int32).reshape(1)
    nonempty = counts > 0
    expert_ord = jnp.cumsum(nonempty.astype(jnp.int32)) - 1
    later = nonempty[None, :] & (experts[None, :] > experts[:, None])
    next_expert = jnp.min(jnp.where(later, experts[None, :], N_EXPERTS), axis=1)
    next_expert = jnp.where(next_expert < N_EXPERTS, next_expert, -1).astype(jnp.int32)
    tile_ord = expert_ord[tile_expert].astype(jnp.int32)
    return pos, tok_of.reshape(n_tiles, 1, tm), tile_expert, tile_ord, next_expert, n_used


def _tile_pos(pos, tm):
    t = pos.shape[1]
    return jnp.concatenate([pos[0].reshape(t // tm, 1, tm), pos[1].reshape(t // tm, 1, tm)], axis=2)


def kernel(x_prompt, x_sample, cache_k, cache_v, cache_logf, state_conv, page_table, w_in, b_f,
           w_conv, w_out, ln1_g, ln1_b, w_rg, b_rg, w_re, b_re, w_gate, w_up, w_down, ln2_g, ln2_b):
    depth, d_model, _ = w_in.shape
    n_batch_p, seq, _ = x_prompt.shape
    db, dseq, _ = x_sample.shape
    n_heads = cache_k.shape[3]
    a = n_heads * HEAD_DIM
    c = w_conv.shape[-1]
    assert n_batch_p == 1 and dseq <= 8 and w_conv.shape[1] == CONV_K
    alpha = (2.0 * depth) ** 0.25
    qscale = HEAD_DIM ** -0.5 * LOG2E
    t_p, t_s = seq, db * dseq
    t_all = t_p + t_s

    sel_neg = _bias_selectors(n_heads, -1.0)
    psel_k = jnp.asarray(np.concatenate(
        [np.pad(sel_neg[p], ((0, LANES - n_heads), (0, 0))) for p in range(3)], axis=0), bf16)

    xp = x_prompt.reshape(t_p, d_model)
    xs = x_sample.reshape(t_s, d_model)
    outs = {k: [] for k in ("kp", "vp", "fp", "cp", "ks", "vs", "fs", "cs")}
    for l in range(depth):
        wl = jnp.swapaxes(w_in[l], 0, 1)
        wq, wk, wv = (wl[j * a:(j + 1) * a].astype(bf16) for j in range(3))
        o = 3 * a
        wf = jnp.pad(wl[o:o + n_heads], ((0, LANES - n_heads), (0, 0)))
        wfh = wf.astype(bf16)
        wf2 = jnp.concatenate([wfh, (wf - wfh.astype(f32)).astype(bf16)], axis=0)
        bfp = jnp.pad(b_f[l], (0, LANES - n_heads)).reshape(1, LANES)
        o += n_heads
        wb, wc, wh = (wl[o + j * c:o + (j + 1) * c].astype(bf16) for j in range(3))
        wo = w_out[l].astype(bf16)
        wr = jnp.pad(jnp.concatenate([w_rg[l], w_re[l]], axis=1),
                     ((0, 0), (0, LANES - N_GROUPS - N_EXPERTS)))
        wrh = wr.astype(bf16)
        wr2 = jnp.concatenate([wrh, (wr - wrh.astype(f32)).astype(bf16)], axis=1)
        br = jnp.pad(jnp.concatenate([b_rg[l], b_re[l]]), (0, LANES - N_GROUPS - N_EXPERTS)).reshape(1, LANES)
        g1, b1 = ln1_g[l].reshape(1, -1), ln1_b[l].reshape(1, -1)
        g2, b2 = ln2_g[l].reshape(1, -1), ln2_b[l].reshape(1, -1)

        q, ko, vo, kb, vb, lf, kx = _qkv_proj(xp, wq, wk, wv, wf2, wfh, bfp, psel_k,
                                              n_heads=n_heads, qscale=qscale, with_cum=True)
        conv, utail = _conv_proj(xp, wb, wc, wh, w_conv[l])
        attn = _prompt_attn(q, kb, kx, vb, n_heads=n_heads)
        outs["kp"].append(ko.reshape(1, t_p, n_heads, HEAD_DIM))
        outs["vp"].append(vo.reshape(1, t_p, n_heads, HEAD_DIM))
        outs["fp"].append(lf[:, :n_heads].reshape(1, t_p, n_heads))
        outs["cp"].append(utail[8 - (CONV_K - 1):].reshape(1, CONV_K - 1, c))

        qs, kos, vos, _, _, lfs = _qkv_proj(xs, wq, wk, wv, wf2, wfh, bfp, None,
                                            n_heads=n_heads, qscale=qscale, with_cum=False)
        st = state_conv[l]
        zero = jnp.zeros((db, dseq, c), f32)
        p1 = zero.at[:, 0].set(st[:, 1]).reshape(t_s, c)
        p2 = zero.at[:, 0].set(st[:, 0]).at[:, 1].set(st[:, 1]).reshape(t_s, c)
        conv_s, u_s = _conv_proj(xs, wb, wc, wh, w_conv[l], p1, p2, group=dseq)
        q8 = jnp.pad(qs.reshape(db, dseq, a), ((0, 0), (0, 8 - dseq), (0, 0))).astype(f32)
        page = cache_k.shape[2]
        lnew = jnp.pad(jnp.swapaxes(lfs[:, :n_heads].reshape(db, dseq, n_heads), 1, 2),
                       ((0, 0), (0, 0), (0, page - dseq)))
        attn_s = _sample_attn(page_table, q8, kos, vos, lnew,
                              cache_k[l].reshape(-1, HEAD_DIM), cache_v[l].reshape(-1, HEAD_DIM),
                              jnp.swapaxes(cache_logf[l], 1, 2), n_heads=n_heads, n_new=dseq)
        attn_s = attn_s[:, :dseq].reshape(t_s, a).astype(bf16)
        outs["ks"].append(kos.reshape(db, dseq, n_heads, HEAD_DIM))
        outs["vs"].append(vos.reshape(db, dseq, n_heads, HEAD_DIM))
        outs["fs"].append(lfs[:, :n_heads].reshape(db, dseq, n_heads))
        outs["cs"].append(u_s.reshape(db, dseq, c)[:, dseq - (CONV_K - 1):])
        x1, x1g, meta, counts, gw = _finish1(attn, conv, xp, attn_s, conv_s, xs, wo, g1, b1,
                                             wr2, wrh, br, alpha=alpha)

        pos, tok_of, tile_expert, tile_ord, next_expert, n_used = _route_tables(
            meta, counts, TM_MOE)
        ys = _moe(x1g.reshape(t_all, d_model // LANES, LANES), tok_of, tile_expert, tile_ord,
                  next_expert, n_used, w_gate[l], w_up[l], w_down[l])
        tm2 = min(TM_PROJ, t_s)
        xp = _finish2(ys, _tile_pos(pos[:, :t_p], TM_PROJ), x1, gw, g2, b2, alpha=alpha, row0=0, t=t_p)
        xs = _finish2(ys, _tile_pos(pos[:, t_p:], tm2), x1, gw, g2, b2, alpha=alpha, row0=t_p, t=t_s)

    stack = lambda k: jnp.stack(outs[k])
    return (xp.reshape(x_prompt.shape), xs.reshape(x_sample.shape),
            stack("kp"), stack("vp"), stack("fp"), stack("cp"),
            stack("ks"), stack("vs"), stack("fs"), stack("cs"))
```

```python
import functools
import math

import numpy as np
import jax
import jax.numpy as jnp
from jax import lax
from jax.experimental import pallas as pl
from jax.experimental.pallas import tpu as pltpu

f32 = jnp.float32
bf16 = jnp.bfloat16

LANES = 128
HEAD_DIM = 128
LOG2E = 1.4426950408889634
LN_EPS = 1e-5
NEG = -1e30
CONV_K = 3
N_GROUPS = 4
EXPERTS_PER_GROUP = 8
N_EXPERTS = N_GROUPS * EXPERTS_PER_GROUP
VMEM_LIMIT = 56 * 1024 * 1024

TM_PROJ = 256
TQ = 512
N_SUB = 2
PAGE_UNROLL = 4
TM_MOE = 256
ROW_PITCH_PAD = 1


def _dot(a, b):
    return jnp.dot(a, b, preferred_element_type=f32)


def _dot_nt(a, b):
    return lax.dot_general(a, b, (((1,), (1,)), ((), ())), preferred_element_type=f32)


def _lane_tile(x, n):
    return jnp.concatenate([x] * n, axis=-1)


def _split3(x):
    hi = x.astype(bf16)
    r1 = x - hi.astype(f32)
    mid = r1.astype(bf16)
    lo = (r1 - mid.astype(f32)).astype(bf16)
    return hi, mid, lo


def _params(sem=("arbitrary",)):
    return pltpu.CompilerParams(dimension_semantics=sem, vmem_limit_bytes=VMEM_LIMIT)


def _resident(shape):
    nd = len(shape)
    return pl.BlockSpec(shape, lambda *_: (0,) * nd, pipeline_mode=pl.Buffered(1))


def _qkv_kernel(x_ref, wq_ref, wk_ref, wv_ref, wf2_ref, wfh_ref, bf_ref, psel_ref,
                q_ref, ko_ref, vo_ref, kb_ref, vb_ref, lf_ref, kx_ref, carry_ref,
                *, tm, n_heads, qscale):
    i = pl.program_id(0)
    x = x_ref[...]
    xb = x.astype(bf16)
    q_ref[...] = (_dot_nt(xb, wq_ref[...]) * qscale).astype(bf16)
    k = _dot_nt(xb, wk_ref[...])
    kb_ref[...] = k.astype(bf16)
    v = _dot_nt(xb, wv_ref[...])
    vb_ref[...] = v.astype(bf16)
    for h in range(n_heads):
        ko_ref[pl.ds(h, tm, stride=n_heads), :] = k[:, h * HEAD_DIM:(h + 1) * HEAD_DIM]
        vo_ref[pl.ds(h, tm, stride=n_heads), :] = v[:, h * HEAD_DIM:(h + 1) * HEAD_DIM]

    xl = (x - xb.astype(f32)).astype(bf16)
    zf2 = _dot_nt(xb, wf2_ref[...])
    zf = zf2[:, :LANES] + zf2[:, LANES:] + _dot_nt(xl, wfh_ref[...]) + bf_ref[...]
    lf = jnp.minimum(zf, 0.0) - jnp.log1p(jnp.exp(-jnp.abs(zf)))
    lf_ref[...] = lf

    if kx_ref is not None:
        @pl.when(i == 0)
        def _():
            carry_ref[...] = jnp.zeros(carry_ref.shape, f32)

        row = lax.broadcasted_iota(jnp.int32, (tm, tm), 0)
        col = lax.broadcasted_iota(jnp.int32, (tm, tm), 1)
        tri = jnp.where(row >= col, 1.0, 0.0).astype(bf16)
        c3 = _dot(tri, jnp.concatenate(_split3(lf), axis=-1))
        cum = (c3[:, :LANES] + c3[:, LANES:2 * LANES] + c3[:, 2 * LANES:]) + carry_ref[0:1, :]
        carry_ref[...] = jnp.broadcast_to(cum[tm - 1:tm, :], carry_ref.shape)
        parts = jnp.concatenate(_split3(cum * LOG2E), axis=-1)
        kx_ref[...] = _dot(parts, psel_ref[...]).astype(bf16)


def _qkv_proj(x, wq, wk, wv, wf2, wfh, bfp, psel, *, n_heads, qscale, with_cum):
    t, d = x.shape
    a = wq.shape[0]
    tm = min(TM_PROJ, t)
    grid = (t // tm,)
    row_blk = lambda w: pl.BlockSpec((tm, w), lambda i: (i, 0))
    in_specs = [row_blk(d), _resident(wq.shape), _resident(wk.shape), _resident(wv.shape),
                _resident(wf2.shape), _resident(wfh.shape), _resident(bfp.shape)]
    args = [x, wq, wk, wv, wf2, wfh, bfp]
    out_shape = [jax.ShapeDtypeStruct((t, a), bf16),
                 jax.ShapeDtypeStruct((t * n_heads, HEAD_DIM), f32),
                 jax.ShapeDtypeStruct((t * n_heads, HEAD_DIM), f32),
                 jax.ShapeDtypeStruct((t, a), bf16),
                 jax.ShapeDtypeStruct((t, a), bf16),
                 jax.ShapeDtypeStruct((t, LANES), f32)]
    hd_blk = pl.BlockSpec((tm * n_heads, HEAD_DIM), lambda i: (i, 0))
    out_specs = [row_blk(a), hd_blk, hd_blk, row_blk(a), row_blk(a), row_blk(LANES)]
    scratch = []
    if with_cum:
        in_specs.append(_resident(psel.shape))
        args.append(psel)
        out_shape.append(jax.ShapeDtypeStruct((t, a), bf16))
        out_specs.append(row_blk(a))
        scratch.append(pltpu.VMEM((8, LANES), f32))

        def body(*refs):
            _qkv_kernel(*refs, tm=tm, n_heads=n_heads, qscale=qscale)
    else:
        def body(x_ref, wq_ref, wk_ref, wv_ref, wf2_ref, wfh_ref, bf_ref, *outs):
            _qkv_kernel(x_ref, wq_ref, wk_ref, wv_ref, wf2_ref, wfh_ref, bf_ref, None,
                        *outs, None, None, tm=tm, n_heads=n_heads, qscale=qscale)

    return pl.pallas_call(
        body, grid=grid, in_specs=in_specs, out_specs=out_specs, out_shape=out_shape,
        scratch_shapes=scratch, compiler_params=_params(), name="qkv_proj")(*args)


def _conv_kernel(x_ref, wb_ref, wc_ref, wh_ref, wconv_ref, p1_ref, p2_ref,
                 conv_ref, u_ref, ubuf, *, tm, group):
    i = pl.program_id(0)
    xb = x_ref[...].astype(bf16)
    gate_b = _dot_nt(xb, wb_ref[...])
    u = _dot_nt(xb, wc_ref[...]) * _dot_nt(xb, wh_ref[...])
    w0 = wconv_ref[0:1, :]
    w1 = wconv_ref[1:2, :]
    w2 = wconv_ref[2:3, :]
    if group is None:
        @pl.when(i == 0)
        def _():
            ubuf[0:8, :] = jnp.zeros((8, ubuf.shape[1]), f32)

        ubuf[8:8 + tm, :] = u
        u1 = ubuf[pl.ds(7, tm), :]
        u2 = ubuf[pl.ds(6, tm), :]
        ubuf[0:8, :] = u[tm - 8:tm, :]
        u_ref[...] = u[tm - 8:tm, :]
    else:
        ubuf[0:8, :] = jnp.zeros((8, ubuf.shape[1]), f32)
        ubuf[8:8 + tm, :] = u
        t_in = lax.broadcasted_iota(jnp.int32, (tm, 1), 0) % group
        u1 = jnp.where(t_in >= 1, ubuf[pl.ds(7, tm), :], p1_ref[...])
        u2 = jnp.where(t_in >= 2, ubuf[pl.ds(6, tm), :], p2_ref[...])
        u_ref[...] = u
    y = (u2 * w0 + u1 * w1) + u * w2
    conv_ref[...] = (gate_b * y).astype(bf16)


def _conv_proj(x, wb, wc, wh, wconv, p1=None, p2=None, *, group=None):
    t, d = x.shape
    c = wb.shape[0]
    tm = min(TM_PROJ, t)
    grid = (t // tm,)
    row_blk = lambda w: pl.BlockSpec((tm, w), lambda i: (i, 0))
    in_specs = [row_blk(d), _resident(wb.shape), _resident(wc.shape), _resident(wh.shape),
                _resident(wconv.shape)]
    args = [x, wb, wc, wh, wconv]
    if group is None:
        u_shape = jax.ShapeDtypeStruct((8, c), f32)
        u_spec = pl.BlockSpec((8, c), lambda i: (0, 0))

        def body(x_ref, wb_ref, wc_ref, wh_ref, wconv_ref, conv_ref, u_ref, ubuf):
            _conv_kernel(x_ref, wb_ref, wc_ref, wh_ref, wconv_ref, None, None,
                         conv_ref, u_ref, ubuf, tm=tm, group=None)
    else:
        in_specs += [row_blk(c), row_blk(c)]
        args += [p1, p2]
        u_shape = jax.ShapeDtypeStruct((t, c), f32)
        u_spec = row_blk(c)

        def body(*refs):
            _conv_kernel(*refs, tm=tm, group=group)

    return pl.pallas_call(
        body, grid=grid, in_specs=in_specs,
        out_specs=[row_blk(c), u_spec],
        out_shape=[jax.ShapeDtypeStruct((t, c), bf16), u_shape],
        scratch_shapes=[pltpu.VMEM((tm + 8, c), f32)],
        compiler_params=_params(), name="conv_proj")(*args)


def _prompt_attn_kernel(q_ref, k_ref, kx_ref, v_ref, o_ref, *scratch, tq, n_sub):
    m_scr, acc_scr = scratch[:n_sub], scratch[n_sub:2 * n_sub]
    s_a, s_b = scratch[2 * n_sub:3 * n_sub], scratch[3 * n_sub:]
    i = pl.program_id(1)
    ts = tq // n_sub
    ones_q = jnp.ones((ts, LANES), bf16)
    qa = [jnp.concatenate([q_ref[u * ts:(u + 1) * ts, :], ones_q], axis=-1) for u in range(n_sub)]
    ones_v = jnp.ones((tq, LANES), bf16)
    for u in range(n_sub):
        m_scr[u][...] = jnp.full(m_scr[u].shape, NEG, f32)
        acc_scr[u][...] = jnp.zeros(acc_scr[u].shape, f32)

    def scores(j, s_buf):
        ks = pl.multiple_of(j * tq, tq)
        ka = jnp.concatenate([k_ref[pl.ds(ks, tq), :], kx_ref[pl.ds(ks, tq), :]], axis=-1)
        for u in range(n_sub):
            s_buf[u][...] = _dot_nt(qa[u], ka)

    def accumulate(j, s_buf, masked):
        ks = pl.multiple_of(j * tq, tq)
        va = jnp.concatenate([v_ref[pl.ds(ks, tq), :], ones_v], axis=-1)
        for u in range(n_sub):
            s = s_buf[u][...]
            if masked:
                row = lax.broadcasted_iota(jnp.int32, (ts, tq), 0) + u * ts
                col = lax.broadcasted_iota(jnp.int32, (ts, tq), 1)
                s = jnp.where(col <= row, s, NEG)
            m_prev = m_scr[u][...]
            m_new = jnp.maximum(m_prev, jnp.max(s, axis=1, keepdims=True))
            p = jnp.exp2(s - _lane_tile(m_new, tq // LANES))
            alpha = jnp.exp2(m_prev - m_new)
            acc_scr[u][...] = acc_scr[u][...] * _lane_tile(alpha, 2) + _dot(p.astype(bf16), va)
            m_scr[u][...] = m_new

    scores(0, s_a)

    def body(jj, c):
        j = 2 * jj
        scores(j + 1, s_b)
        accumulate(j, s_a, False)
        scores(j + 2, s_a)
        accumulate(j + 1, s_b, False)
        return c

    lax.fori_loop(0, i // 2, body, 0)

    @pl.when(i % 2 == 0)
    def _():
        accumulate(i, s_a, True)

    @pl.when(i % 2 == 1)
    def _():
        scores(i, s_b)
        accumulate(i - 1, s_a, False)
        accumulate(i, s_b, True)

    for u in range(n_sub):
        acc = acc_scr[u][...]
        o_ref[u * ts:(u + 1) * ts, :] = (acc[:, :LANES] / acc[:, LANES:]).astype(o_ref.dtype)


def _prompt_attn(q, k, kx, v, *, n_heads):
    s = q.shape[0]
    tq = min(TQ, s)
    grid = (n_heads, s // tq)
    q_blk = pl.BlockSpec((tq, HEAD_DIM), lambda h, i: (i, h))
    kv_blk = pl.BlockSpec((s, HEAD_DIM), lambda h, i: (0, h))
    return pl.pallas_call(
        functools.partial(_prompt_attn_kernel, tq=tq, n_sub=N_SUB),
        grid=grid, in_specs=[q_blk, kv_blk, kv_blk, kv_blk], out_specs=q_blk,
        out_shape=jax.ShapeDtypeStruct(q.shape, bf16),
        scratch_shapes=([pltpu.VMEM((tq // N_SUB, LANES), f32)] * N_SUB
                        + [pltpu.VMEM((tq // N_SUB, 2 * LANES), f32)] * N_SUB
                        + [pltpu.VMEM((tq // N_SUB, tq), f32)] * (2 * N_SUB)),
        compiler_params=_params(("arbitrary", "arbitrary")), name="prompt_attn")(q, k, kx, v)


def _sample_attn_kernel(pt_ref, q_ref, knew_hbm, vnew_hbm, lnew_hbm, ck_hbm, cv_hbm, cl_hbm,
                        o_ref, kbuf, vbuf, lbuf, bias_scr, sbuf, sem,
                        *, n_pages, n_batch, n_heads, n_new, page):
    b = pl.program_id(0)
    slot = b % 2
    npg = n_pages + 1
    rows = page * n_heads
    new_rows = n_new * n_heads

    def copies(bb, sl):
        cps = []
        for p in range(n_pages):
            pg = pt_ref[bb * n_pages + p]
            src = pl.ds(pl.multiple_of(pg * rows, rows), rows)
            dst = pl.ds(p * rows, rows)
            cps.append(pltpu.make_async_copy(ck_hbm.at[src], kbuf.at[sl, dst], sem.at[sl, 0]))
            cps.append(pltpu.make_async_copy(cv_hbm.at[src], vbuf.at[sl, dst], sem.at[sl, 1]))
            cps.append(pltpu.make_async_copy(cl_hbm.at[pg], lbuf.at[sl, p], sem.at[sl, 2]))
        src = pl.ds(pl.multiple_of(bb * new_rows, new_rows), new_rows)
        dst = pl.ds(n_pages * rows, new_rows)
        cps.append(pltpu.make_async_copy(knew_hbm.at[src], kbuf.at[sl, dst], sem.at[sl, 0]))
        cps.append(pltpu.make_async_copy(vnew_hbm.at[src], vbuf.at[sl, dst], sem.at[sl, 1]))
        cps.append(pltpu.make_async_copy(lnew_hbm.at[bb], lbuf.at[sl, n_pages], sem.at[sl, 2]))
        return cps

    @pl.when(b == 0)
    def _():
        for sl in range(2):
            kbuf[sl, pl.ds(n_pages * rows, rows), :] = jnp.zeros((rows, HEAD_DIM), f32)
            vbuf[sl, pl.ds(n_pages * rows, rows), :] = jnp.zeros((rows, HEAD_DIM), f32)
        for cp in copies(0, 0):
            cp.start()

    @pl.when(b + 1 < n_batch)
    def _():
        for cp in copies(b + 1, 1 - slot):
            cp.start()

    for cp in copies(b, slot):
        cp.wait()

    row = lax.broadcasted_iota(jnp.int32, (page, page), 0)
    col = lax.broadcasted_iota(jnp.int32, (page, page), 1)
    after = jnp.where(row > col, 1.0, 0.0).astype(bf16)
    lf_all = lbuf[slot].reshape(npg * n_heads, page)
    hi, mid, lo = _split3(lf_all)
    suf = (_dot(hi, after) + _dot(mid, after)) + _dot(lo, after)
    tot = jnp.sum(lf_all, axis=1, keepdims=True)
    carry = jnp.zeros((n_heads, 1), f32)
    for p in range(npg - 1, -1, -1):
        rs = slice(p * n_heads, (p + 1) * n_heads)
        bias_scr[p] = (suf[rs, :] + carry) * LOG2E
        carry = carry + tot[rs, :]

    qb = q_ref[0].astype(bf16)
    qh = [qb[:, h * HEAD_DIM:(h + 1) * HEAD_DIM] for h in range(n_heads)]
    t_row = lax.broadcasted_iota(jnp.int32, (8, page), 0)
    lane = lax.broadcasted_iota(jnp.int32, (8, page), 1)

    def head_rows(buf, p, h):
        return buf[slot, pl.ds(p * rows + h, page, stride=n_heads), :].astype(bf16)

    def score_page(p, mvec, masked):
        out = []
        for h in range(n_heads):
            s = _dot_nt(qh[h], head_rows(kbuf, p, h)) + bias_scr[p, pl.ds(h, 1), :]
            if masked:
                s = jnp.where(lane <= t_row, s, NEG)
            sbuf[p, h] = s
            out.append(jnp.maximum(mvec[h], s))
        return tuple(out)

    unroll = math.gcd(n_pages, PAGE_UNROLL)
    mvec = lax.fori_loop(0, n_pages, functools.partial(score_page, masked=False),
                         tuple(jnp.full((8, page), NEG, f32) for _ in range(n_heads)),
                         unroll=unroll)
    mvec = score_page(n_pages, mvec, True)
    m = [jnp.max(mv, axis=1, keepdims=True) for mv in mvec]

    def value_page(p, carry):
        lvec, acc = carry
        lout, aout = [], []
        for h in range(n_heads):
            pr = jnp.exp2(sbuf[p, h] - m[h])
            lout.append(lvec[h] + pr)
            aout.append(acc[h] + _dot(pr.astype(bf16), head_rows(vbuf, p, h)))
        return tuple(lout), tuple(aout)

    zeros = tuple(jnp.zeros((8, page), f32) for _ in range(n_heads))
    lvec, acc = lax.fori_loop(0, n_pages, value_page, (zeros, zeros), unroll=unroll)
    lvec, acc = value_page(n_pages, (lvec, acc))
    for h in range(n_heads):
        l = jnp.sum(lvec[h], axis=1, keepdims=True)
        o_ref[0, :, h * HEAD_DIM:(h + 1) * HEAD_DIM] = acc[h] / l


def _sample_attn(page_table, q8, knew, vnew, lnew, ck, cv, cl, *, n_heads, n_new):
    n_batch, n_pages = page_table.shape
    page = cl.shape[-1]
    assert page == HEAD_DIM == LANES
    a = q8.shape[-1]
    rows = (n_pages + 1) * page * n_heads
    any_spec = pl.BlockSpec(memory_space=pl.ANY)
    q_blk = pl.BlockSpec((1, 8, a), lambda b, pt: (b, 0, 0))
    grid_spec = pltpu.PrefetchScalarGridSpec(
        num_scalar_prefetch=1, grid=(n_batch,),
        in_specs=[q_blk, any_spec, any_spec, any_spec, any_spec, any_spec, any_spec],
        out_specs=q_blk,
        scratch_shapes=[
            pltpu.VMEM((2, rows, HEAD_DIM), f32),
            pltpu.VMEM((2, rows, HEAD_DIM), f32),
            pltpu.VMEM((2, n_pages + 1, n_heads, page), f32),
            pltpu.VMEM((n_pages + 1, n_heads, page), f32),
            pltpu.VMEM((n_pages + 1, n_heads, 8, page), f32),
            pltpu.SemaphoreType.DMA((2, 3)),
        ])
    return pl.pallas_call(
        functools.partial(_sample_attn_kernel, n_pages=n_pages, n_batch=n_batch,
                          n_heads=n_heads, n_new=n_new, page=page),
        grid_spec=grid_spec, out_shape=jax.ShapeDtypeStruct(q8.shape, f32),
        compiler_params=_params(), name="sample_attn")(
            page_table.reshape(-1), q8, knew, vnew, lnew, ck, cv, cl)


def _layer_norm(y, g, b):
    mu = jnp.mean(y, axis=-1, keepdims=True)
    yc = y - mu
    var = jnp.mean(yc * yc, axis=-1, keepdims=True)
    return yc * lax.rsqrt(var + LN_EPS) * g + b


def _finish1_kernel(ap_ref, cp_ref, xp_ref, as_ref, cs_ref, xs_ref,
                    woa_ref, woc_ref, g_ref, b_ref, wr2_ref, wrh_ref, br_ref,
                    x1_ref, x1g_ref, meta_ref, cnt_ref, gw_ref, cnt_scr, *, alpha, nb_prompt, tm):
    is_prompt = pl.program_id(0) < nb_prompt
    attn = jnp.where(is_prompt, ap_ref[...], as_ref[...])
    conv = jnp.where(is_prompt, cp_ref[...], cs_ref[...])
    x = jnp.where(is_prompt, xp_ref[...], xs_ref[...])
    o = _dot(attn, woa_ref[...]) + _dot(conv, woc_ref[...])
    x1 = _layer_norm(alpha * x + o, g_ref[...], b_ref[...])
    x1_ref[...] = x1
    n_chunks = x1.shape[1] // LANES
    for c in range(n_chunks):
        x1g_ref[pl.ds(c, tm, stride=n_chunks), :] = x1[:, c * LANES:(c + 1) * LANES]

    x1b = x1.astype(bf16)
    x1l = (x1 - x1b.astype(f32)).astype(bf16)
    r2 = _dot(x1b, wr2_ref[...])
    lr = r2[:, :LANES] + r2[:, LANES:] + _dot(x1l, wrh_ref[...]) + br_ref[...]
    lane = lax.broadcasted_iota(jnp.int32, lr.shape, 1)

    def first_max(mask):
        vals = jnp.where(mask, lr, -jnp.inf)
        mx = jnp.max(vals, axis=-1, keepdims=True)
        idx = jnp.min(jnp.where(mask & (vals == mx), lane, LANES), axis=-1, keepdims=True)
        return mx, idx

    is_group = lane < N_GROUPS
    gmax, gidx = first_max(is_group)
    p_group = 1.0 / jnp.sum(jnp.where(is_group, jnp.exp(lr - gmax), 0.0), axis=-1, keepdims=True)
    lo = N_GROUPS + gidx * EXPERTS_PER_GROUP
    in_group = (lane >= lo) & (lane < lo + EXPERTS_PER_GROUP)
    e1, i1 = first_max(in_group)
    e2, i2 = first_max(in_group & (lane != i1))
    r = jnp.exp(e2 - e1)
    w1 = p_group / (1.0 + r)
    w2 = p_group * r / (1.0 + r)
    gw_ref[...] = jnp.where(lane == 0, w1, jnp.where(lane == 1, w2, 0.0))

    @pl.when(pl.program_id(0) == 0)
    def _():
        cnt_scr[...] = jnp.zeros(cnt_scr.shape, f32)

    id1, id2 = i1 - N_GROUPS, i2 - N_GROUPS
    sel1 = jnp.where(lane == id1, 1.0, 0.0)
    sel2 = jnp.where(lane == id2, 1.0, 0.0)
    row = lax.broadcasted_iota(jnp.int32, (tm, tm), 0)
    col = lax.broadcasted_iota(jnp.int32, (tm, tm), 1)
    before = jnp.where(col < row, 1.0, 0.0).astype(bf16)
    seen = cnt_scr[0:1, :]
    tot1 = jnp.sum(sel1, axis=0, keepdims=True)
    rank1 = jnp.sum(sel1 * (seen + _dot(before, sel1.astype(bf16))), axis=-1, keepdims=True)
    rank2 = jnp.sum(sel2 * ((seen + tot1) + _dot(before, sel2.astype(bf16))), axis=-1, keepdims=True)
    seen = (seen + tot1) + jnp.sum(sel2, axis=0, keepdims=True)
    cnt_scr[...] = jnp.broadcast_to(seen, cnt_scr.shape)
    cnt_ref[...] = jnp.broadcast_to(seen, cnt_ref.shape)
    meta = jnp.where(lane == 0, id1.astype(f32), jnp.where(lane == 1, id2.astype(f32),
                     jnp.where(lane == 2, rank1, jnp.where(lane == 3, rank2, 0.0))))
    meta_ref[...] = meta.T[0:8, :]


def _finish1(attn_p, conv_p, x_p, attn_s, conv_s, x_s, wo, g, b, wr2, wrh, br, *, alpha):
    (t_p, d), t_s = x_p.shape, x_s.shape[0]
    a = attn_p.shape[1]
    assert wo.shape[0] == 2 * a
    tm = min(TM_PROJ, t_s)
    nb_p, nb_s = t_p // tm, t_s // tm
    t_all = t_p + t_s
    p_blk = lambda w: pl.BlockSpec((tm, w), lambda i: (jnp.minimum(i, nb_p - 1), 0))
    s_blk = lambda w: pl.BlockSpec((tm, w), lambda i: (jnp.maximum(i - nb_p, 0), 0))
    out_blk = lambda w: pl.BlockSpec((tm, w), lambda i: (i, 0))
    in_specs = [p_blk(a), p_blk(a), p_blk(d), s_blk(a), s_blk(a), s_blk(d),
                pl.BlockSpec((a, d), lambda i: (0, 0), pipeline_mode=pl.Buffered(1)),
                pl.BlockSpec((a, d), lambda i: (1, 0), pipeline_mode=pl.Buffered(1)),
                _resident(g.shape), _resident(b.shape), _resident(wr2.shape),
                _resident(wrh.shape), _resident(br.shape)]
    n_chunks = d // LANES
    return pl.pallas_call(
        functools.partial(_finish1_kernel, alpha=alpha, nb_prompt=nb_p, tm=tm),
        grid=(nb_p + nb_s,), in_specs=in_specs,
        out_specs=[out_blk(d), pl.BlockSpec((tm * n_chunks, LANES), lambda i: (i, 0)),
                   pl.BlockSpec((8, tm), lambda i: (0, i)),
                   pl.BlockSpec((8, LANES), lambda i: (0, 0)), out_blk(LANES)],
        out_shape=[jax.ShapeDtypeStruct((t_all, d), f32),
                   jax.ShapeDtypeStruct((t_all * n_chunks, LANES), f32),
                   jax.ShapeDtypeStruct((8, t_all), f32),
                   jax.ShapeDtypeStruct((8, LANES), f32),
                   jax.ShapeDtypeStruct((t_all, LANES), f32)],
        scratch_shapes=[pltpu.VMEM((8, LANES), f32)],
        compiler_params=_params(), name="finish1")(
            attn_p, conv_p, x_p, attn_s, conv_s, x_s, wo, wo, g, b, wr2, wrh, br)


def _moe_kernel(te_ref, ord_ref, nxt_ref, nu_ref, idx_ref, x_hbm, wg_hbm, wu_hbm, wd_hbm, y_ref,
                xbuf, wgf, wuf, wdf, wgb, wub, wdb, sem, wsem, *, tm):
    s = pl.program_id(0)
    n_used = nu_ref[0]
    n_chunks = x_hbm.shape[1]
    pitch = n_chunks + ROW_PITCH_PAD

    def weight_copies(e, wslot):
        return (pltpu.make_async_copy(wg_hbm.at[e], wgf.at[wslot], wsem.at[wslot, 0]),
                pltpu.make_async_copy(wu_hbm.at[e], wuf.at[wslot], wsem.at[wslot, 1]),
                pltpu.make_async_copy(wd_hbm.at[e], wdf.at[wslot], wsem.at[wslot, 2]))

    @pl.when(s == 0)
    def _():
        for cp in weight_copies(te_ref[0], 0):
            cp.start(priority=1)

    @pl.when(s < n_used)
    def _():
        slot = s % 2

        def issue(r, c):
            tok = idx_ref[0, 0, r]
            pltpu.make_async_copy(x_hbm.at[tok], xbuf.at[slot, pl.ds(r * pitch, n_chunks)],
                                  sem.at[slot]).start()
            return c

        lax.fori_loop(0, tm, issue, 0, unroll=8)

    t = s - 1

    @pl.when((t >= 0) & (t < n_used))
    def _():
        slot = t % 2
        landed = xbuf.at[slot, pl.ds(0, tm * n_chunks)]
        pltpu.make_async_copy(landed, landed, sem.at[slot]).wait()
        e = te_ref[t]
        new_expert = (t == 0) | (e != te_ref[jnp.maximum(t - 1, 0)])

        @pl.when(new_expert)
        def _():
            wslot = ord_ref[t] % 2
            for cp in weight_copies(e, wslot):
                cp.wait()
            nxt = nxt_ref[e]

            @pl.when(nxt >= 0)
            def _():
                for cp in weight_copies(nxt, 1 - wslot):
                    cp.start(priority=1)

            wgb[...] = wgf[wslot].astype(bf16)
            wub[...] = wuf[wslot].astype(bf16)
            wdb[...] = wdf[wslot].astype(bf16)

        xb = jnp.concatenate([xbuf[slot, pl.ds(c, tm, stride=pitch), :] for c in range(n_chunks)],
                             axis=-1).astype(bf16)
        hg = _dot(xb, wgb[...])
        hu = _dot(xb, wub[...])
        act = hg * (1.0 / (1.0 + jnp.exp(-hg))) * hu
        y = _dot(act.astype(bf16), wdb[...])
        for c in range(n_chunks):
            y_ref[pl.ds(c, tm, stride=n_chunks), :] = y[:, c * LANES:(c + 1) * LANES]

    @pl.when(t >= n_used)
    def _():
        y_ref[...] = jnp.zeros(y_ref.shape, f32)


def _moe(x1g, tok_of, tile_expert, tile_ord, next_expert, n_used, wg, wu, wd):
    n_tiles, _, tm = tok_of.shape
    _, d, f = wg.shape
    n_chunks = d // LANES
    last = n_tiles - 1
    any_spec = pl.BlockSpec(memory_space=pl.ANY)
    grid_spec = pltpu.PrefetchScalarGridSpec(
        num_scalar_prefetch=4, grid=(n_tiles + 1,),
        in_specs=[pl.BlockSpec((1, 1, tm), lambda s, *_: (jnp.minimum(s, last), 0, 0),
                               memory_space=pltpu.SMEM),
                  any_spec, any_spec, any_spec, any_spec],
        out_specs=pl.BlockSpec((tm * n_chunks, LANES), lambda s, *_: (jnp.maximum(s - 1, 0), 0)),
        scratch_shapes=[pltpu.VMEM((2, tm * (n_chunks + ROW_PITCH_PAD), LANES), f32),
                        pltpu.VMEM((2, d, f), f32), pltpu.VMEM((2, d, f), f32),
                        pltpu.VMEM((2, f, d), f32),
                        pltpu.VMEM((d, f), bf16), pltpu.VMEM((d, f), bf16), pltpu.VMEM((f, d), bf16),
                        pltpu.SemaphoreType.DMA((2,)), pltpu.SemaphoreType.DMA((2, 3))])
    return pl.pallas_call(
        functools.partial(_moe_kernel, tm=tm),
        grid_spec=grid_spec, out_shape=jax.ShapeDtypeStruct((n_tiles * tm * n_chunks, LANES), f32),
        compiler_params=_params(), name="moe")(
            tile_expert, tile_ord, next_expert, n_used, tok_of, x1g, wg, wu, wd)


def _finish2_kernel(idx_ref, ys_hbm, x1_ref, gw_ref, g_ref, b_ref, o_ref, gbuf, sem,
                    *, tm, n_steps, alpha):
    s = pl.program_id(0)
    n_chunks = ys_hbm.shape[1]
    pitch = n_chunks + ROW_PITCH_PAD

    @pl.when(s < n_steps)
    def _():
        slot = s % 2

        def issue(r, c):
            row = idx_ref[0, 0, r]
            pltpu.make_async_copy(ys_hbm.at[row], gbuf.at[slot, pl.ds(r * pitch, n_chunks)],
                                  sem.at[slot]).start()
            return c

        lax.fori_loop(0, 2 * tm, issue, 0, unroll=8)

    @pl.when(s >= 1)
    def _():
        slot = (s - 1) % 2
        landed = gbuf.at[slot, pl.ds(0, 2 * tm * n_chunks)]
        pltpu.make_async_copy(landed, landed, sem.at[slot]).wait()

        def rows(first):
            return jnp.concatenate(
                [gbuf[slot, pl.ds(first * pitch + c, tm, stride=pitch), :] for c in range(n_chunks)],
                axis=-1)

        gw = gw_ref[...]
        f = gw[:, 0:1] * rows(0) + gw[:, 1:2] * rows(tm)
        o_ref[...] = _layer_norm(alpha * x1_ref[...] + f, g_ref[...], b_ref[...])


def _finish2(ys, pos, x1, gw, g, b, *, alpha, row0, t):
    n_steps, _, two_tm = pos.shape
    tm = two_tm // 2
    d = x1.shape[1]
    n_chunks = d // LANES
    blk0 = row0 // tm
    last = n_steps - 1
    in_blk = lambda w: pl.BlockSpec((tm, w), lambda s: (jnp.maximum(s - 1, 0) + blk0, 0))
    return pl.pallas_call(
        functools.partial(_finish2_kernel, tm=tm, n_steps=n_steps, alpha=alpha),
        grid=(n_steps + 1,),
        in_specs=[pl.BlockSpec((1, 1, two_tm), lambda s: (jnp.minimum(s, last), 0, 0),
                               memory_space=pltpu.SMEM),
                  pl.BlockSpec(memory_space=pl.ANY),
                  in_blk(d), in_blk(LANES), _resident(g.shape), _resident(b.shape)],
        out_specs=pl.BlockSpec((tm, d), lambda s: (jnp.maximum(s - 1, 0), 0)),
        out_shape=jax.ShapeDtypeStruct((t, d), f32),
        scratch_shapes=[pltpu.VMEM((2, two_tm * (n_chunks + ROW_PITCH_PAD), LANES), f32),
                        pltpu.SemaphoreType.DMA((2,))],
        compiler_params=_params(), name="finish2")(pos, ys, x1, gw, g, b)


def _bias_selectors(n_heads, sign):
    sel = np.zeros((3, n_heads, n_heads * HEAD_DIM), np.float32)
    for p in range(3):
        for h in range(n_heads):
            sel[p, h, h * HEAD_DIM + p] = sign
    return sel


def _route_tables(meta, counts, tm):
    n_tok = meta.shape[1]
    e = meta[0:2].astype(jnp.int32)
    rank = meta[2:4].astype(jnp.int32)
    counts = counts[0, :N_EXPERTS].astype(jnp.int32)
    n_pairs = 2 * n_tok
    padded = ((counts + tm - 1) // tm) * tm
    ends = jnp.cumsum(padded)
    experts = jnp.arange(N_EXPERTS, dtype=jnp.int32)
    first_row = jnp.sum(jnp.where(e[..., None] == experts, ends - padded, 0), axis=-1)
    pos = first_row + rank
    n_tiles = (n_pairs + N_EXPERTS * (tm - 1)) // tm
    rows = n_tiles * tm
    tok = jnp.broadcast_to(jnp.arange(n_tok, dtype=jnp.int32)[None, :], (2, n_tok))
    tok_of = jnp.zeros((rows,), jnp.int32).at[pos.reshape(-1)].set(
        tok.reshape(-1), unique_indices=True, mode="promise_in_bounds")
    tile_start = jnp.arange(n_tiles, dtype=jnp.int32) * tm
    tile_expert = jnp.minimum(
        jnp.sum((ends[None, :] <= tile_start[:, None]).astype(jnp.int32), axis=1), N_EXPERTS - 1)
    n_used = (ends[-1] // tm).astype(jnp.int32).reshape(1)
    nonempty = counts > 0
    expert_ord = jnp.cumsum(nonempty.astype(jnp.int32)) - 1
    later = nonempty[None, :] & (experts[None, :] > experts[:, None])
    next_expert = jnp.min(jnp.where(later, experts[None, :], N_EXPERTS), axis=1)
    next_expert = jnp.where(next_expert < N_EXPERTS, next_expert, -1).astype(jnp.int32)
    tile_ord = expert_ord[tile_expert].astype(jnp.int32)
    return pos, tok_of.reshape(n_tiles, 1, tm), tile_expert, tile_ord, next_expert, n_used


def _tile_pos(pos, tm):
    t = pos.shape[1]
    return jnp.concatenate([pos[0].reshape(t // tm, 1, tm), pos[1].reshape(t // tm, 1, tm)], axis=2)


def kernel(x_prompt, x_sample, cache_k, cache_v, cache_logf, state_conv, page_table, w_in, b_f,
           w_conv, w_out, ln1_g, ln1_b, w_rg, b_rg, w_re, b_re, w_gate, w_up, w_down, ln2_g, ln2_b):
    depth, d_model, _ = w_in.shape
    n_batch_p, seq, _ = x_prompt.shape
    db, dseq, _ = x_sample.shape
    n_heads = cache_k.shape[3]
    a = n_heads * HEAD_DIM
    c = w_conv.shape[-1]
    assert n_batch_p == 1 and dseq <= 8 and w_conv.shape[1] == CONV_K
    alpha = (2.0 * depth) ** 0.25
    qscale = HEAD_DIM ** -0.5 * LOG2E
    t_p, t_s = seq, db * dseq
    t_all = t_p + t_s

    sel_neg = _bias_selectors(n_heads, -1.0)
    psel_k = jnp.asarray(np.concatenate(
        [np.pad(sel_neg[p], ((0, LANES - n_heads), (0, 0))) for p in range(3)], axis=0), bf16)

    xp = x_prompt.reshape(t_p, d_model)
    xs = x_sample.reshape(t_s, d_model)
    outs = {k: [] for k in ("kp", "vp", "fp", "cp", "ks", "vs", "fs", "cs")}
    for l in range(depth):
        wl = jnp.swapaxes(w_in[l], 0, 1)
        wq, wk, wv = (wl[j * a:(j + 1) * a].astype(bf16) for j in range(3))
        o = 3 * a
        wf = jnp.pad(wl[o:o + n_heads], ((0, LANES - n_heads), (0, 0)))
        wfh = wf.astype(bf16)
        wf2 = jnp.concatenate([wfh, (wf - wfh.astype(f32)).astype(bf16)], axis=0)
        bfp = jnp.pad(b_f[l], (0, LANES - n_heads)).reshape(1, LANES)
        o += n_heads
        wb, wc, wh = (wl[o + j * c:o + (j + 1) * c].astype(bf16) for j in range(3))
        wo = w_out[l].astype(bf16)
        wr = jnp.pad(jnp.concatenate([w_rg[l], w_re[l]], axis=1),
                     ((0, 0), (0, LANES - N_GROUPS - N_EXPERTS)))
        wrh = wr.astype(bf16)
        wr2 = jnp.concatenate([wrh, (wr - wrh.astype(f32)).astype(bf16)], axis=1)
        br = jnp.pad(jnp.concatenate([b_rg[l], b_re[l]]), (0, LANES - N_GROUPS - N_EXPERTS)).reshape(1, LANES)
        g1, b1 = ln1_g[l].reshape(1, -1), ln1_b[l].reshape(1, -1)
        g2, b2 = ln2_g[l].reshape(1, -1), ln2_b[l].reshape(1, -1)

        q, ko, vo, kb, vb, lf, kx = _qkv_proj(xp, wq, wk, wv, wf2, wfh, bfp, psel_k,
                                              n_heads=n_heads, qscale=qscale, with_cum=True)
        conv, utail = _conv_proj(xp, wb, wc, wh, w_conv[l])
        attn = _prompt_attn(q, kb, kx, vb, n_heads=n_heads)
        outs["kp"].append(ko.reshape(1, t_p, n_heads, HEAD_DIM))
        outs["vp"].append(vo.reshape(1, t_p, n_heads, HEAD_DIM))
        outs["fp"].append(lf[:, :n_heads].reshape(1, t_p, n_heads))
        outs["cp"].append(utail[8 - (CONV_K - 1):].reshape(1, CONV_K - 1, c))

        qs, kos, vos, _, _, lfs = _qkv_proj(xs, wq, wk, wv, wf2, wfh, bfp, None,
                                            n_heads=n_heads, qscale=qscale, with_cum=False)
        st = state_conv[l]
        zero = jnp.zeros((db, dseq, c), f32)
        p1 = zero.at[:, 0].set(st[:, 1]).reshape(t_s, c)
        p2 = zero.at[:, 0].set(st[:, 0]).at[:, 1].set(st[:, 1]).reshape(t_s, c)
        conv_s, u_s = _conv_proj(xs, wb, wc, wh, w_conv[l], p1, p2, group=dseq)
        q8 = jnp.pad(qs.reshape(db, dseq, a), ((0, 0), (0, 8 - dseq), (0, 0))).astype(f32)
        page = cache_k.shape[2]
        lnew = jnp.pad(jnp.swapaxes(lfs[:, :n_heads].reshape(db, dseq, n_heads), 1, 2),
                       ((0, 0), (0, 0), (0, page - dseq)))
        attn_s = _sample_attn(page_table, q8, kos, vos, lnew,
                              cache_k[l].reshape(-1, HEAD_DIM), cache_v[l].reshape(-1, HEAD_DIM),
                              jnp.swapaxes(cache_logf[l], 1, 2), n_heads=n_heads, n_new=dseq)
        attn_s = attn_s[:, :dseq].reshape(t_s, a).astype(bf16)
        outs["ks"].append(kos.reshape(db, dseq, n_heads, HEAD_DIM))
        outs["vs"].append(vos.reshape(db, dseq, n_heads, HEAD_DIM))
        outs["fs"].append(lfs[:, :n_heads].reshape(db, dseq, n_heads))
        outs["cs"].append(u_s.reshape(db, dseq, c)[:, dseq - (CONV_K - 1):])
        x1, x1g, meta, counts, gw = _finish1(attn, conv, xp, attn_s, conv_s, xs, wo, g1, b1,
                                             wr2, wrh, br, alpha=alpha)

        pos, tok_of, tile_expert, tile_ord, next_expert, n_used = _route_tables(
            meta, counts, TM_MOE)
        ys = _moe(x1g.reshape(t_all, d_model // LANES, LANES), tok_of, tile_expert, tile_ord,
                  next_expert, n_used, w_gate[l], w_up[l], w_down[l])
        ys = ys.reshape(-1, d_model // LANES, LANES)
        tm2 = min(TM_PROJ, t_s)
        xp = _finish2(ys, _tile_pos(pos[:, :t_p], TM_PROJ), x1, gw, g2, b2, alpha=alpha, row0=0, t=t_p)
        xs = _finish2(ys, _tile_pos(pos[:, t_p:], tm2), x1, gw, g2, b2, alpha=alpha, row0=t_p, t=t_s)

    stack = lambda k: jnp.stack(outs[k])
    return (xp.reshape(x_prompt.shape), xs.reshape(x_sample.shape),
            stack("kp"), stack("vp"), stack("fp"), stack("cp"),
            stack("ks"), stack("vs"), stack("fs"), stack("cs"))
```

```python
import functools
import math

import numpy as np
import jax
import jax.numpy as jnp
from jax import lax
from jax.experimental import pallas as pl
from jax.experimental.pallas import tpu as pltpu

f32 = jnp.float32
bf16 = jnp.bfloat16

LANES = 128
HEAD_DIM = 128
LOG2E = 1.4426950408889634
LN_EPS = 1e-5
NEG = -1e30
CONV_K = 3
N_GROUPS = 4
EXPERTS_PER_GROUP = 8
N_EXPERTS = N_GROUPS * EXPERTS_PER_GROUP
VMEM_LIMIT = 56 * 1024 * 1024

TM_PROJ = 256
TQ = 512
N_SUB = 2
PAGE_UNROLL = 4
TM_MOE = 256
ROW_PITCH_PAD = 1


def _dot(a, b):
    return jnp.dot(a, b, preferred_element_type=f32)


def _dot_nt(a, b):
    return lax.dot_general(a, b, (((1,), (1,)), ((), ())), preferred_element_type=f32)


def _lane_tile(x, n):
    return jnp.concatenate([x] * n, axis=-1)


def _split3(x):
    hi = x.astype(bf16)
    r1 = x - hi.astype(f32)
    mid = r1.astype(bf16)
    lo = (r1 - mid.astype(f32)).astype(bf16)
    return hi, mid, lo


def _params(sem=("arbitrary",)):
    return pltpu.CompilerParams(dimension_semantics=sem, vmem_limit_bytes=VMEM_LIMIT)


def _resident(shape):
    nd = len(shape)
    return pl.BlockSpec(shape, lambda *_: (0,) * nd, pipeline_mode=pl.Buffered(1))


def _qkv_kernel(x_ref, wq_ref, wk_ref, wv_ref, wf2_ref, wfh_ref, bf_ref, psel_ref,
                q_ref, ko_ref, vo_ref, kb_ref, vb_ref, lf_ref, kx_ref, carry_ref,
                *, tm, n_heads, qscale):
    i = pl.program_id(0)
    x = x_ref[...]
    xb = x.astype(bf16)
    q_ref[...] = (_dot_nt(xb, wq_ref[...]) * qscale).astype(bf16)
    k = _dot_nt(xb, wk_ref[...])
    kb_ref[...] = k.astype(bf16)
    v = _dot_nt(xb, wv_ref[...])
    vb_ref[...] = v.astype(bf16)
    for h in range(n_heads):
        ko_ref[pl.ds(h, tm, stride=n_heads), :] = k[:, h * HEAD_DIM:(h + 1) * HEAD_DIM]
        vo_ref[pl.ds(h, tm, stride=n_heads), :] = v[:, h * HEAD_DIM:(h + 1) * HEAD_DIM]

    xl = (x - xb.astype(f32)).astype(bf16)
    zf2 = _dot_nt(xb, wf2_ref[...])
    zf = zf2[:, :LANES] + zf2[:, LANES:] + _dot_nt(xl, wfh_ref[...]) + bf_ref[...]
    lf = jnp.minimum(zf, 0.0) - jnp.log1p(jnp.exp(-jnp.abs(zf)))
    lf_ref[...] = lf

    if kx_ref is not None:
        @pl.when(i == 0)
        def _():
            carry_ref[...] = jnp.zeros(carry_ref.shape, f32)

        row = lax.broadcasted_iota(jnp.int32, (tm, tm), 0)
        col = lax.broadcasted_iota(jnp.int32, (tm, tm), 1)
        tri = jnp.where(row >= col, 1.0, 0.0).astype(bf16)
        c3 = _dot(tri, jnp.concatenate(_split3(lf), axis=-1))
        cum = (c3[:, :LANES] + c3[:, LANES:2 * LANES] + c3[:, 2 * LANES:]) + carry_ref[0:1, :]
        carry_ref[...] = jnp.broadcast_to(cum[tm - 1:tm, :], carry_ref.shape)
        parts = jnp.concatenate(_split3(cum * LOG2E), axis=-1)
        kx_ref[...] = _dot(parts, psel_ref[...]).astype(bf16)


def _qkv_proj(x, wq, wk, wv, wf2, wfh, bfp, psel, *, n_heads, qscale, with_cum):
    t, d = x.shape
    a = wq.shape[0]
    tm = min(TM_PROJ, t)
    grid = (t // tm,)
    row_blk = lambda w: pl.BlockSpec((tm, w), lambda i: (i, 0))
    in_specs = [row_blk(d), _resident(wq.shape), _resident(wk.shape), _resident(wv.shape),
                _resident(wf2.shape), _resident(wfh.shape), _resident(bfp.shape)]
    args = [x, wq, wk, wv, wf2, wfh, bfp]
    out_shape = [jax.ShapeDtypeStruct((t, a), bf16),
                 jax.ShapeDtypeStruct((t * n_heads, HEAD_DIM), f32),
                 jax.ShapeDtypeStruct((t * n_heads, HEAD_DIM), f32),
                 jax.ShapeDtypeStruct((t, a), bf16),
                 jax.ShapeDtypeStruct((t, a), bf16),
                 jax.ShapeDtypeStruct((t, LANES), f32)]
    hd_blk = pl.BlockSpec((tm * n_heads, HEAD_DIM), lambda i: (i, 0))
    out_specs = [row_blk(a), hd_blk, hd_blk, row_blk(a), row_blk(a), row_blk(LANES)]
    scratch = []
    if with_cum:
        in_specs.append(_resident(psel.shape))
        args.append(psel)
        out_shape.append(jax.ShapeDtypeStruct((t, a), bf16))
        out_specs.append(row_blk(a))
        scratch.append(pltpu.VMEM((8, LANES), f32))

        def body(*refs):
            _qkv_kernel(*refs, tm=tm, n_heads=n_heads, qscale=qscale)
    else:
        def body(x_ref, wq_ref, wk_ref, wv_ref, wf2_ref, wfh_ref, bf_ref, *outs):
            _qkv_kernel(x_ref, wq_ref, wk_ref, wv_ref, wf2_ref, wfh_ref, bf_ref, None,
                        *outs, None, None, tm=tm, n_heads=n_heads, qscale=qscale)

    return pl.pallas_call(
        body, grid=grid, in_specs=in_specs, out_specs=out_specs, out_shape=out_shape,
        scratch_shapes=scratch, compiler_params=_params(), name="qkv_proj")(*args)


def _conv_kernel(x_ref, wb_ref, wc_ref, wh_ref, wconv_ref, p1_ref, p2_ref,
                 conv_ref, u_ref, ubuf, *, tm, group):
    i = pl.program_id(0)
    xb = x_ref[...].astype(bf16)
    gate_b = _dot_nt(xb, wb_ref[...])
    u = _dot_nt(xb, wc_ref[...]) * _dot_nt(xb, wh_ref[...])
    w0 = wconv_ref[0:1, :]
    w1 = wconv_ref[1:2, :]
    w2 = wconv_ref[2:3, :]
    if group is None:
        @pl.when(i == 0)
        def _():
            ubuf[0:8, :] = jnp.zeros((8, ubuf.shape[1]), f32)

        ubuf[8:8 + tm, :] = u
        u1 = ubuf[pl.ds(7, tm), :]
        u2 = ubuf[pl.ds(6, tm), :]
        ubuf[0:8, :] = u[tm - 8:tm, :]
        u_ref[...] = u[tm - 8:tm, :]
    else:
        ubuf[0:8, :] = jnp.zeros((8, ubuf.shape[1]), f32)
        ubuf[8:8 + tm, :] = u
        t_in = lax.broadcasted_iota(jnp.int32, (tm, 1), 0) % group
        u1 = jnp.where(t_in >= 1, ubuf[pl.ds(7, tm), :], p1_ref[...])
        u2 = jnp.where(t_in >= 2, ubuf[pl.ds(6, tm), :], p2_ref[...])
        u_ref[...] = u
    y = (u2 * w0 + u1 * w1) + u * w2
    conv_ref[...] = (gate_b * y).astype(bf16)


def _conv_proj(x, wb, wc, wh, wconv, p1=None, p2=None, *, group=None):
    t, d = x.shape
    c = wb.shape[0]
    tm = min(TM_PROJ, t)
    grid = (t // tm,)
    row_blk = lambda w: pl.BlockSpec((tm, w), lambda i: (i, 0))
    in_specs = [row_blk(d), _resident(wb.shape), _resident(wc.shape), _resident(wh.shape),
                _resident(wconv.shape)]
    args = [x, wb, wc, wh, wconv]
    if group is None:
        u_shape = jax.ShapeDtypeStruct((8, c), f32)
        u_spec = pl.BlockSpec((8, c), lambda i: (0, 0))

        def body(x_ref, wb_ref, wc_ref, wh_ref, wconv_ref, conv_ref, u_ref, ubuf):
            _conv_kernel(x_ref, wb_ref, wc_ref, wh_ref, wconv_ref, None, None,
                         conv_ref, u_ref, ubuf, tm=tm, group=None)
    else:
        in_specs += [row_blk(c), row_blk(c)]
        args += [p1, p2]
        u_shape = jax.ShapeDtypeStruct((t, c), f32)
        u_spec = row_blk(c)

        def body(*refs):
            _conv_kernel(*refs, tm=tm, group=group)

    return pl.pallas_call(
        body, grid=grid, in_specs=in_specs,
        out_specs=[row_blk(c), u_spec],
        out_shape=[jax.ShapeDtypeStruct((t, c), bf16), u_shape],
        scratch_shapes=[pltpu.VMEM((tm + 8, c), f32)],
        compiler_params=_params(), name="conv_proj")(*args)


def _prompt_attn_kernel(q_ref, k_ref, kx_ref, v_ref, o_ref, *scratch, tq, n_sub):
    m_scr, acc_scr = scratch[:n_sub], scratch[n_sub:2 * n_sub]
    s_a, s_b = scratch[2 * n_sub:3 * n_sub], scratch[3 * n_sub:]
    i = pl.program_id(1)
    ts = tq // n_sub
    ones_q = jnp.ones((ts, LANES), bf16)
    qa = [jnp.concatenate([q_ref[u * ts:(u + 1) * ts, :], ones_q], axis=-1) for u in range(n_sub)]
    ones_v = jnp.ones((tq, LANES), bf16)
    for u in range(n_sub):
        m_scr[u][...] = jnp.full(m_scr[u].shape, NEG, f32)
        acc_scr[u][...] = jnp.zeros(acc_scr[u].shape, f32)

    def scores(j, s_buf):
        ks = pl.multiple_of(j * tq, tq)
        ka = jnp.concatenate([k_ref[pl.ds(ks, tq), :], kx_ref[pl.ds(ks, tq), :]], axis=-1)
        for u in range(n_sub):
            s_buf[u][...] = _dot_nt(qa[u], ka)

    def accumulate(j, s_buf, masked):
        ks = pl.multiple_of(j * tq, tq)
        va = jnp.concatenate([v_ref[pl.ds(ks, tq), :], ones_v], axis=-1)
        for u in range(n_sub):
            s = s_buf[u][...]
            if masked:
                row = lax.broadcasted_iota(jnp.int32, (ts, tq), 0) + u * ts
                col = lax.broadcasted_iota(jnp.int32, (ts, tq), 1)
                s = jnp.where(col <= row, s, NEG)
            m_prev = m_scr[u][...]
            m_new = jnp.maximum(m_prev, jnp.max(s, axis=1, keepdims=True))
            p = jnp.exp2(s - _lane_tile(m_new, tq // LANES))
            alpha = jnp.exp2(m_prev - m_new)
            acc_scr[u][...] = acc_scr[u][...] * _lane_tile(alpha, 2) + _dot(p.astype(bf16), va)
            m_scr[u][...] = m_new

    scores(0, s_a)

    def body(jj, c):
        j = 2 * jj
        scores(j + 1, s_b)
        accumulate(j, s_a, False)
        scores(j + 2, s_a)
        accumulate(j + 1, s_b, False)
        return c

    lax.fori_loop(0, i // 2, body, 0)

    @pl.when(i % 2 == 0)
    def _():
        accumulate(i, s_a, True)

    @pl.when(i % 2 == 1)
    def _():
        scores(i, s_b)
        accumulate(i - 1, s_a, False)
        accumulate(i, s_b, True)

    for u in range(n_sub):
        acc = acc_scr[u][...]
        o_ref[u * ts:(u + 1) * ts, :] = (acc[:, :LANES] / acc[:, LANES:]).astype(o_ref.dtype)


def _prompt_attn(q, k, kx, v, *, n_heads):
    s = q.shape[0]
    tq = min(TQ, s)
    grid = (n_heads, s // tq)
    q_blk = pl.BlockSpec((tq, HEAD_DIM), lambda h, i: (i, h))
    kv_blk = pl.BlockSpec((s, HEAD_DIM), lambda h, i: (0, h))
    return pl.pallas_call(
        functools.partial(_prompt_attn_kernel, tq=tq, n_sub=N_SUB),
        grid=grid, in_specs=[q_blk, kv_blk, kv_blk, kv_blk], out_specs=q_blk,
        out_shape=jax.ShapeDtypeStruct(q.shape, bf16),
        scratch_shapes=([pltpu.VMEM((tq // N_SUB, LANES), f32)] * N_SUB
                        + [pltpu.VMEM((tq // N_SUB, 2 * LANES), f32)] * N_SUB
                        + [pltpu.VMEM((tq // N_SUB, tq), f32)] * (2 * N_SUB)),
        compiler_params=_params(("arbitrary", "arbitrary")), name="prompt_attn")(q, k, kx, v)


def _sample_attn_kernel(pt_ref, q_ref, knew_hbm, vnew_hbm, lnew_hbm, ck_hbm, cv_hbm, cl_hbm,
                        o_ref, kbuf, vbuf, lbuf, bias_scr, sbuf, sem,
                        *, n_pages, n_batch, n_heads, n_new, page):
    b = pl.program_id(0)
    slot = b % 2
    npg = n_pages + 1
    rows = page * n_heads
    new_rows = n_new * n_heads

    def copies(bb, sl):
        cps = []
        for p in range(n_pages):
            pg = pt_ref[bb * n_pages + p]
            src = pl.ds(pl.multiple_of(pg * rows, rows), rows)
            dst = pl.ds(p * rows, rows)
            cps.append(pltpu.make_async_copy(ck_hbm.at[src], kbuf.at[sl, dst], sem.at[sl, 0]))
            cps.append(pltpu.make_async_copy(cv_hbm.at[src], vbuf.at[sl, dst], sem.at[sl, 1]))
            cps.append(pltpu.make_async_copy(cl_hbm.at[pg], lbuf.at[sl, p], sem.at[sl, 2]))
        src = pl.ds(pl.multiple_of(bb * new_rows, new_rows), new_rows)
        dst = pl.ds(n_pages * rows, new_rows)
        cps.append(pltpu.make_async_copy(knew_hbm.at[src], kbuf.at[sl, dst], sem.at[sl, 0]))
        cps.append(pltpu.make_async_copy(vnew_hbm.at[src], vbuf.at[sl, dst], sem.at[sl, 1]))
        cps.append(pltpu.make_async_copy(lnew_hbm.at[bb], lbuf.at[sl, n_pages], sem.at[sl, 2]))
        return cps

    @pl.when(b == 0)
    def _():
        for sl in range(2):
            kbuf[sl, pl.ds(n_pages * rows, rows), :] = jnp.zeros((rows, HEAD_DIM), f32)
            vbuf[sl, pl.ds(n_pages * rows, rows), :] = jnp.zeros((rows, HEAD_DIM), f32)
        for cp in copies(0, 0):
            cp.start()

    @pl.when(b + 1 < n_batch)
    def _():
        for cp in copies(b + 1, 1 - slot):
            cp.start()

    for cp in copies(b, slot):
        cp.wait()

    row = lax.broadcasted_iota(jnp.int32, (page, page), 0)
    col = lax.broadcasted_iota(jnp.int32, (page, page), 1)
    after = jnp.where(row > col, 1.0, 0.0).astype(bf16)
    lf_all = lbuf[slot].reshape(npg * n_heads, page)
    hi, mid, lo = _split3(lf_all)
    suf = (_dot(hi, after) + _dot(mid, after)) + _dot(lo, after)
    tot = jnp.sum(lf_all, axis=1, keepdims=True)
    carry = jnp.zeros((n_heads, 1), f32)
    for p in range(npg - 1, -1, -1):
        rs = slice(p * n_heads, (p + 1) * n_heads)
        bias_scr[p] = (suf[rs, :] + carry) * LOG2E
        carry = carry + tot[rs, :]

    qb = q_ref[0].astype(bf16)
    qh = [qb[:, h * HEAD_DIM:(h + 1) * HEAD_DIM] for h in range(n_heads)]
    t_row = lax.broadcasted_iota(jnp.int32, (8, page), 0)
    lane = lax.broadcasted_iota(jnp.int32, (8, page), 1)

    def head_rows(buf, p, h):
        return buf[slot, pl.ds(p * rows + h, page, stride=n_heads), :].astype(bf16)

    def score_page(p, mvec, masked):
        out = []
        for h in range(n_heads):
            s = _dot_nt(qh[h], head_rows(kbuf, p, h)) + bias_scr[p, pl.ds(h, 1), :]
            if masked:
                s = jnp.where(lane <= t_row, s, NEG)
            sbuf[p, h] = s
            out.append(jnp.maximum(mvec[h], s))
        return tuple(out)

    unroll = math.gcd(n_pages, PAGE_UNROLL)
    mvec = lax.fori_loop(0, n_pages, functools.partial(score_page, masked=False),
                         tuple(jnp.full((8, page), NEG, f32) for _ in range(n_heads)),
                         unroll=unroll)
    mvec = score_page(n_pages, mvec, True)
    m = [jnp.max(mv, axis=1, keepdims=True) for mv in mvec]

    def value_page(p, carry):
        lvec, acc = carry
        lout, aout = [], []
        for h in range(n_heads):
            pr = jnp.exp2(sbuf[p, h] - m[h])
            lout.append(lvec[h] + pr)
            aout.append(acc[h] + _dot(pr.astype(bf16), head_rows(vbuf, p, h)))
        return tuple(lout), tuple(aout)

    zeros = tuple(jnp.zeros((8, page), f32) for _ in range(n_heads))
    lvec, acc = lax.fori_loop(0, n_pages, value_page, (zeros, zeros), unroll=unroll)
    lvec, acc = value_page(n_pages, (lvec, acc))
    for h in range(n_heads):
        l = jnp.sum(lvec[h], axis=1, keepdims=True)
        o_ref[0, :, h * HEAD_DIM:(h + 1) * HEAD_DIM] = acc[h] / l


def _sample_attn(page_table, q8, knew, vnew, lnew, ck, cv, cl, *, n_heads, n_new):
    n_batch, n_pages = page_table.shape
    page = cl.shape[-1]
    assert page == HEAD_DIM == LANES
    a = q8.shape[-1]
    rows = (n_pages + 1) * page * n_heads
    any_spec = pl.BlockSpec(memory_space=pl.ANY)
    q_blk = pl.BlockSpec((1, 8, a), lambda b, pt: (b, 0, 0))
    grid_spec = pltpu.PrefetchScalarGridSpec(
        num_scalar_prefetch=1, grid=(n_batch,),
        in_specs=[q_blk, any_spec, any_spec, any_spec, any_spec, any_spec, any_spec],
        out_specs=q_blk,
        scratch_shapes=[
            pltpu.VMEM((2, rows, HEAD_DIM), f32),
            pltpu.VMEM((2, rows, HEAD_DIM), f32),
            pltpu.VMEM((2, n_pages + 1, n_heads, page), f32),
            pltpu.VMEM((n_pages + 1, n_heads, page), f32),
            pltpu.VMEM((n_pages + 1, n_heads, 8, page), f32),
            pltpu.SemaphoreType.DMA((2, 3)),
        ])
    return pl.pallas_call(
        functools.partial(_sample_attn_kernel, n_pages=n_pages, n_batch=n_batch,
                          n_heads=n_heads, n_new=n_new, page=page),
        grid_spec=grid_spec, out_shape=jax.ShapeDtypeStruct(q8.shape, f32),
        compiler_params=_params(), name="sample_attn")(
            page_table.reshape(-1), q8, knew, vnew, lnew, ck, cv, cl)


def _layer_norm(y, g, b):
    mu = jnp.mean(y, axis=-1, keepdims=True)
    yc = y - mu
    var = jnp.mean(yc * yc, axis=-1, keepdims=True)
    return yc * lax.rsqrt(var + LN_EPS) * g + b


def _finish1_kernel(ap_ref, cp_ref, xp_ref, as_ref, cs_ref, xs_ref,
                    woa_ref, woc_ref, g_ref, b_ref, wr2_ref, wrh_ref, br_ref,
                    x1_ref, x1g_ref, meta_ref, cnt_ref, gw_ref, cnt_scr, *, alpha, nb_prompt, tm):
    is_prompt = pl.program_id(0) < nb_prompt
    attn = jnp.where(is_prompt, ap_ref[...], as_ref[...])
    conv = jnp.where(is_prompt, cp_ref[...], cs_ref[...])
    x = jnp.where(is_prompt, xp_ref[...], xs_ref[...])
    o = _dot(attn, woa_ref[...]) + _dot(conv, woc_ref[...])
    x1 = _layer_norm(alpha * x + o, g_ref[...], b_ref[...])
    x1_ref[...] = x1
    n_chunks = x1.shape[1] // LANES
    for c in range(n_chunks):
        x1g_ref[pl.ds(c, tm, stride=n_chunks), :] = x1[:, c * LANES:(c + 1) * LANES]

    x1b = x1.astype(bf16)
    x1l = (x1 - x1b.astype(f32)).astype(bf16)
    r2 = _dot(x1b, wr2_ref[...])
    lr = r2[:, :LANES] + r2[:, LANES:] + _dot(x1l, wrh_ref[...]) + br_ref[...]
    lane = lax.broadcasted_iota(jnp.int32, lr.shape, 1)

    def first_max(mask):
        vals = jnp.where(mask, lr, -jnp.inf)
        mx = jnp.max(vals, axis=-1, keepdims=True)
        idx = jnp.min(jnp.where(mask & (vals == mx), lane, LANES), axis=-1, keepdims=True)
        return mx, idx

    is_group = lane < N_GROUPS
    gmax, gidx = first_max(is_group)
    p_group = 1.0 / jnp.sum(jnp.where(is_group, jnp.exp(lr - gmax), 0.0), axis=-1, keepdims=True)
    lo = N_GROUPS + gidx * EXPERTS_PER_GROUP
    in_group = (lane >= lo) & (lane < lo + EXPERTS_PER_GROUP)
    e1, i1 = first_max(in_group)
    e2, i2 = first_max(in_group & (lane != i1))
    r = jnp.exp(e2 - e1)
    w1 = p_group / (1.0 + r)
    w2 = p_group * r / (1.0 + r)
    gw_ref[...] = jnp.where(lane == 0, w1, jnp.where(lane == 1, w2, 0.0))

    @pl.when(pl.program_id(0) == 0)
    def _():
        cnt_scr[...] = jnp.zeros(cnt_scr.shape, f32)

    id1, id2 = i1 - N_GROUPS, i2 - N_GROUPS
    sel1 = jnp.where(lane == id1, 1.0, 0.0)
    sel2 = jnp.where(lane == id2, 1.0, 0.0)
    row = lax.broadcasted_iota(jnp.int32, (tm, tm), 0)
    col = lax.broadcasted_iota(jnp.int32, (tm, tm), 1)
    before = jnp.where(col < row, 1.0, 0.0).astype(bf16)
    seen = cnt_scr[0:1, :]
    tot1 = jnp.sum(sel1, axis=0, keepdims=True)
    rank1 = jnp.sum(sel1 * (seen + _dot(before, sel1.astype(bf16))), axis=-1, keepdims=True)
    rank2 = jnp.sum(sel2 * ((seen + tot1) + _dot(before, sel2.astype(bf16))), axis=-1, keepdims=True)
    seen = (seen + tot1) + jnp.sum(sel2, axis=0, keepdims=True)
    cnt_scr[...] = jnp.broadcast_to(seen, cnt_scr.shape)
    cnt_ref[...] = jnp.broadcast_to(seen, cnt_ref.shape)
    meta = jnp.where(lane == 0, id1.astype(f32), jnp.where(lane == 1, id2.astype(f32),
                     jnp.where(lane == 2, rank1, jnp.where(lane == 3, rank2, 0.0))))
    meta_ref[...] = meta.T[0:8, :]


def _finish1(attn_p, conv_p, x_p, attn_s, conv_s, x_s, wo, g, b, wr2, wrh, br, *, alpha):
    (t_p, d), t_s = x_p.shape, x_s.shape[0]
    a = attn_p.shape[1]
    assert wo.shape[0] == 2 * a
    tm = min(TM_PROJ, t_s)
    nb_p, nb_s = t_p // tm, t_s // tm
    t_all = t_p + t_s
    p_blk = lambda w: pl.BlockSpec((tm, w), lambda i: (jnp.minimum(i, nb_p - 1), 0))
    s_blk = lambda w: pl.BlockSpec((tm, w), lambda i: (jnp.maximum(i - nb_p, 0), 0))
    out_blk = lambda w: pl.BlockSpec((tm, w), lambda i: (i, 0))
    in_specs = [p_blk(a), p_blk(a), p_blk(d), s_blk(a), s_blk(a), s_blk(d),
                pl.BlockSpec((a, d), lambda i: (0, 0), pipeline_mode=pl.Buffered(1)),
                pl.BlockSpec((a, d), lambda i: (1, 0), pipeline_mode=pl.Buffered(1)),
                _resident(g.shape), _resident(b.shape), _resident(wr2.shape),
                _resident(wrh.shape), _resident(br.shape)]
    n_chunks = d // LANES
    return pl.pallas_call(
        functools.partial(_finish1_kernel, alpha=alpha, nb_prompt=nb_p, tm=tm),
        grid=(nb_p + nb_s,), in_specs=in_specs,
        out_specs=[out_blk(d), pl.BlockSpec((tm * n_chunks, LANES), lambda i: (i, 0)),
                   pl.BlockSpec((8, tm), lambda i: (0, i)),
                   pl.BlockSpec((8, LANES), lambda i: (0, 0)), out_blk(LANES)],
        out_shape=[jax.ShapeDtypeStruct((t_all, d), f32),
                   jax.ShapeDtypeStruct((t_all * n_chunks, LANES), f32),
                   jax.ShapeDtypeStruct((8, t_all), f32),
                   jax.ShapeDtypeStruct((8, LANES), f32),
                   jax.ShapeDtypeStruct((t_all, LANES), f32)],
        scratch_shapes=[pltpu.VMEM((8, LANES), f32)],
        compiler_params=_params(), name="finish1")(
            attn_p, conv_p, x_p, attn_s, conv_s, x_s, wo, wo, g, b, wr2, wrh, br)


def _dispatch_kernel(lt_ref, nu_ref, pos_ref, x_ref, xs_hbm, zbuf, sem, zsem, *, tm, tm_moe, n_tiles):
    s = pl.program_id(0)

    @pl.when(s == 0)
    def _():
        zbuf[...] = jnp.zeros(zbuf.shape, f32)

        def zero_tile(row0):
            return pltpu.make_async_copy(zbuf, xs_hbm.at[pl.ds(row0, tm_moe)], zsem)

        def each_zero_tile(act):
            for e in range(N_EXPERTS):
                @pl.when(lt_ref[e] >= 0)
                def _():
                    act(zero_tile(lt_ref[e]))

            def unused(t, c):
                act(zero_tile(t * tm_moe))
                return c

            lax.fori_loop(nu_ref[0], n_tiles, unused, 0)

        each_zero_tile(lambda cp: cp.start())
        each_zero_tile(lambda cp: cp.wait())

    def issue(r, c):
        pltpu.make_async_copy(x_ref.at[r], xs_hbm.at[pos_ref[0, 0, r]], sem).start()
        pltpu.make_async_copy(x_ref.at[r], xs_hbm.at[pos_ref[0, 0, tm + r]], sem).start()
        return c

    lax.fori_loop(0, tm, issue, 0, unroll=8)
    done = xs_hbm.at[pl.ds(0, 2 * tm)]
    pltpu.make_async_copy(done, done, sem).wait()


def _dispatch(x1g, pos, last_tile_row, n_used, *, n_tiles, tm_moe):
    t, n_chunks, _ = x1g.shape
    n_steps, _, two_tm = pos.shape
    tm = two_tm // 2
    grid_spec = pltpu.PrefetchScalarGridSpec(
        num_scalar_prefetch=2, grid=(n_steps,),
        in_specs=[pl.BlockSpec((1, 1, two_tm), lambda s, *_: (s, 0, 0), memory_space=pltpu.SMEM),
                  pl.BlockSpec((tm, n_chunks, LANES), lambda s, *_: (s, 0, 0))],
        out_specs=pl.BlockSpec(memory_space=pl.ANY),
        scratch_shapes=[pltpu.VMEM((tm_moe, n_chunks, LANES), f32),
                        pltpu.SemaphoreType.DMA(()), pltpu.SemaphoreType.DMA(())])
    return pl.pallas_call(
        functools.partial(_dispatch_kernel, tm=tm, tm_moe=tm_moe, n_tiles=n_tiles),
        grid_spec=grid_spec,
        out_shape=jax.ShapeDtypeStruct((n_tiles * tm_moe, n_chunks, LANES), f32),
        compiler_params=_params(), name="dispatch")(last_tile_row, n_used, pos, x1g)


def _moe_kernel(te_ref, ord_ref, nxt_ref, nu_ref, x_ref, wg_hbm, wu_hbm, wd_hbm, y_ref,
                wgf, wuf, wdf, wgb, wub, wdb, wsem, *, tm, n_chunks):
    t = pl.program_id(0)
    n_used = nu_ref[0]

    def weight_copies(e, wslot):
        return (pltpu.make_async_copy(wg_hbm.at[e], wgf.at[wslot], wsem.at[wslot, 0]),
                pltpu.make_async_copy(wu_hbm.at[e], wuf.at[wslot], wsem.at[wslot, 1]),
                pltpu.make_async_copy(wd_hbm.at[e], wdf.at[wslot], wsem.at[wslot, 2]))

    @pl.when(t == 0)
    def _():
        for cp in weight_copies(te_ref[0], 0):
            cp.start(priority=1)

    @pl.when(t < n_used)
    def _():
        e = te_ref[t]
        new_expert = (t == 0) | (e != te_ref[jnp.maximum(t - 1, 0)])

        @pl.when(new_expert)
        def _():
            wslot = ord_ref[t] % 2
            for cp in weight_copies(e, wslot):
                cp.wait()
            nxt = nxt_ref[e]

            @pl.when(nxt >= 0)
            def _():
                for cp in weight_copies(nxt, 1 - wslot):
                    cp.start(priority=1)

            wgb[...] = wgf[wslot].astype(bf16)
            wub[...] = wuf[wslot].astype(bf16)
            wdb[...] = wdf[wslot].astype(bf16)

        xb = jnp.concatenate([x_ref[pl.ds(c, tm, stride=n_chunks), :] for c in range(n_chunks)],
                             axis=-1).astype(bf16)
        hg = _dot(xb, wgb[...])
        hu = _dot(xb, wub[...])
        act = hg * (1.0 / (1.0 + jnp.exp(-hg))) * hu
        y = _dot(act.astype(bf16), wdb[...])
        for c in range(n_chunks):
            y_ref[pl.ds(c, tm, stride=n_chunks), :] = y[:, c * LANES:(c + 1) * LANES]

    @pl.when(t >= n_used)
    def _():
        y_ref[...] = jnp.zeros(y_ref.shape, f32)


def _moe(xs, tile_expert, tile_ord, next_expert, n_used, wg, wu, wd, *, tm):
    _, d, f = wg.shape
    n_chunks = d // LANES
    n_tiles = xs.shape[0] // (tm * n_chunks)
    any_spec = pl.BlockSpec(memory_space=pl.ANY)

    def x_idx(t, te, od, nx, nu):
        return (jnp.minimum(t, jnp.maximum(nu[0] - 1, 0)), 0)

    grid_spec = pltpu.PrefetchScalarGridSpec(
        num_scalar_prefetch=4, grid=(n_tiles,),
        in_specs=[pl.BlockSpec((tm * n_chunks, LANES), x_idx), any_spec, any_spec, any_spec],
        out_specs=pl.BlockSpec((tm * n_chunks, LANES), lambda t, *_: (t, 0)),
        scratch_shapes=[pltpu.VMEM((2, d, f), f32), pltpu.VMEM((2, d, f), f32),
                        pltpu.VMEM((2, f, d), f32),
                        pltpu.VMEM((d, f), bf16), pltpu.VMEM((d, f), bf16), pltpu.VMEM((f, d), bf16),
                        pltpu.SemaphoreType.DMA((2, 3))])
    return pl.pallas_call(
        functools.partial(_moe_kernel, tm=tm, n_chunks=n_chunks),
        grid_spec=grid_spec, out_shape=jax.ShapeDtypeStruct(xs.shape, f32),
        compiler_params=_params(), name="moe")(
            tile_expert, tile_ord, next_expert, n_used, xs, wg, wu, wd)


def _finish2_kernel(idx_ref, ys_hbm, x1_ref, gw_ref, g_ref, b_ref, o_ref, gbuf, sem,
                    *, tm, n_steps, alpha):
    s = pl.program_id(0)
    n_chunks = ys_hbm.shape[1]
    pitch = n_chunks + ROW_PITCH_PAD

    @pl.when(s < n_steps)
    def _():
        slot = s % 2

        def issue(r, c):
            row = idx_ref[0, 0, r]
            pltpu.make_async_copy(ys_hbm.at[row], gbuf.at[slot, pl.ds(r * pitch, n_chunks)],
                                  sem.at[slot]).start()
            return c

        lax.fori_loop(0, 2 * tm, issue, 0, unroll=8)

    @pl.when(s >= 1)
    def _():
        slot = (s - 1) % 2
        landed = gbuf.at[slot, pl.ds(0, 2 * tm * n_chunks)]
        pltpu.make_async_copy(landed, landed, sem.at[slot]).wait()

        def rows(first):
            return jnp.concatenate(
                [gbuf[slot, pl.ds(first * pitch + c, tm, stride=pitch), :] for c in range(n_chunks)],
                axis=-1)

        gw = gw_ref[...]
        f = gw[:, 0:1] * rows(0) + gw[:, 1:2] * rows(tm)
        o_ref[...] = _layer_norm(alpha * x1_ref[...] + f, g_ref[...], b_ref[...])


def _finish2(ys, pos, x1, gw, g, b, *, alpha, row0, t):
    n_steps, _, two_tm = pos.shape
    tm = two_tm // 2
    d = x1.shape[1]
    n_chunks = d // LANES
    blk0 = row0 // tm
    last = n_steps - 1
    in_blk = lambda w: pl.BlockSpec((tm, w), lambda s: (jnp.maximum(s - 1, 0) + blk0, 0))
    return pl.pallas_call(
        functools.partial(_finish2_kernel, tm=tm, n_steps=n_steps, alpha=alpha),
        grid=(n_steps + 1,),
        in_specs=[pl.BlockSpec((1, 1, two_tm), lambda s: (jnp.minimum(s, last), 0, 0),
                               memory_space=pltpu.SMEM),
                  pl.BlockSpec(memory_space=pl.ANY),
                  in_blk(d), in_blk(LANES), _resident(g.shape), _resident(b.shape)],
        out_specs=pl.BlockSpec((tm, d), lambda s: (jnp.maximum(s - 1, 0), 0)),
        out_shape=jax.ShapeDtypeStruct((t, d), f32),
        scratch_shapes=[pltpu.VMEM((2, two_tm * (n_chunks + ROW_PITCH_PAD), LANES), f32),
                        pltpu.SemaphoreType.DMA((2,))],
        compiler_params=_params(), name="finish2")(pos, ys, x1, gw, g, b)


def _bias_selectors(n_heads, sign):
    sel = np.zeros((3, n_heads, n_heads * HEAD_DIM), np.float32)
    for p in range(3):
        for h in range(n_heads):
            sel[p, h, h * HEAD_DIM + p] = sign
    return sel


def _route_tables(meta, counts, tm):
    n_tok = meta.shape[1]
    e = meta[0:2].astype(jnp.int32)
    rank = meta[2:4].astype(jnp.int32)
    counts = counts[0, :N_EXPERTS].astype(jnp.int32)
    n_pairs = 2 * n_tok
    padded = ((counts + tm - 1) // tm) * tm
    ends = jnp.cumsum(padded)
    experts = jnp.arange(N_EXPERTS, dtype=jnp.int32)
    first_row = jnp.sum(jnp.where(e[..., None] == experts, ends - padded, 0), axis=-1)
    pos = first_row + rank
    n_tiles = (n_pairs + N_EXPERTS * (tm - 1)) // tm
    last_tile_row = jnp.where(counts > 0, ends - tm, -1).astype(jnp.int32)
    tile_start = jnp.arange(n_tiles, dtype=jnp.int32) * tm
    tile_expert = jnp.minimum(
        jnp.sum((ends[None, :] <= tile_start[:, None]).astype(jnp.int32), axis=1), N_EXPERTS - 1)
    n_used = (ends[-1] // tm).astype(jnp.int32).reshape(1)
    nonempty = counts > 0
    expert_ord = jnp.cumsum(nonempty.astype(jnp.int32)) - 1
    later = nonempty[None, :] & (experts[None, :] > experts[:, None])
    next_expert = jnp.min(jnp.where(later, experts[None, :], N_EXPERTS), axis=1)
    next_expert = jnp.where(next_expert < N_EXPERTS, next_expert, -1).astype(jnp.int32)
    tile_ord = expert_ord[tile_expert].astype(jnp.int32)
    return pos, last_tile_row, n_tiles, tile_expert, tile_ord, next_expert, n_used


def _tile_pos(pos, tm):
    t = pos.shape[1]
    return jnp.concatenate([pos[0].reshape(t // tm, 1, tm), pos[1].reshape(t // tm, 1, tm)], axis=2)


def kernel(x_prompt, x_sample, cache_k, cache_v, cache_logf, state_conv, page_table, w_in, b_f,
           w_conv, w_out, ln1_g, ln1_b, w_rg, b_rg, w_re, b_re, w_gate, w_up, w_down, ln2_g, ln2_b):
    depth, d_model, _ = w_in.shape
    n_batch_p, seq, _ = x_prompt.shape
    db, dseq, _ = x_sample.shape
    n_heads = cache_k.shape[3]
    a = n_heads * HEAD_DIM
    c = w_conv.shape[-1]
    assert n_batch_p == 1 and dseq <= 8 and w_conv.shape[1] == CONV_K
    alpha = (2.0 * depth) ** 0.25
    qscale = HEAD_DIM ** -0.5 * LOG2E
    t_p, t_s = seq, db * dseq
    t_all = t_p + t_s

    sel_neg = _bias_selectors(n_heads, -1.0)
    psel_k = jnp.asarray(np.concatenate(
        [np.pad(sel_neg[p], ((0, LANES - n_heads), (0, 0))) for p in range(3)], axis=0), bf16)

    xp = x_prompt.reshape(t_p, d_model)
    xs = x_sample.reshape(t_s, d_model)
    outs = {k: [] for k in ("kp", "vp", "fp", "cp", "ks", "vs", "fs", "cs")}
    for l in range(depth):
        wl = jnp.swapaxes(w_in[l], 0, 1)
        wq, wk, wv = (wl[j * a:(j + 1) * a].astype(bf16) for j in range(3))
        o = 3 * a
        wf = jnp.pad(wl[o:o + n_heads], ((0, LANES - n_heads), (0, 0)))
        wfh = wf.astype(bf16)
        wf2 = jnp.concatenate([wfh, (wf - wfh.astype(f32)).astype(bf16)], axis=0)
        bfp = jnp.pad(b_f[l], (0, LANES - n_heads)).reshape(1, LANES)
        o += n_heads
        wb, wc, wh = (wl[o + j * c:o + (j + 1) * c].astype(bf16) for j in range(3))
        wo = w_out[l].astype(bf16)
        wr = jnp.pad(jnp.concatenate([w_rg[l], w_re[l]], axis=1),
                     ((0, 0), (0, LANES - N_GROUPS - N_EXPERTS)))
        wrh = wr.astype(bf16)
        wr2 = jnp.concatenate([wrh, (wr - wrh.astype(f32)).astype(bf16)], axis=1)
        br = jnp.pad(jnp.concatenate([b_rg[l], b_re[l]]), (0, LANES - N_GROUPS - N_EXPERTS)).reshape(1, LANES)
        g1, b1 = ln1_g[l].reshape(1, -1), ln1_b[l].reshape(1, -1)
        g2, b2 = ln2_g[l].reshape(1, -1), ln2_b[l].reshape(1, -1)

        q, ko, vo, kb, vb, lf, kx = _qkv_proj(xp, wq, wk, wv, wf2, wfh, bfp, psel_k,
                                              n_heads=n_heads, qscale=qscale, with_cum=True)
        conv, utail = _conv_proj(xp, wb, wc, wh, w_conv[l])
        attn = _prompt_attn(q, kb, kx, vb, n_heads=n_heads)
        outs["kp"].append(ko.reshape(1, t_p, n_heads, HEAD_DIM))
        outs["vp"].append(vo.reshape(1, t_p, n_heads, HEAD_DIM))
        outs["fp"].append(lf[:, :n_heads].reshape(1, t_p, n_heads))
        outs["cp"].append(utail[8 - (CONV_K - 1):].reshape(1, CONV_K - 1, c))

        qs, kos, vos, _, _, lfs = _qkv_proj(xs, wq, wk, wv, wf2, wfh, bfp, None,
                                            n_heads=n_heads, qscale=qscale, with_cum=False)
        st = state_conv[l]
        zero = jnp.zeros((db, dseq, c), f32)
        p1 = zero.at[:, 0].set(st[:, 1]).reshape(t_s, c)
        p2 = zero.at[:, 0].set(st[:, 0]).at[:, 1].set(st[:, 1]).reshape(t_s, c)
        conv_s, u_s = _conv_proj(xs, wb, wc, wh, w_conv[l], p1, p2, group=dseq)
        q8 = jnp.pad(qs.reshape(db, dseq, a), ((0, 0), (0, 8 - dseq), (0, 0))).astype(f32)
        page = cache_k.shape[2]
        lnew = jnp.pad(jnp.swapaxes(lfs[:, :n_heads].reshape(db, dseq, n_heads), 1, 2),
                       ((0, 0), (0, 0), (0, page - dseq)))
        attn_s = _sample_attn(page_table, q8, kos, vos, lnew,
                              cache_k[l].reshape(-1, HEAD_DIM), cache_v[l].reshape(-1, HEAD_DIM),
                              jnp.swapaxes(cache_logf[l], 1, 2), n_heads=n_heads, n_new=dseq)
        attn_s = attn_s[:, :dseq].reshape(t_s, a).astype(bf16)
        outs["ks"].append(kos.reshape(db, dseq, n_heads, HEAD_DIM))
        outs["vs"].append(vos.reshape(db, dseq, n_heads, HEAD_DIM))
        outs["fs"].append(lfs[:, :n_heads].reshape(db, dseq, n_heads))
        outs["cs"].append(u_s.reshape(db, dseq, c)[:, dseq - (CONV_K - 1):])
        x1, x1g, meta, counts, gw = _finish1(attn, conv, xp, attn_s, conv_s, xs, wo, g1, b1,
                                             wr2, wrh, br, alpha=alpha)

        pos, last_tile_row, n_tiles, tile_expert, tile_ord, next_expert, n_used = _route_tables(
            meta, counts, TM_MOE)
        n_chunks = d_model // LANES
        tm2 = min(TM_PROJ, t_s)
        xsort = _dispatch(x1g.reshape(t_all, n_chunks, LANES), _tile_pos(pos, tm2), last_tile_row,
                          n_used, n_tiles=n_tiles, tm_moe=TM_MOE)
        ys = _moe(xsort.reshape(-1, LANES), tile_expert, tile_ord, next_expert, n_used,
                  w_gate[l], w_up[l], w_down[l], tm=TM_MOE)
        ys = ys.reshape(-1, n_chunks, LANES)
        xp = _finish2(ys, _tile_pos(pos[:, :t_p], TM_PROJ), x1, gw, g2, b2, alpha=alpha, row0=0, t=t_p)
        xs = _finish2(ys, _tile_pos(pos[:, t_p:], tm2), x1, gw, g2, b2, alpha=alpha, row0=t_p, t=t_s)

    stack = lambda k: jnp.stack(outs[k])
    return (xp.reshape(x_prompt.shape), xs.reshape(x_sample.shape),
            stack("kp"), stack("vp"), stack("fp"), stack("cp"),
            stack("ks"), stack("vs"), stack("fs"), stack("cs"))
```

```python
import functools
import math

import numpy as np
import jax
import jax.numpy as jnp
from jax import lax
from jax.experimental import pallas as pl
from jax.experimental.pallas import tpu as pltpu

f32 = jnp.float32
bf16 = jnp.bfloat16

LANES = 128
HEAD_DIM = 128
LOG2E = 1.4426950408889634
LN_EPS = 1e-5
NEG = -1e30
CONV_K = 3
N_GROUPS = 4
EXPERTS_PER_GROUP = 8
N_EXPERTS = N_GROUPS * EXPERTS_PER_GROUP
VMEM_LIMIT = 56 * 1024 * 1024

TM_IN = 512
TM_PROJ = 256
TQ = 512
N_SUB = 2
LOOP_BLOCKS = 4
PAGE_UNROLL = 4
TM_MOE = 256
ROW_PITCH_PAD = 1


def _dot(a, b):
    return jnp.dot(a, b, preferred_element_type=f32)


def _dot_nt(a, b):
    return lax.dot_general(a, b, (((1,), (1,)), ((), ())), preferred_element_type=f32)


def _lane_tile(x, n):
    return jnp.concatenate([x] * n, axis=-1)


def _split3(x):
    hi = x.astype(bf16)
    r1 = x - hi.astype(f32)
    mid = r1.astype(bf16)
    lo = (r1 - mid.astype(f32)).astype(bf16)
    return hi, mid, lo


def _params(sem=("arbitrary",)):
    return pltpu.CompilerParams(dimension_semantics=sem, vmem_limit_bytes=VMEM_LIMIT)


def _resident(shape):
    nd = len(shape)
    return pl.BlockSpec(shape, lambda *_: (0,) * nd, pipeline_mode=pl.Buffered(1))


def _qkv_kernel(x_ref, wq_ref, wk_ref, wv_ref, wf2_ref, wfh_ref, bf_ref, psel_ref,
                q_ref, ko_ref, vo_ref, kb_ref, vb_ref, lf_ref, kx_ref, carry_ref,
                *, tm, n_heads, qscale):
    i = pl.program_id(0)
    x = x_ref[...]
    xb = x.astype(bf16)
    q_ref[...] = (_dot_nt(xb, wq_ref[...]) * qscale).astype(bf16)
    k = _dot_nt(xb, wk_ref[...])
    kb_ref[...] = k.astype(bf16)
    v = _dot_nt(xb, wv_ref[...])
    vb_ref[...] = v.astype(bf16)
    for h in range(n_heads):
        ko_ref[pl.ds(h, tm, stride=n_heads), :] = k[:, h * HEAD_DIM:(h + 1) * HEAD_DIM]
        vo_ref[pl.ds(h, tm, stride=n_heads), :] = v[:, h * HEAD_DIM:(h + 1) * HEAD_DIM]

    xl = (x - xb.astype(f32)).astype(bf16)
    zf2 = _dot_nt(xb, wf2_ref[...])
    zf = zf2[:, :LANES] + zf2[:, LANES:] + _dot_nt(xl, wfh_ref[...]) + bf_ref[...]
    lf = jnp.minimum(zf, 0.0) - jnp.log1p(jnp.exp(-jnp.abs(zf)))
    lf_ref[...] = lf

    if kx_ref is not None:
        @pl.when(i == 0)
        def _():
            carry_ref[...] = jnp.zeros(carry_ref.shape, f32)

        row = lax.broadcasted_iota(jnp.int32, (tm, tm), 0)
        col = lax.broadcasted_iota(jnp.int32, (tm, tm), 1)
        tri = jnp.where(row >= col, 1.0, 0.0).astype(bf16)
        c3 = _dot(tri, jnp.concatenate(_split3(lf), axis=-1))
        cum = (c3[:, :LANES] + c3[:, LANES:2 * LANES] + c3[:, 2 * LANES:]) + carry_ref[0:1, :]
        carry_ref[...] = jnp.broadcast_to(cum[tm - 1:tm, :], carry_ref.shape)
        parts = jnp.concatenate(_split3(cum * LOG2E), axis=-1)
        kx_ref[...] = _dot(parts, psel_ref[...]).astype(bf16)


def _qkv_proj(x, wq, wk, wv, wf2, wfh, bfp, psel, *, n_heads, qscale, with_cum):
    t, d = x.shape
    a = wq.shape[0]
    tm = min(TM_IN, t)
    grid = (t // tm,)
    row_blk = lambda w: pl.BlockSpec((tm, w), lambda i: (i, 0))
    in_specs = [row_blk(d), _resident(wq.shape), _resident(wk.shape), _resident(wv.shape),
                _resident(wf2.shape), _resident(wfh.shape), _resident(bfp.shape)]
    args = [x, wq, wk, wv, wf2, wfh, bfp]
    out_shape = [jax.ShapeDtypeStruct((t, a), bf16),
                 jax.ShapeDtypeStruct((t * n_heads, HEAD_DIM), f32),
                 jax.ShapeDtypeStruct((t * n_heads, HEAD_DIM), f32),
                 jax.ShapeDtypeStruct((t, a), bf16),
                 jax.ShapeDtypeStruct((t, a), bf16),
                 jax.ShapeDtypeStruct((t, LANES), f32)]
    hd_blk = pl.BlockSpec((tm * n_heads, HEAD_DIM), lambda i: (i, 0))
    out_specs = [row_blk(a), hd_blk, hd_blk, row_blk(a), row_blk(a), row_blk(LANES)]
    scratch = []
    if with_cum:
        in_specs.append(_resident(psel.shape))
        args.append(psel)
        out_shape.append(jax.ShapeDtypeStruct((t, a), bf16))
        out_specs.append(row_blk(a))
        scratch.append(pltpu.VMEM((8, LANES), f32))

        def body(*refs):
            _qkv_kernel(*refs, tm=tm, n_heads=n_heads, qscale=qscale)
    else:
        def body(x_ref, wq_ref, wk_ref, wv_ref, wf2_ref, wfh_ref, bf_ref, *outs):
            _qkv_kernel(x_ref, wq_ref, wk_ref, wv_ref, wf2_ref, wfh_ref, bf_ref, None,
                        *outs, None, None, tm=tm, n_heads=n_heads, qscale=qscale)

    return pl.pallas_call(
        body, grid=grid, in_specs=in_specs, out_specs=out_specs, out_shape=out_shape,
        scratch_shapes=scratch, compiler_params=_params(), name="qkv_proj")(*args)


def _conv_kernel(x_ref, wb_ref, wc_ref, wh_ref, wconv_ref, p1_ref, p2_ref,
                 conv_ref, u_ref, ubuf, *, tm, group):
    i = pl.program_id(0)
    xb = x_ref[...].astype(bf16)
    gate_b = _dot_nt(xb, wb_ref[...])
    u = _dot_nt(xb, wc_ref[...]) * _dot_nt(xb, wh_ref[...])
    w0 = wconv_ref[0:1, :]
    w1 = wconv_ref[1:2, :]
    w2 = wconv_ref[2:3, :]
    if group is None:
        @pl.when(i == 0)
        def _():
            ubuf[0:8, :] = jnp.zeros((8, ubuf.shape[1]), f32)

        ubuf[8:8 + tm, :] = u
        u1 = ubuf[pl.ds(7, tm), :]
        u2 = ubuf[pl.ds(6, tm), :]
        ubuf[0:8, :] = u[tm - 8:tm, :]
        u_ref[...] = u[tm - 8:tm, :]
    else:
        ubuf[0:8, :] = jnp.zeros((8, ubuf.shape[1]), f32)
        ubuf[8:8 + tm, :] = u
        t_in = lax.broadcasted_iota(jnp.int32, (tm, 1), 0) % group
        u1 = jnp.where(t_in >= 1, ubuf[pl.ds(7, tm), :], p1_ref[...])
        u2 = jnp.where(t_in >= 2, ubuf[pl.ds(6, tm), :], p2_ref[...])
        u_ref[...] = u
    y = (u2 * w0 + u1 * w1) + u * w2
    conv_ref[...] = (gate_b * y).astype(bf16)


def _conv_proj(x, wb, wc, wh, wconv, p1=None, p2=None, *, group=None):
    t, d = x.shape
    c = wb.shape[0]
    tm = min(TM_IN, t)
    grid = (t // tm,)
    row_blk = lambda w: pl.BlockSpec((tm, w), lambda i: (i, 0))
    in_specs = [row_blk(d), _resident(wb.shape), _resident(wc.shape), _resident(wh.shape),
                _resident(wconv.shape)]
    args = [x, wb, wc, wh, wconv]
    if group is None:
        u_shape = jax.ShapeDtypeStruct((8, c), f32)
        u_spec = pl.BlockSpec((8, c), lambda i: (0, 0))

        def body(x_ref, wb_ref, wc_ref, wh_ref, wconv_ref, conv_ref, u_ref, ubuf):
            _conv_kernel(x_ref, wb_ref, wc_ref, wh_ref, wconv_ref, None, None,
                         conv_ref, u_ref, ubuf, tm=tm, group=None)
    else:
        in_specs += [row_blk(c), row_blk(c)]
        args += [p1, p2]
        u_shape = jax.ShapeDtypeStruct((t, c), f32)
        u_spec = row_blk(c)

        def body(*refs):
            _conv_kernel(*refs, tm=tm, group=group)

    return pl.pallas_call(
        body, grid=grid, in_specs=in_specs,
        out_specs=[row_blk(c), u_spec],
        out_shape=[jax.ShapeDtypeStruct((t, c), bf16), u_shape],
        scratch_shapes=[pltpu.VMEM((tm + 8, c), f32)],
        compiler_params=_params(), name="conv_proj")(*args)


def _prompt_attn_kernel(q_ref, k_ref, kx_ref, v_ref, o_ref, *scratch, tq, n_sub):
    m_scr, acc_scr = scratch[:n_sub], scratch[n_sub:2 * n_sub]
    s_a, s_b = scratch[2 * n_sub:3 * n_sub], scratch[3 * n_sub:]
    i = pl.program_id(1)
    ts = tq // n_sub
    ones_q = jnp.ones((ts, LANES), bf16)
    qa = [jnp.concatenate([q_ref[u * ts:(u + 1) * ts, :], ones_q], axis=-1) for u in range(n_sub)]
    ones_v = jnp.ones((tq, LANES), bf16)
    for u in range(n_sub):
        m_scr[u][...] = jnp.full(m_scr[u].shape, NEG, f32)
        acc_scr[u][...] = jnp.zeros(acc_scr[u].shape, f32)

    def scores(j, s_buf):
        ks = pl.multiple_of(j * tq, tq)
        ka = jnp.concatenate([k_ref[pl.ds(ks, tq), :], kx_ref[pl.ds(ks, tq), :]], axis=-1)
        for u in range(n_sub):
            s_buf[u][...] = _dot_nt(qa[u], ka)

    def accumulate(j, s_buf, masked):
        ks = pl.multiple_of(j * tq, tq)
        va = jnp.concatenate([v_ref[pl.ds(ks, tq), :], ones_v], axis=-1)
        for u in range(n_sub):
            s = s_buf[u][...]
            if masked:
                row = lax.broadcasted_iota(jnp.int32, (ts, tq), 0) + u * ts
                col = lax.broadcasted_iota(jnp.int32, (ts, tq), 1)
                s = jnp.where(col <= row, s, NEG)
            m_prev = m_scr[u][...]
            m_new = jnp.maximum(m_prev, jnp.max(s, axis=1, keepdims=True))
            p = jnp.exp2(s - _lane_tile(m_new, tq // LANES))
            alpha = jnp.exp2(m_prev - m_new)
            acc_scr[u][...] = acc_scr[u][...] * _lane_tile(alpha, 2) + _dot(p.astype(bf16), va)
            m_scr[u][...] = m_new

    def run_blocks(first, n, *, prefetch_next, mask_last):
        bufs = (s_a, s_b)
        for b in range(n):
            if b + 1 < n or prefetch_next:
                scores(first + b + 1, bufs[(b + 1) % 2])
            accumulate(first + b, bufs[b % 2], mask_last and b == n - 1)

    scores(0, s_a)

    def body(jj, c):
        run_blocks(LOOP_BLOCKS * jj, LOOP_BLOCKS, prefetch_next=True, mask_last=False)
        return c

    lax.fori_loop(0, i // LOOP_BLOCKS, body, 0)

    for rest in range(LOOP_BLOCKS):
        @pl.when(i % LOOP_BLOCKS == rest)
        def _():
            run_blocks(i - rest, rest + 1, prefetch_next=False, mask_last=True)

    for u in range(n_sub):
        acc = acc_scr[u][...]
        o_ref[u * ts:(u + 1) * ts, :] = (acc[:, :LANES] / acc[:, LANES:]).astype(o_ref.dtype)


def _prompt_attn(q, k, kx, v, *, n_heads):
    s = q.shape[0]
    tq = min(TQ, s)
    grid = (n_heads, s // tq)
    q_blk = pl.BlockSpec((tq, HEAD_DIM), lambda h, i: (i, h))
    kv_blk = pl.BlockSpec((s, HEAD_DIM), lambda h, i: (0, h))
    return pl.pallas_call(
        functools.partial(_prompt_attn_kernel, tq=tq, n_sub=N_SUB),
        grid=grid, in_specs=[q_blk, kv_blk, kv_blk, kv_blk], out_specs=q_blk,
        out_shape=jax.ShapeDtypeStruct(q.shape, bf16),
        scratch_shapes=([pltpu.VMEM((tq // N_SUB, LANES), f32)] * N_SUB
                        + [pltpu.VMEM((tq // N_SUB, 2 * LANES), f32)] * N_SUB
                        + [pltpu.VMEM((tq // N_SUB, tq), f32)] * (2 * N_SUB)),
        compiler_params=_params(("arbitrary", "arbitrary")), name="prompt_attn")(q, k, kx, v)


def _sample_attn_kernel(pt_ref, q_ref, knew_hbm, vnew_hbm, lnew_hbm, ck_hbm, cv_hbm, cl_hbm,
                        o_ref, kbuf, vbuf, lbuf, bias_scr, sbuf, sem,
                        *, n_pages, n_batch, n_heads, n_new, page):
    b = pl.program_id(0)
    slot = b % 2
    npg = n_pages + 1
    rows = page * n_heads
    new_rows = n_new * n_heads

    def copies(bb, sl):
        cps = []
        for p in range(n_pages):
            pg = pt_ref[bb * n_pages + p]
            src = pl.ds(pl.multiple_of(pg * rows, rows), rows)
            dst = pl.ds(p * rows, rows)
            cps.append(pltpu.make_async_copy(ck_hbm.at[src], kbuf.at[sl, dst], sem.at[sl, 0]))
            cps.append(pltpu.make_async_copy(cv_hbm.at[src], vbuf.at[sl, dst], sem.at[sl, 1]))
            cps.append(pltpu.make_async_copy(cl_hbm.at[pg], lbuf.at[sl, p], sem.at[sl, 2]))
        src = pl.ds(pl.multiple_of(bb * new_rows, new_rows), new_rows)
        dst = pl.ds(n_pages * rows, new_rows)
        cps.append(pltpu.make_async_copy(knew_hbm.at[src], kbuf.at[sl, dst], sem.at[sl, 0]))
        cps.append(pltpu.make_async_copy(vnew_hbm.at[src], vbuf.at[sl, dst], sem.at[sl, 1]))
        cps.append(pltpu.make_async_copy(lnew_hbm.at[bb], lbuf.at[sl, n_pages], sem.at[sl, 2]))
        return cps

    @pl.when(b == 0)
    def _():
        for sl in range(2):
            kbuf[sl, pl.ds(n_pages * rows, rows), :] = jnp.zeros((rows, HEAD_DIM), f32)
            vbuf[sl, pl.ds(n_pages * rows, rows), :] = jnp.zeros((rows, HEAD_DIM), f32)
        for cp in copies(0, 0):
            cp.start()

    @pl.when(b + 1 < n_batch)
    def _():
        for cp in copies(b + 1, 1 - slot):
            cp.start()

    for cp in copies(b, slot):
        cp.wait()

    row = lax.broadcasted_iota(jnp.int32, (page, page), 0)
    col = lax.broadcasted_iota(jnp.int32, (page, page), 1)
    after = jnp.where(row > col, 1.0, 0.0).astype(bf16)
    lf_all = lbuf[slot].reshape(npg * n_heads, page)
    hi, mid, lo = _split3(lf_all)
    suf = (_dot(hi, after) + _dot(mid, after)) + _dot(lo, after)
    tot = jnp.sum(lf_all, axis=1, keepdims=True)
    carry = jnp.zeros((n_heads, 1), f32)
    for p in range(npg - 1, -1, -1):
        rs = slice(p * n_heads, (p + 1) * n_heads)
        bias_scr[p] = (suf[rs, :] + carry) * LOG2E
        carry = carry + tot[rs, :]

    qb = q_ref[0].astype(bf16)
    qh = [qb[:, h * HEAD_DIM:(h + 1) * HEAD_DIM] for h in range(n_heads)]
    t_row = lax.broadcasted_iota(jnp.int32, (8, page), 0)
    lane = lax.broadcasted_iota(jnp.int32, (8, page), 1)

    def head_rows(buf, p, h):
        return buf[slot, pl.ds(p * rows + h, page, stride=n_heads), :].astype(bf16)

    def score_page(p, mvec, masked):
        out = []
        for h in range(n_heads):
            s = _dot_nt(qh[h], head_rows(kbuf, p, h)) + bias_scr[p, pl.ds(h, 1), :]
            if masked:
                s = jnp.where(lane <= t_row, s, NEG)
            sbuf[p, h] = s
            out.append(jnp.maximum(mvec[h], s))
        return tuple(out)

    unroll = math.gcd(n_pages, PAGE_UNROLL)
    mvec = lax.fori_loop(0, n_pages, functools.partial(score_page, masked=False),
                         tuple(jnp.full((8, page), NEG, f32) for _ in range(n_heads)),
                         unroll=unroll)
    mvec = score_page(n_pages, mvec, True)
    m = [jnp.max(mv, axis=1, keepdims=True) for mv in mvec]

    def value_page(p, carry):
        lvec, acc = carry
        lout, aout = [], []
        for h in range(n_heads):
            pr = jnp.exp2(sbuf[p, h] - m[h])
            lout.append(lvec[h] + pr)
            aout.append(acc[h] + _dot(pr.astype(bf16), head_rows(vbuf, p, h)))
        return tuple(lout), tuple(aout)

    zeros = tuple(jnp.zeros((8, page), f32) for _ in range(n_heads))
    lvec, acc = lax.fori_loop(0, n_pages, value_page, (zeros, zeros), unroll=unroll)
    lvec, acc = value_page(n_pages, (lvec, acc))
    for h in range(n_heads):
        l = jnp.sum(lvec[h], axis=1, keepdims=True)
        o_ref[0, :, h * HEAD_DIM:(h + 1) * HEAD_DIM] = acc[h] / l


def _sample_attn(page_table, q8, knew, vnew, lnew, ck, cv, cl, *, n_heads, n_new):
    n_batch, n_pages = page_table.shape
    page = cl.shape[-1]
    assert page == HEAD_DIM == LANES
    a = q8.shape[-1]
    rows = (n_pages + 1) * page * n_heads
    any_spec = pl.BlockSpec(memory_space=pl.ANY)
    q_blk = pl.BlockSpec((1, 8, a), lambda b, pt: (b, 0, 0))
    grid_spec = pltpu.PrefetchScalarGridSpec(
        num_scalar_prefetch=1, grid=(n_batch,),
        in_specs=[q_blk, any_spec, any_spec, any_spec, any_spec, any_spec, any_spec],
        out_specs=q_blk,
        scratch_shapes=[
            pltpu.VMEM((2, rows, HEAD_DIM), f32),
            pltpu.VMEM((2, rows, HEAD_DIM), f32),
            pltpu.VMEM((2, n_pages + 1, n_heads, page), f32),
            pltpu.VMEM((n_pages + 1, n_heads, page), f32),
            pltpu.VMEM((n_pages + 1, n_heads, 8, page), f32),
            pltpu.SemaphoreType.DMA((2, 3)),
        ])
    return pl.pallas_call(
        functools.partial(_sample_attn_kernel, n_pages=n_pages, n_batch=n_batch,
                          n_heads=n_heads, n_new=n_new, page=page),
        grid_spec=grid_spec, out_shape=jax.ShapeDtypeStruct(q8.shape, f32),
        compiler_params=_params(), name="sample_attn")(
            page_table.reshape(-1), q8, knew, vnew, lnew, ck, cv, cl)


def _layer_norm(y, g, b):
    mu = jnp.mean(y, axis=-1, keepdims=True)
    yc = y - mu
    var = jnp.mean(yc * yc, axis=-1, keepdims=True)
    return yc * lax.rsqrt(var + LN_EPS) * g + b


def _finish1_kernel(ap_ref, cp_ref, xp_ref, as_ref, cs_ref, xs_ref,
                    woa_ref, woc_ref, g_ref, b_ref, wr2_ref, wrh_ref, br_ref,
                    x1_ref, x1g_ref, meta_ref, cnt_ref, gw_ref, cnt_scr, *, alpha, nb_prompt, tm):
    is_prompt = pl.program_id(0) < nb_prompt
    attn = jnp.where(is_prompt, ap_ref[...], as_ref[...])
    conv = jnp.where(is_prompt, cp_ref[...], cs_ref[...])
    x = jnp.where(is_prompt, xp_ref[...], xs_ref[...])
    o = _dot(attn, woa_ref[...]) + _dot(conv, woc_ref[...])
    x1 = _layer_norm(alpha * x + o, g_ref[...], b_ref[...])
    x1_ref[...] = x1
    n_chunks = x1.shape[1] // LANES
    for c in range(n_chunks):
        x1g_ref[pl.ds(c, tm, stride=n_chunks), :] = x1[:, c * LANES:(c + 1) * LANES]

    x1b = x1.astype(bf16)
    x1l = (x1 - x1b.astype(f32)).astype(bf16)
    r2 = _dot(x1b, wr2_ref[...])
    lr = r2[:, :LANES] + r2[:, LANES:] + _dot(x1l, wrh_ref[...]) + br_ref[...]
    lane = lax.broadcasted_iota(jnp.int32, lr.shape, 1)

    def first_max(mask):
        vals = jnp.where(mask, lr, -jnp.inf)
        mx = jnp.max(vals, axis=-1, keepdims=True)
        idx = jnp.min(jnp.where(mask & (vals == mx), lane, LANES), axis=-1, keepdims=True)
        return mx, idx

    is_group = lane < N_GROUPS
    gmax, gidx = first_max(is_group)
    p_group = 1.0 / jnp.sum(jnp.where(is_group, jnp.exp(lr - gmax), 0.0), axis=-1, keepdims=True)
    lo = N_GROUPS + gidx * EXPERTS_PER_GROUP
    in_group = (lane >= lo) & (lane < lo + EXPERTS_PER_GROUP)
    e1, i1 = first_max(in_group)
    e2, i2 = first_max(in_group & (lane != i1))
    r = jnp.exp(e2 - e1)
    w1 = p_group / (1.0 + r)
    w2 = p_group * r / (1.0 + r)
    gw_ref[...] = jnp.where(lane == 0, w1, jnp.where(lane == 1, w2, 0.0))

    @pl.when(pl.program_id(0) == 0)
    def _():
        cnt_scr[...] = jnp.zeros(cnt_scr.shape, f32)

    id1, id2 = i1 - N_GROUPS, i2 - N_GROUPS
    sel1 = jnp.where(lane == id1, 1.0, 0.0)
    sel2 = jnp.where(lane == id2, 1.0, 0.0)
    row = lax.broadcasted_iota(jnp.int32, (tm, tm), 0)
    col = lax.broadcasted_iota(jnp.int32, (tm, tm), 1)
    before = jnp.where(col < row, 1.0, 0.0).astype(bf16)
    seen = cnt_scr[0:1, :]
    tot1 = jnp.sum(sel1, axis=0, keepdims=True)
    rank1 = jnp.sum(sel1 * (seen + _dot(before, sel1.astype(bf16))), axis=-1, keepdims=True)
    rank2 = jnp.sum(sel2 * ((seen + tot1) + _dot(before, sel2.astype(bf16))), axis=-1, keepdims=True)
    seen = (seen + tot1) + jnp.sum(sel2, axis=0, keepdims=True)
    cnt_scr[...] = jnp.broadcast_to(seen, cnt_scr.shape)
    cnt_ref[...] = jnp.broadcast_to(seen, cnt_ref.shape)
    meta = jnp.where(lane == 0, id1.astype(f32), jnp.where(lane == 1, id2.astype(f32),
                     jnp.where(lane == 2, rank1, jnp.where(lane == 3, rank2, 0.0))))
    meta_ref[...] = meta.T[0:8, :]


def _finish1(attn_p, conv_p, x_p, attn_s, conv_s, x_s, wo, g, b, wr2, wrh, br, *, alpha):
    (t_p, d), t_s = x_p.shape, x_s.shape[0]
    a = attn_p.shape[1]
    assert wo.shape[0] == 2 * a
    tm = min(TM_PROJ, t_s)
    nb_p, nb_s = t_p // tm, t_s // tm
    t_all = t_p + t_s
    p_blk = lambda w: pl.BlockSpec((tm, w), lambda i: (jnp.minimum(i, nb_p - 1), 0))
    s_blk = lambda w: pl.BlockSpec((tm, w), lambda i: (jnp.maximum(i - nb_p, 0), 0))
    out_blk = lambda w: pl.BlockSpec((tm, w), lambda i: (i, 0))
    in_specs = [p_blk(a), p_blk(a), p_blk(d), s_blk(a), s_blk(a), s_blk(d),
                pl.BlockSpec((a, d), lambda i: (0, 0), pipeline_mode=pl.Buffered(1)),
                pl.BlockSpec((a, d), lambda i: (1, 0), pipeline_mode=pl.Buffered(1)),
                _resident(g.shape), _resident(b.shape), _resident(wr2.shape),
                _resident(wrh.shape), _resident(br.shape)]
    n_chunks = d // LANES
    return pl.pallas_call(
        functools.partial(_finish1_kernel, alpha=alpha, nb_prompt=nb_p, tm=tm),
        grid=(nb_p + nb_s,), in_specs=in_specs,
        out_specs=[out_blk(d), pl.BlockSpec((tm * n_chunks, LANES), lambda i: (i, 0)),
                   pl.BlockSpec((8, tm), lambda i: (0, i)),
                   pl.BlockSpec((8, LANES), lambda i: (0, 0)), out_blk(LANES)],
        out_shape=[jax.ShapeDtypeStruct((t_all, d), f32),
                   jax.ShapeDtypeStruct((t_all * n_chunks, LANES), f32),
                   jax.ShapeDtypeStruct((8, t_all), f32),
                   jax.ShapeDtypeStruct((8, LANES), f32),
                   jax.ShapeDtypeStruct((t_all, LANES), f32)],
        scratch_shapes=[pltpu.VMEM((8, LANES), f32)],
        compiler_params=_params(), name="finish1")(
            attn_p, conv_p, x_p, attn_s, conv_s, x_s, wo, wo, g, b, wr2, wrh, br)


def _dispatch_kernel(lt_ref, nu_ref, pos_ref, x_ref, xs_hbm, zbuf, sem, zsem, *, tm, tm_moe, n_tiles):
    s = pl.program_id(0)

    @pl.when(s == 0)
    def _():
        zbuf[...] = jnp.zeros(zbuf.shape, f32)

        def zero_tile(row0):
            return pltpu.make_async_copy(zbuf, xs_hbm.at[pl.ds(row0, tm_moe)], zsem)

        def each_zero_tile(act):
            for e in range(N_EXPERTS):
                @pl.when(lt_ref[e] >= 0)
                def _():
                    act(zero_tile(lt_ref[e]))

            def unused(t, c):
                act(zero_tile(t * tm_moe))
                return c

            lax.fori_loop(nu_ref[0], n_tiles, unused, 0)

        each_zero_tile(lambda cp: cp.start())
        each_zero_tile(lambda cp: cp.wait())

    def issue(r, c):
        pltpu.make_async_copy(x_ref.at[r], xs_hbm.at[pos_ref[0, 0, r]], sem).start(priority=0)
        pltpu.make_async_copy(x_ref.at[r], xs_hbm.at[pos_ref[0, 0, tm + r]], sem).start(priority=1)
        return c

    lax.fori_loop(0, tm, issue, 0, unroll=8)
    done = xs_hbm.at[pl.ds(0, 2 * tm)]
    pltpu.make_async_copy(done, done, sem).wait()


def _dispatch(x1g, pos, last_tile_row, n_used, *, n_tiles, tm_moe):
    t, n_chunks, _ = x1g.shape
    n_steps, _, two_tm = pos.shape
    tm = two_tm // 2
    grid_spec = pltpu.PrefetchScalarGridSpec(
        num_scalar_prefetch=2, grid=(n_steps,),
        in_specs=[pl.BlockSpec((1, 1, two_tm), lambda s, *_: (s, 0, 0), memory_space=pltpu.SMEM),
                  pl.BlockSpec((tm, n_chunks, LANES), lambda s, *_: (s, 0, 0))],
        out_specs=pl.BlockSpec(memory_space=pl.ANY),
        scratch_shapes=[pltpu.VMEM((tm_moe, n_chunks, LANES), f32),
                        pltpu.SemaphoreType.DMA(()), pltpu.SemaphoreType.DMA(())])
    return pl.pallas_call(
        functools.partial(_dispatch_kernel, tm=tm, tm_moe=tm_moe, n_tiles=n_tiles),
        grid_spec=grid_spec,
        out_shape=jax.ShapeDtypeStruct((n_tiles * tm_moe, n_chunks, LANES), f32),
        compiler_params=_params(), name="dispatch")(last_tile_row, n_used, pos, x1g)


def _moe_kernel(te_ref, ord_ref, nxt_ref, nu_ref, x_ref, wg_hbm, wu_hbm, wd_hbm, y_ref,
                wgf, wuf, wdf, wgb, wub, wdb, wsem, *, tm, n_chunks):
    t = pl.program_id(0)
    n_used = nu_ref[0]

    def weight_copies(e, wslot):
        return (pltpu.make_async_copy(wg_hbm.at[e], wgf.at[wslot], wsem.at[wslot, 0]),
                pltpu.make_async_copy(wu_hbm.at[e], wuf.at[wslot], wsem.at[wslot, 1]),
                pltpu.make_async_copy(wd_hbm.at[e], wdf.at[wslot], wsem.at[wslot, 2]))

    @pl.when(t == 0)
    def _():
        for cp in weight_copies(te_ref[0], 0):
            cp.start(priority=1)

    @pl.when(t < n_used)
    def _():
        e = te_ref[t]
        new_expert = (t == 0) | (e != te_ref[jnp.maximum(t - 1, 0)])

        @pl.when(new_expert)
        def _():
            wslot = ord_ref[t] % 2
            for cp in weight_copies(e, wslot):
                cp.wait()
            nxt = nxt_ref[e]

            @pl.when(nxt >= 0)
            def _():
                for cp in weight_copies(nxt, 1 - wslot):
                    cp.start(priority=1)

            wgb[...] = wgf[wslot].astype(bf16)
            wub[...] = wuf[wslot].astype(bf16)
            wdb[...] = wdf[wslot].astype(bf16)

        xb = jnp.concatenate([x_ref[pl.ds(c, tm, stride=n_chunks), :] for c in range(n_chunks)],
                             axis=-1).astype(bf16)
        hg = _dot(xb, wgb[...])
        hu = _dot(xb, wub[...])
        act = hg * (1.0 / (1.0 + jnp.exp(-hg))) * hu
        y = _dot(act.astype(bf16), wdb[...])
        for c in range(n_chunks):
            y_ref[pl.ds(c, tm, stride=n_chunks), :] = y[:, c * LANES:(c + 1) * LANES]

    @pl.when(t >= n_used)
    def _():
        y_ref[...] = jnp.zeros(y_ref.shape, f32)


def _moe(xs, tile_expert, tile_ord, next_expert, n_used, wg, wu, wd, *, tm):
    _, d, f = wg.shape
    n_chunks = d // LANES
    n_tiles = xs.shape[0] // (tm * n_chunks)
    any_spec = pl.BlockSpec(memory_space=pl.ANY)

    def x_idx(t, te, od, nx, nu):
        return (jnp.minimum(t, jnp.maximum(nu[0] - 1, 0)), 0)

    grid_spec = pltpu.PrefetchScalarGridSpec(
        num_scalar_prefetch=4, grid=(n_tiles,),
        in_specs=[pl.BlockSpec((tm * n_chunks, LANES), x_idx), any_spec, any_spec, any_spec],
        out_specs=pl.BlockSpec((tm * n_chunks, LANES), lambda t, *_: (t, 0)),
        scratch_shapes=[pltpu.VMEM((2, d, f), f32), pltpu.VMEM((2, d, f), f32),
                        pltpu.VMEM((2, f, d), f32),
                        pltpu.VMEM((d, f), bf16), pltpu.VMEM((d, f), bf16), pltpu.VMEM((f, d), bf16),
                        pltpu.SemaphoreType.DMA((2, 3))])
    return pl.pallas_call(
        functools.partial(_moe_kernel, tm=tm, n_chunks=n_chunks),
        grid_spec=grid_spec, out_shape=jax.ShapeDtypeStruct(xs.shape, f32),
        compiler_params=_params(), name="moe")(
            tile_expert, tile_ord, next_expert, n_used, xs, wg, wu, wd)


def _finish2_kernel(idx_ref, ys_hbm, x1_ref, gw_ref, g_ref, b_ref, o_ref, gbuf, sem,
                    *, tm, n_steps, alpha):
    s = pl.program_id(0)
    n_chunks = ys_hbm.shape[1]
    pitch = n_chunks + ROW_PITCH_PAD

    @pl.when(s < n_steps)
    def _():
        slot = s % 2

        def issue(r, c):
            for k in range(2):
                rr = k * tm + r
                pltpu.make_async_copy(ys_hbm.at[idx_ref[0, 0, rr]],
                                      gbuf.at[slot, pl.ds(rr * pitch, n_chunks)],
                                      sem.at[slot]).start(priority=k)
            return c

        lax.fori_loop(0, tm, issue, 0, unroll=4)

    @pl.when(s >= 1)
    def _():
        slot = (s - 1) % 2
        landed = gbuf.at[slot, pl.ds(0, 2 * tm * n_chunks)]
        pltpu.make_async_copy(landed, landed, sem.at[slot]).wait()

        def rows(first):
            return jnp.concatenate(
                [gbuf[slot, pl.ds(first * pitch + c, tm, stride=pitch), :] for c in range(n_chunks)],
                axis=-1)

        gw = gw_ref[...]
        f = gw[:, 0:1] * rows(0) + gw[:, 1:2] * rows(tm)
        o_ref[...] = _layer_norm(alpha * x1_ref[...] + f, g_ref[...], b_ref[...])


def _finish2(ys, pos, x1, gw, g, b, *, alpha, row0, t):
    n_steps, _, two_tm = pos.shape
    tm = two_tm // 2
    d = x1.shape[1]
    n_chunks = d // LANES
    blk0 = row0 // tm
    last = n_steps - 1
    in_blk = lambda w: pl.BlockSpec((tm, w), lambda s: (jnp.maximum(s - 1, 0) + blk0, 0))
    return pl.pallas_call(
        functools.partial(_finish2_kernel, tm=tm, n_steps=n_steps, alpha=alpha),
        grid=(n_steps + 1,),
        in_specs=[pl.BlockSpec((1, 1, two_tm), lambda s: (jnp.minimum(s, last), 0, 0),
                               memory_space=pltpu.SMEM),
                  pl.BlockSpec(memory_space=pl.ANY),
                  in_blk(d), in_blk(LANES), _resident(g.shape), _resident(b.shape)],
        out_specs=pl.BlockSpec((tm, d), lambda s: (jnp.maximum(s - 1, 0), 0)),
        out_shape=jax.ShapeDtypeStruct((t, d), f32),
        scratch_shapes=[pltpu.VMEM((2, two_tm * (n_chunks + ROW_PITCH_PAD), LANES), f32),
                        pltpu.SemaphoreType.DMA((2,))],
        compiler_params=_params(), name="finish2")(pos, ys, x1, gw, g, b)


def _bias_selectors(n_heads, sign):
    sel = np.zeros((3, n_heads, n_heads * HEAD_DIM), np.float32)
    for p in range(3):
        for h in range(n_heads):
            sel[p, h, h * HEAD_DIM + p] = sign
    return sel


def _route_tables(meta, counts, tm):
    n_tok = meta.shape[1]
    e = meta[0:2].astype(jnp.int32)
    rank = meta[2:4].astype(jnp.int32)
    counts = counts[0, :N_EXPERTS].astype(jnp.int32)
    n_pairs = 2 * n_tok
    padded = ((counts + tm - 1) // tm) * tm
    ends = jnp.cumsum(padded)
    experts = jnp.arange(N_EXPERTS, dtype=jnp.int32)
    first_row = jnp.sum(jnp.where(e[..., None] == experts, ends - padded, 0), axis=-1)
    pos = first_row + rank
    n_tiles = (n_pairs + N_EXPERTS * (tm - 1)) // tm
    last_tile_row = jnp.where(counts > 0, ends - tm, -1).astype(jnp.int32)
    tile_start = jnp.arange(n_tiles, dtype=jnp.int32) * tm
    tile_expert = jnp.minimum(
        jnp.sum((ends[None, :] <= tile_start[:, None]).astype(jnp.int32), axis=1), N_EXPERTS - 1)
    n_used = (ends[-1] // tm).astype(jnp.int32).reshape(1)
    nonempty = counts > 0
    expert_ord = jnp.cumsum(nonempty.astype(jnp.int32)) - 1
    later = nonempty[None, :] & (experts[None, :] > experts[:, None])
    next_expert = jnp.min(jnp.where(later, experts[None, :], N_EXPERTS), axis=1)
    next_expert = jnp.where(next_expert < N_EXPERTS, next_expert, -1).astype(jnp.int32)
    tile_ord = expert_ord[tile_expert].astype(jnp.int32)
    return pos, last_tile_row, n_tiles, tile_expert, tile_ord, next_expert, n_used


def _tile_pos(pos, tm):
    t = pos.shape[1]
    return jnp.concatenate([pos[0].reshape(t // tm, 1, tm), pos[1].reshape(t // tm, 1, tm)], axis=2)


def kernel(x_prompt, x_sample, cache_k, cache_v, cache_logf, state_conv, page_table, w_in, b_f,
           w_conv, w_out, ln1_g, ln1_b, w_rg, b_rg, w_re, b_re, w_gate, w_up, w_down, ln2_g, ln2_b):
    depth, d_model, _ = w_in.shape
    n_batch_p, seq, _ = x_prompt.shape
    db, dseq, _ = x_sample.shape
    n_heads = cache_k.shape[3]
    a = n_heads * HEAD_DIM
    c = w_conv.shape[-1]
    assert n_batch_p == 1 and dseq <= 8 and w_conv.shape[1] == CONV_K
    alpha = (2.0 * depth) ** 0.25
    qscale = HEAD_DIM ** -0.5 * LOG2E
    t_p, t_s = seq, db * dseq
    t_all = t_p + t_s

    sel_neg = _bias_selectors(n_heads, -1.0)
    psel_k = jnp.asarray(np.concatenate(
        [np.pad(sel_neg[p], ((0, LANES - n_heads), (0, 0))) for p in range(3)], axis=0), bf16)

    xp = x_prompt.reshape(t_p, d_model)
    xs = x_sample.reshape(t_s, d_model)
    outs = {k: [] for k in ("kp", "vp", "fp", "cp", "ks", "vs", "fs", "cs")}
    for l in range(depth):
        wl = jnp.swapaxes(w_in[l], 0, 1)
        wq, wk, wv = (wl[j * a:(j + 1) * a].astype(bf16) for j in range(3))
        o = 3 * a
        wf = jnp.pad(wl[o:o + n_heads], ((0, LANES - n_heads), (0, 0)))
        wfh = wf.astype(bf16)
        wf2 = jnp.concatenate([wfh, (wf - wfh.astype(f32)).astype(bf16)], axis=0)
        bfp = jnp.pad(b_f[l], (0, LANES - n_heads)).reshape(1, LANES)
        o += n_heads
        wb, wc, wh = (wl[o + j * c:o + (j + 1) * c].astype(bf16) for j in range(3))
        wo = w_out[l].astype(bf16)
        wr = jnp.pad(jnp.concatenate([w_rg[l], w_re[l]], axis=1),
                     ((0, 0), (0, LANES - N_GROUPS - N_EXPERTS)))
        wrh = wr.astype(bf16)
        wr2 = jnp.concatenate([wrh, (wr - wrh.astype(f32)).astype(bf16)], axis=1)
        br = jnp.pad(jnp.concatenate([b_rg[l], b_re[l]]), (0, LANES - N_GROUPS - N_EXPERTS)).reshape(1, LANES)
        g1, b1 = ln1_g[l].reshape(1, -1), ln1_b[l].reshape(1, -1)
        g2, b2 = ln2_g[l].reshape(1, -1), ln2_b[l].reshape(1, -1)

        q, ko, vo, kb, vb, lf, kx = _qkv_proj(xp, wq, wk, wv, wf2, wfh, bfp, psel_k,
                                              n_heads=n_heads, qscale=qscale, with_cum=True)
        conv, utail = _conv_proj(xp, wb, wc, wh, w_conv[l])
        attn = _prompt_attn(q, kb, kx, vb, n_heads=n_heads)
        outs["kp"].append(ko.reshape(1, t_p, n_heads, HEAD_DIM))
        outs["vp"].append(vo.reshape(1, t_p, n_heads, HEAD_DIM))
        outs["fp"].append(lf[:, :n_heads].reshape(1, t_p, n_heads))
        outs["cp"].append(utail[8 - (CONV_K - 1):].reshape(1, CONV_K - 1, c))

        qs, kos, vos, _, _, lfs = _qkv_proj(xs, wq, wk, wv, wf2, wfh, bfp, None,
                                            n_heads=n_heads, qscale=qscale, with_cum=False)
        st = state_conv[l]
        zero = jnp.zeros((db, dseq, c), f32)
        p1 = zero.at[:, 0].set(st[:, 1]).reshape(t_s, c)
        p2 = zero.at[:, 0].set(st[:, 0]).at[:, 1].set(st[:, 1]).reshape(t_s, c)
        conv_s, u_s = _conv_proj(xs, wb, wc, wh, w_conv[l], p1, p2, group=dseq)
        q8 = jnp.pad(qs.reshape(db, dseq, a), ((0, 0), (0, 8 - dseq), (0, 0))).astype(f32)
        page = cache_k.shape[2]
        lnew = jnp.pad(jnp.swapaxes(lfs[:, :n_heads].reshape(db, dseq, n_heads), 1, 2),
                       ((0, 0), (0, 0), (0, page - dseq)))
        attn_s = _sample_attn(page_table, q8, kos, vos, lnew,
                              cache_k[l].reshape(-1, HEAD_DIM), cache_v[l].reshape(-1, HEAD_DIM),
                              jnp.swapaxes(cache_logf[l], 1, 2), n_heads=n_heads, n_new=dseq)
        attn_s = attn_s[:, :dseq].reshape(t_s, a).astype(bf16)
        outs["ks"].append(kos.reshape(db, dseq, n_heads, HEAD_DIM))
        outs["vs"].append(vos.reshape(db, dseq, n_heads, HEAD_DIM))
        outs["fs"].append(lfs[:, :n_heads].reshape(db, dseq, n_heads))
        outs["cs"].append(u_s.reshape(db, dseq, c)[:, dseq - (CONV_K - 1):])
        x1, x1g, meta, counts, gw = _finish1(attn, conv, xp, attn_s, conv_s, xs, wo, g1, b1,
                                             wr2, wrh, br, alpha=alpha)

        pos, last_tile_row, n_tiles, tile_expert, tile_ord, next_expert, n_used = _route_tables(
            meta, counts, TM_MOE)
        n_chunks = d_model // LANES
        tm2 = min(TM_PROJ, t_s)
        xsort = _dispatch(x1g.reshape(t_all, n_chunks, LANES), _tile_pos(pos, tm2), last_tile_row,
                          n_used, n_tiles=n_tiles, tm_moe=TM_MOE)
        ys = _moe(xsort.reshape(-1, LANES), tile_expert, tile_ord, next_expert, n_used,
                  w_gate[l], w_up[l], w_down[l], tm=TM_MOE)
        ys = ys.reshape(-1, n_chunks, LANES)
        xp = _finish2(ys, _tile_pos(pos[:, :t_p], TM_PROJ), x1, gw, g2, b2, alpha=alpha, row0=0, t=t_p)
        xs = _finish2(ys, _tile_pos(pos[:, t_p:], tm2), x1, gw, g2, b2, alpha=alpha, row0=t_p, t=t_s)

    stack = lambda k: jnp.stack(outs[k])
    return (xp.reshape(x_prompt.shape), xs.reshape(x_sample.shape),
            stack("kp"), stack("vp"), stack("fp"), stack("cp"),
            stack("ks"), stack("vs"), stack("fs"), stack("cs"))
```

```python
import functools
import math

import numpy as np
import jax
import jax.numpy as jnp
from jax import lax
from jax.experimental import pallas as pl
from jax.experimental.pallas import tpu as pltpu

f32 = jnp.float32
bf16 = jnp.bfloat16

LANES = 128
HEAD_DIM = 128
LOG2E = 1.4426950408889634
LN_EPS = 1e-5
NEG = -1e30
CONV_K = 3
N_GROUPS = 4
EXPERTS_PER_GROUP = 8
N_EXPERTS = N_GROUPS * EXPERTS_PER_GROUP
VMEM_LIMIT = 56 * 1024 * 1024

TM_IN = 512
TM_PROJ = 256
TQ = 512
N_SUB = 2
LOOP_BLOCKS = 4
PAGE_UNROLL = 16
TM_MOE = 256
ROW_PITCH_PAD = 1


def _dot(a, b):
    return jnp.dot(a, b, preferred_element_type=f32)


def _dot_nt(a, b):
    return lax.dot_general(a, b, (((1,), (1,)), ((), ())), preferred_element_type=f32)


def _lane_tile(x, n):
    return jnp.concatenate([x] * n, axis=-1)


def _split3(x):
    hi = x.astype(bf16)
    r1 = x - hi.astype(f32)
    mid = r1.astype(bf16)
    lo = (r1 - mid.astype(f32)).astype(bf16)
    return hi, mid, lo


def _params(sem=("arbitrary",)):
    return pltpu.CompilerParams(dimension_semantics=sem, vmem_limit_bytes=VMEM_LIMIT)


def _resident(shape):
    nd = len(shape)
    return pl.BlockSpec(shape, lambda *_: (0,) * nd, pipeline_mode=pl.Buffered(1))


def _qkv_kernel(x_ref, wq_ref, wk_ref, wv_ref, wf2_ref, wfh_ref, bf_ref, psel_ref,
                q_ref, ko_ref, vo_ref, kb_ref, vb_ref, lf_ref, kx_ref, carry_ref,
                *, tm, n_heads, qscale):
    i = pl.program_id(0)
    x = x_ref[...]
    xb = x.astype(bf16)
    q_ref[...] = (_dot_nt(xb, wq_ref[...]) * qscale).astype(bf16)
    k = _dot_nt(xb, wk_ref[...])
    kb_ref[...] = k.astype(bf16)
    v = _dot_nt(xb, wv_ref[...])
    vb_ref[...] = v.astype(bf16)
    for h in range(n_heads):
        ko_ref[pl.ds(h, tm, stride=n_heads), :] = k[:, h * HEAD_DIM:(h + 1) * HEAD_DIM]
        vo_ref[pl.ds(h, tm, stride=n_heads), :] = v[:, h * HEAD_DIM:(h + 1) * HEAD_DIM]

    xl = (x - xb.astype(f32)).astype(bf16)
    zf2 = _dot_nt(xb, wf2_ref[...])
    zf = zf2[:, :LANES] + zf2[:, LANES:] + _dot_nt(xl, wfh_ref[...]) + bf_ref[...]
    lf = jnp.minimum(zf, 0.0) - jnp.log1p(jnp.exp(-jnp.abs(zf)))
    lf_ref[...] = lf

    if kx_ref is not None:
        @pl.when(i == 0)
        def _():
            carry_ref[...] = jnp.zeros(carry_ref.shape, f32)

        row = lax.broadcasted_iota(jnp.int32, (tm, tm), 0)
        col = lax.broadcasted_iota(jnp.int32, (tm, tm), 1)
        tri = jnp.where(row >= col, 1.0, 0.0).astype(bf16)
        c3 = _dot(tri, jnp.concatenate(_split3(lf), axis=-1))
        cum = (c3[:, :LANES] + c3[:, LANES:2 * LANES] + c3[:, 2 * LANES:]) + carry_ref[0:1, :]
        carry_ref[...] = jnp.broadcast_to(cum[tm - 1:tm, :], carry_ref.shape)
        parts = jnp.concatenate(_split3(cum * LOG2E), axis=-1)
        kx_ref[...] = _dot(parts, psel_ref[...]).astype(bf16)


def _qkv_proj(x, wq, wk, wv, wf2, wfh, bfp, psel, *, n_heads, qscale, with_cum):
    t, d = x.shape
    a = wq.shape[0]
    tm = min(TM_IN, t)
    grid = (t // tm,)
    row_blk = lambda w: pl.BlockSpec((tm, w), lambda i: (i, 0))
    in_specs = [row_blk(d), _resident(wq.shape), _resident(wk.shape), _resident(wv.shape),
                _resident(wf2.shape), _resident(wfh.shape), _resident(bfp.shape)]
    args = [x, wq, wk, wv, wf2, wfh, bfp]
    out_shape = [jax.ShapeDtypeStruct((t, a), bf16),
                 jax.ShapeDtypeStruct((t * n_heads, HEAD_DIM), f32),
                 jax.ShapeDtypeStruct((t * n_heads, HEAD_DIM), f32),
                 jax.ShapeDtypeStruct((t, a), bf16),
                 jax.ShapeDtypeStruct((t, a), bf16),
                 jax.ShapeDtypeStruct((t, LANES), f32)]
    hd_blk = pl.BlockSpec((tm * n_heads, HEAD_DIM), lambda i: (i, 0))
    out_specs = [row_blk(a), hd_blk, hd_blk, row_blk(a), row_blk(a), row_blk(LANES)]
    scratch = []
    if with_cum:
        in_specs.append(_resident(psel.shape))
        args.append(psel)
        out_shape.append(jax.ShapeDtypeStruct((t, a), bf16))
        out_specs.append(row_blk(a))
        scratch.append(pltpu.VMEM((8, LANES), f32))

        def body(*refs):
            _qkv_kernel(*refs, tm=tm, n_heads=n_heads, qscale=qscale)
    else:
        def body(x_ref, wq_ref, wk_ref, wv_ref, wf2_ref, wfh_ref, bf_ref, *outs):
            _qkv_kernel(x_ref, wq_ref, wk_ref, wv_ref, wf2_ref, wfh_ref, bf_ref, None,
                        *outs, None, None, tm=tm, n_heads=n_heads, qscale=qscale)

    return pl.pallas_call(
        body, grid=grid, in_specs=in_specs, out_specs=out_specs, out_shape=out_shape,
        scratch_shapes=scratch, compiler_params=_params(), name="qkv_proj")(*args)


def _conv_kernel(x_ref, wb_ref, wc_ref, wh_ref, wconv_ref, p1_ref, p2_ref,
                 conv_ref, u_ref, ubuf, *, tm, group):
    i = pl.program_id(0)
    xb = x_ref[...].astype(bf16)
    gate_b = _dot_nt(xb, wb_ref[...])
    u = _dot_nt(xb, wc_ref[...]) * _dot_nt(xb, wh_ref[...])
    w0 = wconv_ref[0:1, :]
    w1 = wconv_ref[1:2, :]
    w2 = wconv_ref[2:3, :]
    if group is None:
        @pl.when(i == 0)
        def _():
            ubuf[0:8, :] = jnp.zeros((8, ubuf.shape[1]), f32)

        ubuf[8:8 + tm, :] = u
        u1 = ubuf[pl.ds(7, tm), :]
        u2 = ubuf[pl.ds(6, tm), :]
        ubuf[0:8, :] = u[tm - 8:tm, :]
        u_ref[...] = u[tm - 8:tm, :]
    else:
        ubuf[0:8, :] = jnp.zeros((8, ubuf.shape[1]), f32)
        ubuf[8:8 + tm, :] = u
        t_in = lax.broadcasted_iota(jnp.int32, (tm, 1), 0) % group
        u1 = jnp.where(t_in >= 1, ubuf[pl.ds(7, tm), :], p1_ref[...])
        u2 = jnp.where(t_in >= 2, ubuf[pl.ds(6, tm), :], p2_ref[...])
        u_ref[...] = u
    y = (u2 * w0 + u1 * w1) + u * w2
    conv_ref[...] = (gate_b * y).astype(bf16)


def _conv_proj(x, wb, wc, wh, wconv, p1=None, p2=None, *, group=None):
    t, d = x.shape
    c = wb.shape[0]
    tm = min(TM_IN, t)
    grid = (t // tm,)
    row_blk = lambda w: pl.BlockSpec((tm, w), lambda i: (i, 0))
    in_specs = [row_blk(d), _resident(wb.shape), _resident(wc.shape), _resident(wh.shape),
                _resident(wconv.shape)]
    args = [x, wb, wc, wh, wconv]
    if group is None:
        u_shape = jax.ShapeDtypeStruct((8, c), f32)
        u_spec = pl.BlockSpec((8, c), lambda i: (0, 0))

        def body(x_ref, wb_ref, wc_ref, wh_ref, wconv_ref, conv_ref, u_ref, ubuf):
            _conv_kernel(x_ref, wb_ref, wc_ref, wh_ref, wconv_ref, None, None,
                         conv_ref, u_ref, ubuf, tm=tm, group=None)
    else:
        in_specs += [row_blk(c), row_blk(c)]
        args += [p1, p2]
        u_shape = jax.ShapeDtypeStruct((t, c), f32)
        u_spec = row_blk(c)

        def body(*refs):
            _conv_kernel(*refs, tm=tm, group=group)

    return pl.pallas_call(
        body, grid=grid, in_specs=in_specs,
        out_specs=[row_blk(c), u_spec],
        out_shape=[jax.ShapeDtypeStruct((t, c), bf16), u_shape],
        scratch_shapes=[pltpu.VMEM((tm + 8, c), f32)],
        compiler_params=_params(), name="conv_proj")(*args)


def _prompt_attn_kernel(q_ref, k_ref, kx_ref, v_ref, o_ref, *scratch, tq, n_sub):
    m_scr, acc_scr = scratch[:n_sub], scratch[n_sub:2 * n_sub]
    s_a, s_b = scratch[2 * n_sub:3 * n_sub], scratch[3 * n_sub:]
    i = pl.program_id(1)
    ts = tq // n_sub
    ones_q = jnp.ones((ts, LANES), bf16)
    qa = [jnp.concatenate([q_ref[u * ts:(u + 1) * ts, :], ones_q], axis=-1) for u in range(n_sub)]
    ones_v = jnp.ones((tq, LANES), bf16)
    for u in range(n_sub):
        m_scr[u][...] = jnp.full(m_scr[u].shape, NEG, f32)
        acc_scr[u][...] = jnp.zeros(acc_scr[u].shape, f32)

    def scores(j, s_buf):
        ks = pl.multiple_of(j * tq, tq)
        ka = jnp.concatenate([k_ref[pl.ds(ks, tq), :], kx_ref[pl.ds(ks, tq), :]], axis=-1)
        for u in range(n_sub):
            s_buf[u][...] = _dot_nt(qa[u], ka)

    def accumulate(j, s_buf, masked):
        ks = pl.multiple_of(j * tq, tq)
        va = jnp.concatenate([v_ref[pl.ds(ks, tq), :], ones_v], axis=-1)
        for u in range(n_sub):
            s = s_buf[u][...]
            if masked:
                row = lax.broadcasted_iota(jnp.int32, (ts, tq), 0) + u * ts
                col = lax.broadcasted_iota(jnp.int32, (ts, tq), 1)
                s = jnp.where(col <= row, s, NEG)
            m_prev = m_scr[u][...]
            m_new = jnp.maximum(m_prev, jnp.max(s, axis=1, keepdims=True))
            p = jnp.exp2(s - _lane_tile(m_new, tq // LANES))
            alpha = jnp.exp2(m_prev - m_new)
            acc_scr[u][...] = acc_scr[u][...] * _lane_tile(alpha, 2) + _dot(p.astype(bf16), va)
            m_scr[u][...] = m_new

    def run_blocks(first, n, *, prefetch_next, mask_last):
        bufs = (s_a, s_b)
        for b in range(n):
            if b + 1 < n or prefetch_next:
                scores(first + b + 1, bufs[(b + 1) % 2])
            accumulate(first + b, bufs[b % 2], mask_last and b == n - 1)

    scores(0, s_a)

    def body(jj, c):
        run_blocks(LOOP_BLOCKS * jj, LOOP_BLOCKS, prefetch_next=True, mask_last=False)
        return c

    lax.fori_loop(0, i // LOOP_BLOCKS, body, 0)

    for rest in range(LOOP_BLOCKS):
        @pl.when(i % LOOP_BLOCKS == rest)
        def _():
            run_blocks(i - rest, rest + 1, prefetch_next=False, mask_last=True)

    for u in range(n_sub):
        acc = acc_scr[u][...]
        o_ref[u * ts:(u + 1) * ts, :] = (acc[:, :LANES] / acc[:, LANES:]).astype(o_ref.dtype)


def _prompt_attn(q, k, kx, v, *, n_heads):
    s = q.shape[0]
    tq = min(TQ, s)
    grid = (n_heads, s // tq)
    q_blk = pl.BlockSpec((tq, HEAD_DIM), lambda h, i: (i, h))
    kv_blk = pl.BlockSpec((s, HEAD_DIM), lambda h, i: (0, h))
    return pl.pallas_call(
        functools.partial(_prompt_attn_kernel, tq=tq, n_sub=N_SUB),
        grid=grid, in_specs=[q_blk, kv_blk, kv_blk, kv_blk], out_specs=q_blk,
        out_shape=jax.ShapeDtypeStruct(q.shape, bf16),
        scratch_shapes=([pltpu.VMEM((tq // N_SUB, LANES), f32)] * N_SUB
                        + [pltpu.VMEM((tq // N_SUB, 2 * LANES), f32)] * N_SUB
                        + [pltpu.VMEM((tq // N_SUB, tq), f32)] * (2 * N_SUB)),
        compiler_params=_params(("arbitrary", "arbitrary")), name="prompt_attn")(q, k, kx, v)


def _sample_attn_kernel(pt_ref, q_ref, knew_hbm, vnew_hbm, lnew_hbm, ck_hbm, cv_hbm, cl_hbm,
                        o_ref, kbuf, vbuf, lbuf, bias_scr, sbuf, sem,
                        *, n_pages, n_batch, n_heads, n_new, page):
    b = pl.program_id(0)
    slot = b % 2
    npg = n_pages + 1
    rows = page * n_heads
    new_rows = n_new * n_heads

    def copies(bb, sl):
        cps = []
        for p in range(n_pages):
            pg = pt_ref[bb * n_pages + p]
            src = pl.ds(pl.multiple_of(pg * rows, rows), rows)
            dst = pl.ds(p * rows, rows)
            cps.append(pltpu.make_async_copy(ck_hbm.at[src], kbuf.at[sl, dst], sem.at[sl, 0]))
            cps.append(pltpu.make_async_copy(cv_hbm.at[src], vbuf.at[sl, dst], sem.at[sl, 1]))
            cps.append(pltpu.make_async_copy(cl_hbm.at[pg], lbuf.at[sl, p], sem.at[sl, 2]))
        src = pl.ds(pl.multiple_of(bb * new_rows, new_rows), new_rows)
        dst = pl.ds(n_pages * rows, new_rows)
        cps.append(pltpu.make_async_copy(knew_hbm.at[src], kbuf.at[sl, dst], sem.at[sl, 0]))
        cps.append(pltpu.make_async_copy(vnew_hbm.at[src], vbuf.at[sl, dst], sem.at[sl, 1]))
        cps.append(pltpu.make_async_copy(lnew_hbm.at[bb], lbuf.at[sl, n_pages], sem.at[sl, 2]))
        return cps

    @pl.when(b == 0)
    def _():
        for sl in range(2):
            kbuf[sl, pl.ds(n_pages * rows, rows), :] = jnp.zeros((rows, HEAD_DIM), f32)
            vbuf[sl, pl.ds(n_pages * rows, rows), :] = jnp.zeros((rows, HEAD_DIM), f32)
        for cp in copies(0, 0):
            cp.start()

    @pl.when(b + 1 < n_batch)
    def _():
        for cp in copies(b + 1, 1 - slot):
            cp.start()

    kv_rows = pl.ds(0, n_pages * rows + new_rows)
    for j, region in enumerate((kbuf.at[slot, kv_rows], vbuf.at[slot, kv_rows], lbuf.at[slot])):
        pltpu.make_async_copy(region, region, sem.at[slot, j]).wait()

    row = lax.broadcasted_iota(jnp.int32, (page, page), 0)
    col = lax.broadcasted_iota(jnp.int32, (page, page), 1)
    after = jnp.where(row > col, 1.0, 0.0).astype(bf16)
    lf_all = lbuf[slot].reshape(npg * n_heads, page)
    hi, mid, lo = _split3(lf_all)
    suf = (_dot(hi, after) + _dot(mid, after)) + _dot(lo, after)
    tot = jnp.sum(lf_all, axis=1, keepdims=True)
    carry = jnp.zeros((n_heads, 1), f32)
    for p in range(npg - 1, -1, -1):
        rs = slice(p * n_heads, (p + 1) * n_heads)
        bias_scr[p] = (suf[rs, :] + carry) * LOG2E
        carry = carry + tot[rs, :]

    qb = q_ref[0].astype(bf16)
    qh = [qb[:, h * HEAD_DIM:(h + 1) * HEAD_DIM] for h in range(n_heads)]
    t_row = lax.broadcasted_iota(jnp.int32, (8, page), 0)
    lane = lax.broadcasted_iota(jnp.int32, (8, page), 1)

    def head_rows(buf, p, h):
        return buf[slot, pl.ds(p * rows + h, page, stride=n_heads), :].astype(bf16)

    def score_page(p, mvec, masked):
        out = []
        for h in range(n_heads):
            s = _dot_nt(qh[h], head_rows(kbuf, p, h)) + bias_scr[p, pl.ds(h, 1), :]
            if masked:
                s = jnp.where(lane <= t_row, s, NEG)
            sbuf[p, h] = s
            out.append(jnp.maximum(mvec[h], s))
        return tuple(out)

    unroll = math.gcd(n_pages, PAGE_UNROLL)
    mvec = lax.fori_loop(0, n_pages, functools.partial(score_page, masked=False),
                         tuple(jnp.full((8, page), NEG, f32) for _ in range(n_heads)),
                         unroll=unroll)
    mvec = score_page(n_pages, mvec, True)
    m = [jnp.max(mv, axis=1, keepdims=True) for mv in mvec]

    def value_page(p, carry):
        lvec, acc = carry
        lout, aout = [], []
        for h in range(n_heads):
            pr = jnp.exp2(sbuf[p, h] - m[h])
            lout.append(lvec[h] + pr)
            aout.append(acc[h] + _dot(pr.astype(bf16), head_rows(vbuf, p, h)))
        return tuple(lout), tuple(aout)

    zeros = tuple(jnp.zeros((8, page), f32) for _ in range(n_heads))
    lvec, acc = lax.fori_loop(0, n_pages, value_page, (zeros, zeros), unroll=unroll)
    lvec, acc = value_page(n_pages, (lvec, acc))
    for h in range(n_heads):
        l = jnp.sum(lvec[h], axis=1, keepdims=True)
        o_ref[0, :, h * HEAD_DIM:(h + 1) * HEAD_DIM] = acc[h] / l


def _sample_attn(page_table, q8, knew, vnew, lnew, ck, cv, cl, *, n_heads, n_new):
    n_batch, n_pages = page_table.shape
    page = cl.shape[-1]
    assert page == HEAD_DIM == LANES
    a = q8.shape[-1]
    rows = (n_pages + 1) * page * n_heads
    any_spec = pl.BlockSpec(memory_space=pl.ANY)
    q_blk = pl.BlockSpec((1, 8, a), lambda b, pt: (b, 0, 0))
    grid_spec = pltpu.PrefetchScalarGridSpec(
        num_scalar_prefetch=1, grid=(n_batch,),
        in_specs=[q_blk, any_spec, any_spec, any_spec, any_spec, any_spec, any_spec],
        out_specs=q_blk,
        scratch_shapes=[
            pltpu.VMEM((2, rows, HEAD_DIM), f32),
            pltpu.VMEM((2, rows, HEAD_DIM), f32),
            pltpu.VMEM((2, n_pages + 1, n_heads, page), f32),
            pltpu.VMEM((n_pages + 1, n_heads, page), f32),
            pltpu.VMEM((n_pages + 1, n_heads, 8, page), f32),
            pltpu.SemaphoreType.DMA((2, 3)),
        ])
    return pl.pallas_call(
        functools.partial(_sample_attn_kernel, n_pages=n_pages, n_batch=n_batch,
                          n_heads=n_heads, n_new=n_new, page=page),
        grid_spec=grid_spec, out_shape=jax.ShapeDtypeStruct(q8.shape, f32),
        compiler_params=_params(), name="sample_attn")(
            page_table.reshape(-1), q8, knew, vnew, lnew, ck, cv, cl)


def _layer_norm(y, g, b):
    mu = jnp.mean(y, axis=-1, keepdims=True)
    yc = y - mu
    var = jnp.mean(yc * yc, axis=-1, keepdims=True)
    return yc * lax.rsqrt(var + LN_EPS) * g + b


def _finish1_kernel(ap_ref, cp_ref, xp_ref, as_ref, cs_ref, xs_ref,
                    woa_ref, woc_ref, g_ref, b_ref, wr2_ref, wrh_ref, br_ref,
                    x1_ref, x1g_ref, meta_ref, cnt_ref, gw_ref, cnt_scr, *, alpha, nb_prompt, tm):
    is_prompt = pl.program_id(0) < nb_prompt
    attn = jnp.where(is_prompt, ap_ref[...], as_ref[...])
    conv = jnp.where(is_prompt, cp_ref[...], cs_ref[...])
    x = jnp.where(is_prompt, xp_ref[...], xs_ref[...])
    o = _dot(attn, woa_ref[...]) + _dot(conv, woc_ref[...])
    x1 = _layer_norm(alpha * x + o, g_ref[...], b_ref[...])
    x1_ref[...] = x1
    n_chunks = x1.shape[1] // LANES
    for c in range(n_chunks):
        x1g_ref[pl.ds(c, tm, stride=n_chunks), :] = x1[:, c * LANES:(c + 1) * LANES]

    x1b = x1.astype(bf16)
    x1l = (x1 - x1b.astype(f32)).astype(bf16)
    r2 = _dot(x1b, wr2_ref[...])
    lr = r2[:, :LANES] + r2[:, LANES:] + _dot(x1l, wrh_ref[...]) + br_ref[...]
    lane = lax.broadcasted_iota(jnp.int32, lr.shape, 1)

    def first_max(mask):
        vals = jnp.where(mask, lr, -jnp.inf)
        mx = jnp.max(vals, axis=-1, keepdims=True)
        idx = jnp.min(jnp.where(mask & (vals == mx), lane, LANES), axis=-1, keepdims=True)
        return mx, idx

    is_group = lane < N_GROUPS
    gmax, gidx = first_max(is_group)
    p_group = 1.0 / jnp.sum(jnp.where(is_group, jnp.exp(lr - gmax), 0.0), axis=-1, keepdims=True)
    lo = N_GROUPS + gidx * EXPERTS_PER_GROUP
    in_group = (lane >= lo) & (lane < lo + EXPERTS_PER_GROUP)
    e1, i1 = first_max(in_group)
    e2, i2 = first_max(in_group & (lane != i1))
    r = jnp.exp(e2 - e1)
    w1 = p_group / (1.0 + r)
    w2 = p_group * r / (1.0 + r)
    gw_ref[...] = jnp.where(lane == 0, w1, jnp.where(lane == 1, w2, 0.0))

    @pl.when(pl.program_id(0) == 0)
    def _():
        cnt_scr[...] = jnp.zeros(cnt_scr.shape, f32)

    id1, id2 = i1 - N_GROUPS, i2 - N_GROUPS
    sel1 = jnp.where(lane == id1, 1.0, 0.0)
    sel2 = jnp.where(lane == id2, 1.0, 0.0)
    row = lax.broadcasted_iota(jnp.int32, (tm, tm), 0)
    col = lax.broadcasted_iota(jnp.int32, (tm, tm), 1)
    before = jnp.where(col < row, 1.0, 0.0).astype(bf16)
    seen = cnt_scr[0:1, :]
    tot1 = jnp.sum(sel1, axis=0, keepdims=True)
    rank1 = jnp.sum(sel1 * (seen + _dot(before, sel1.astype(bf16))), axis=-1, keepdims=True)
    rank2 = jnp.sum(sel2 * ((seen + tot1) + _dot(before, sel2.astype(bf16))), axis=-1, keepdims=True)
    seen = (seen + tot1) + jnp.sum(sel2, axis=0, keepdims=True)
    cnt_scr[...] = jnp.broadcast_to(seen, cnt_scr.shape)
    cnt_ref[...] = jnp.broadcast_to(seen, cnt_ref.shape)
    meta = jnp.where(lane == 0, id1.astype(f32), jnp.where(lane == 1, id2.astype(f32),
                     jnp.where(lane == 2, rank1, jnp.where(lane == 3, rank2, 0.0))))
    meta_ref[...] = meta.T[0:8, :]


def _finish1(attn_p, conv_p, x_p, attn_s, conv_s, x_s, wo, g, b, wr2, wrh, br, *, alpha):
    (t_p, d), t_s = x_p.shape, x_s.shape[0]
    a = attn_p.shape[1]
    assert wo.shape[0] == 2 * a
    tm = min(TM_PROJ, t_s)
    nb_p, nb_s = t_p // tm, t_s // tm
    t_all = t_p + t_s
    p_blk = lambda w: pl.BlockSpec((tm, w), lambda i: (jnp.minimum(i, nb_p - 1), 0))
    s_blk = lambda w: pl.BlockSpec((tm, w), lambda i: (jnp.maximum(i - nb_p, 0), 0))
    out_blk = lambda w: pl.BlockSpec((tm, w), lambda i: (i, 0))
    in_specs = [p_blk(a), p_blk(a), p_blk(d), s_blk(a), s_blk(a), s_blk(d),
                pl.BlockSpec((a, d), lambda i: (0, 0), pipeline_mode=pl.Buffered(1)),
                pl.BlockSpec((a, d), lambda i: (1, 0), pipeline_mode=pl.Buffered(1)),
                _resident(g.shape), _resident(b.shape), _resident(wr2.shape),
                _resident(wrh.shape), _resident(br.shape)]
    n_chunks = d // LANES
    return pl.pallas_call(
        functools.partial(_finish1_kernel, alpha=alpha, nb_prompt=nb_p, tm=tm),
        grid=(nb_p + nb_s,), in_specs=in_specs,
        out_specs=[out_blk(d), pl.BlockSpec((tm * n_chunks, LANES), lambda i: (i, 0)),
                   pl.BlockSpec((8, tm), lambda i: (0, i)),
                   pl.BlockSpec((8, LANES), lambda i: (0, 0)), out_blk(LANES)],
        out_shape=[jax.ShapeDtypeStruct((t_all, d), f32),
                   jax.ShapeDtypeStruct((t_all * n_chunks, LANES), f32),
                   jax.ShapeDtypeStruct((8, t_all), f32),
                   jax.ShapeDtypeStruct((8, LANES), f32),
                   jax.ShapeDtypeStruct((t_all, LANES), f32)],
        scratch_shapes=[pltpu.VMEM((8, LANES), f32)],
        compiler_params=_params(), name="finish1")(
            attn_p, conv_p, x_p, attn_s, conv_s, x_s, wo, wo, g, b, wr2, wrh, br)


def _dispatch_kernel(lt_ref, nu_ref, pos_ref, x_ref, xs_hbm, zbuf, sem, zsem, *, tm, tm_moe, n_tiles):
    s = pl.program_id(0)

    @pl.when(s == 0)
    def _():
        zbuf[...] = jnp.zeros(zbuf.shape, f32)

        def zero_tile(row0):
            return pltpu.make_async_copy(zbuf, xs_hbm.at[pl.ds(row0, tm_moe)], zsem)

        def each_zero_tile(act):
            for e in range(N_EXPERTS):
                @pl.when(lt_ref[e] >= 0)
                def _():
                    act(zero_tile(lt_ref[e]))

            def unused(t, c):
                act(zero_tile(t * tm_moe))
                return c

            lax.fori_loop(nu_ref[0], n_tiles, unused, 0)

        each_zero_tile(lambda cp: cp.start())
        each_zero_tile(lambda cp: cp.wait())

    def issue(r, c):
        pltpu.make_async_copy(x_ref.at[r], xs_hbm.at[pos_ref[0, 0, r]], sem).start(priority=0)
        pltpu.make_async_copy(x_ref.at[r], xs_hbm.at[pos_ref[0, 0, tm + r]], sem).start(priority=1)
        return c

    lax.fori_loop(0, tm, issue, 0, unroll=8)
    done = xs_hbm.at[pl.ds(0, 2 * tm)]
    pltpu.make_async_copy(done, done, sem).wait()


def _dispatch(x1g, pos, last_tile_row, n_used, *, n_tiles, tm_moe):
    t, n_chunks, _ = x1g.shape
    n_steps, _, two_tm = pos.shape
    tm = two_tm // 2
    grid_spec = pltpu.PrefetchScalarGridSpec(
        num_scalar_prefetch=2, grid=(n_steps,),
        in_specs=[pl.BlockSpec((1, 1, two_tm), lambda s, *_: (s, 0, 0), memory_space=pltpu.SMEM),
                  pl.BlockSpec((tm, n_chunks, LANES), lambda s, *_: (s, 0, 0))],
        out_specs=pl.BlockSpec(memory_space=pl.ANY),
        scratch_shapes=[pltpu.VMEM((tm_moe, n_chunks, LANES), f32),
                        pltpu.SemaphoreType.DMA(()), pltpu.SemaphoreType.DMA(())])
    return pl.pallas_call(
        functools.partial(_dispatch_kernel, tm=tm, tm_moe=tm_moe, n_tiles=n_tiles),
        grid_spec=grid_spec,
        out_shape=jax.ShapeDtypeStruct((n_tiles * tm_moe, n_chunks, LANES), f32),
        compiler_params=_params(), name="dispatch")(last_tile_row, n_used, pos, x1g)


def _moe_kernel(te_ref, ord_ref, nxt_ref, nu_ref, x_ref, wg_hbm, wu_hbm, wd_hbm, y_ref,
                wgf, wuf, wdf, wgb, wub, wdb, wsem, *, tm, n_chunks):
    t = pl.program_id(0)
    n_used = nu_ref[0]

    def weight_copies(e, wslot):
        return (pltpu.make_async_copy(wg_hbm.at[e], wgf.at[wslot], wsem.at[wslot, 0]),
                pltpu.make_async_copy(wu_hbm.at[e], wuf.at[wslot], wsem.at[wslot, 1]),
                pltpu.make_async_copy(wd_hbm.at[e], wdf.at[wslot], wsem.at[wslot, 2]))

    @pl.when(t == 0)
    def _():
        for cp in weight_copies(te_ref[0], 0):
            cp.start(priority=1)

    @pl.when(t < n_used)
    def _():
        e = te_ref[t]
        new_expert = (t == 0) | (e != te_ref[jnp.maximum(t - 1, 0)])

        @pl.when(new_expert)
        def _():
            wslot = ord_ref[t] % 2
            for cp in weight_copies(e, wslot):
                cp.wait()
            nxt = nxt_ref[e]

            @pl.when(nxt >= 0)
            def _():
                for cp in weight_copies(nxt, 1 - wslot):
                    cp.start(priority=1)

            wgb[...] = wgf[wslot].astype(bf16)
            wub[...] = wuf[wslot].astype(bf16)
            wdb[...] = wdf[wslot].astype(bf16)

        xb = jnp.concatenate([x_ref[pl.ds(c, tm, stride=n_chunks), :] for c in range(n_chunks)],
                             axis=-1).astype(bf16)
        hg = _dot(xb, wgb[...])
        hu = _dot(xb, wub[...])
        act = hg * (1.0 / (1.0 + jnp.exp(-hg))) * hu
        y = _dot(act.astype(bf16), wdb[...])
        for c in range(n_chunks):
            y_ref[pl.ds(c, tm, stride=n_chunks), :] = y[:, c * LANES:(c + 1) * LANES]

    @pl.when(t >= n_used)
    def _():
        y_ref[...] = jnp.zeros(y_ref.shape, f32)


def _moe(xs, tile_expert, tile_ord, next_expert, n_used, wg, wu, wd, *, tm):
    _, d, f = wg.shape
    n_chunks = d // LANES
    n_tiles = xs.shape[0] // (tm * n_chunks)
    any_spec = pl.BlockSpec(memory_space=pl.ANY)

    def x_idx(t, te, od, nx, nu):
        return (jnp.minimum(t, jnp.maximum(nu[0] - 1, 0)), 0)

    grid_spec = pltpu.PrefetchScalarGridSpec(
        num_scalar_prefetch=4, grid=(n_tiles,),
        in_specs=[pl.BlockSpec((tm * n_chunks, LANES), x_idx), any_spec, any_spec, any_spec],
        out_specs=pl.BlockSpec((tm * n_chunks, LANES), lambda t, *_: (t, 0)),
        scratch_shapes=[pltpu.VMEM((2, d, f), f32), pltpu.VMEM((2, d, f), f32),
                        pltpu.VMEM((2, f, d), f32),
                        pltpu.VMEM((d, f), bf16), pltpu.VMEM((d, f), bf16), pltpu.VMEM((f, d), bf16),
                        pltpu.SemaphoreType.DMA((2, 3))])
    return pl.pallas_call(
        functools.partial(_moe_kernel, tm=tm, n_chunks=n_chunks),
        grid_spec=grid_spec, out_shape=jax.ShapeDtypeStruct(xs.shape, f32),
        compiler_params=_params(), name="moe")(
            tile_expert, tile_ord, next_expert, n_used, xs, wg, wu, wd)


def _finish2_kernel(idx_ref, ys_hbm, x1_ref, gw_ref, g_ref, b_ref, o_ref, gbuf, sem,
                    *, tm, n_steps, alpha):
    s = pl.program_id(0)
    n_chunks = ys_hbm.shape[1]
    pitch = n_chunks + ROW_PITCH_PAD

    @pl.when(s < n_steps)
    def _():
        slot = s % 2

        def issue(r, c):
            for k in range(2):
                rr = k * tm + r
                pltpu.make_async_copy(ys_hbm.at[idx_ref[0, 0, rr]],
                                      gbuf.at[slot, pl.ds(rr * pitch, n_chunks)],
                                      sem.at[slot]).start(priority=k)
            return c

        lax.fori_loop(0, tm, issue, 0, unroll=4)

    @pl.when(s >= 1)
    def _():
        slot = (s - 1) % 2
        landed = gbuf.at[slot, pl.ds(0, 2 * tm * n_chunks)]
        pltpu.make_async_copy(landed, landed, sem.at[slot]).wait()

        def rows(first):
            return jnp.concatenate(
                [gbuf[slot, pl.ds(first * pitch + c, tm, stride=pitch), :] for c in range(n_chunks)],
                axis=-1)

        gw = gw_ref[...]
        f = gw[:, 0:1] * rows(0) + gw[:, 1:2] * rows(tm)
        o_ref[...] = _layer_norm(alpha * x1_ref[...] + f, g_ref[...], b_ref[...])


def _finish2(ys, pos, x1, gw, g, b, *, alpha, row0, t):
    n_steps, _, two_tm = pos.shape
    tm = two_tm // 2
    d = x1.shape[1]
    n_chunks = d // LANES
    blk0 = row0 // tm
    last = n_steps - 1
    in_blk = lambda w: pl.BlockSpec((tm, w), lambda s: (jnp.maximum(s - 1, 0) + blk0, 0))
    return pl.pallas_call(
        functools.partial(_finish2_kernel, tm=tm, n_steps=n_steps, alpha=alpha),
        grid=(n_steps + 1,),
        in_specs=[pl.BlockSpec((1, 1, two_tm), lambda s: (jnp.minimum(s, last), 0, 0),
                               memory_space=pltpu.SMEM),
                  pl.BlockSpec(memory_space=pl.ANY),
                  in_blk(d), in_blk(LANES), _resident(g.shape), _resident(b.shape)],
        out_specs=pl.BlockSpec((tm, d), lambda s: (jnp.maximum(s - 1, 0), 0)),
        out_shape=jax.ShapeDtypeStruct((t, d), f32),
        scratch_shapes=[pltpu.VMEM((2, two_tm * (n_chunks + ROW_PITCH_PAD), LANES), f32),
                        pltpu.SemaphoreType.DMA((2,))],
        compiler_params=_params(), name="finish2")(pos, ys, x1, gw, g, b)


def _bias_selectors(n_heads, sign):
    sel = np.zeros((3, n_heads, n_heads * HEAD_DIM), np.float32)
    for p in range(3):
        for h in range(n_heads):
            sel[p, h, h * HEAD_DIM + p] = sign
    return sel


def _route_tables(meta, counts, tm):
    n_tok = meta.shape[1]
    e = meta[0:2].astype(jnp.int32)
    rank = meta[2:4].astype(jnp.int32)
    counts = counts[0, :N_EXPERTS].astype(jnp.int32)
    n_pairs = 2 * n_tok
    padded = ((counts + tm - 1) // tm) * tm
    ends = jnp.cumsum(padded)
    experts = jnp.arange(N_EXPERTS, dtype=jnp.int32)
    first_row = jnp.sum(jnp.where(e[..., None] == experts, ends - padded, 0), axis=-1)
    pos = first_row + rank
    n_tiles = (n_pairs + N_EXPERTS * (tm - 1)) // tm
    last_tile_row = jnp.where(counts > 0, ends - tm, -1).astype(jnp.int32)
    tile_start = jnp.arange(n_tiles, dtype=jnp.int32) * tm
    tile_expert = jnp.minimum(
        jnp.sum((ends[None, :] <= tile_start[:, None]).astype(jnp.int32), axis=1), N_EXPERTS - 1)
    n_used = (ends[-1] // tm).astype(jnp.int32).reshape(1)
    nonempty = counts > 0
    expert_ord = jnp.cumsum(nonempty.astype(jnp.int32)) - 1
    later = nonempty[None, :] & (experts[None, :] > experts[:, None])
    next_expert = jnp.min(jnp.where(later, experts[None, :], N_EXPERTS), axis=1)
    next_expert = jnp.where(next_expert < N_EXPERTS, next_expert, -1).astype(jnp.int32)
    tile_ord = expert_ord[tile_expert].astype(jnp.int32)
    return pos, last_tile_row, n_tiles, tile_expert, tile_ord, next_expert, n_used


def _tile_pos(pos, tm):
    t = pos.shape[1]
    return jnp.concatenate([pos[0].reshape(t // tm, 1, tm), pos[1].reshape(t // tm, 1, tm)], axis=2)


def kernel(x_prompt, x_sample, cache_k, cache_v, cache_logf, state_conv, page_table, w_in, b_f,
           w_conv, w_out, ln1_g, ln1_b, w_rg, b_rg, w_re, b_re, w_gate, w_up, w_down, ln2_g, ln2_b):
    depth, d_model, _ = w_in.shape
    n_batch_p, seq, _ = x_prompt.shape
    db, dseq, _ = x_sample.shape
    n_heads = cache_k.shape[3]
    a = n_heads * HEAD_DIM
    c = w_conv.shape[-1]
    assert n_batch_p == 1 and dseq <= 8 and w_conv.shape[1] == CONV_K
    alpha = (2.0 * depth) ** 0.25
    qscale = HEAD_DIM ** -0.5 * LOG2E
    t_p, t_s = seq, db * dseq
    t_all = t_p + t_s

    sel_neg = _bias_selectors(n_heads, -1.0)
    psel_k = jnp.asarray(np.concatenate(
        [np.pad(sel_neg[p], ((0, LANES - n_heads), (0, 0))) for p in range(3)], axis=0), bf16)

    xp = x_prompt.reshape(t_p, d_model)
    xs = x_sample.reshape(t_s, d_model)
    outs = {k: [] for k in ("kp", "vp", "fp", "cp", "ks", "vs", "fs", "cs")}
    for l in range(depth):
        wl = jnp.swapaxes(w_in[l], 0, 1)
        wq, wk, wv = (wl[j * a:(j + 1) * a].astype(bf16) for j in range(3))
        o = 3 * a
        wf = jnp.pad(wl[o:o + n_heads], ((0, LANES - n_heads), (0, 0)))
        wfh = wf.astype(bf16)
        wf2 = jnp.concatenate([wfh, (wf - wfh.astype(f32)).astype(bf16)], axis=0)
        bfp = jnp.pad(b_f[l], (0, LANES - n_heads)).reshape(1, LANES)
        o += n_heads
        wb, wc, wh = (wl[o + j * c:o + (j + 1) * c].astype(bf16) for j in range(3))
        wo = w_out[l].astype(bf16)
        wr = jnp.pad(jnp.concatenate([w_rg[l], w_re[l]], axis=1),
                     ((0, 0), (0, LANES - N_GROUPS - N_EXPERTS)))
        wrh = wr.astype(bf16)
        wr2 = jnp.concatenate([wrh, (wr - wrh.astype(f32)).astype(bf16)], axis=1)
        br = jnp.pad(jnp.concatenate([b_rg[l], b_re[l]]), (0, LANES - N_GROUPS - N_EXPERTS)).reshape(1, LANES)
        g1, b1 = ln1_g[l].reshape(1, -1), ln1_b[l].reshape(1, -1)
        g2, b2 = ln2_g[l].reshape(1, -1), ln2_b[l].reshape(1, -1)

        q, ko, vo, kb, vb, lf, kx = _qkv_proj(xp, wq, wk, wv, wf2, wfh, bfp, psel_k,
                                              n_heads=n_heads, qscale=qscale, with_cum=True)
        conv, utail = _conv_proj(xp, wb, wc, wh, w_conv[l])
        attn = _prompt_attn(q, kb, kx, vb, n_heads=n_heads)
        outs["kp"].append(ko.reshape(1, t_p, n_heads, HEAD_DIM))
        outs["vp"].append(vo.reshape(1, t_p, n_heads, HEAD_DIM))
        outs["fp"].append(lf[:, :n_heads].reshape(1, t_p, n_heads))
        outs["cp"].append(utail[8 - (CONV_K - 1):].reshape(1, CONV_K - 1, c))

        qs, kos, vos, _, _, lfs = _qkv_proj(xs, wq, wk, wv, wf2, wfh, bfp, None,
                                            n_heads=n_heads, qscale=qscale, with_cum=False)
        st = state_conv[l]
        zero = jnp.zeros((db, dseq, c), f32)
        p1 = zero.at[:, 0].set(st[:, 1]).reshape(t_s, c)
        p2 = zero.at[:, 0].set(st[:, 0]).at[:, 1].set(st[:, 1]).reshape(t_s, c)
        conv_s, u_s = _conv_proj(xs, wb, wc, wh, w_conv[l], p1, p2, group=dseq)
        q8 = jnp.pad(qs.reshape(db, dseq, a), ((0, 0), (0, 8 - dseq), (0, 0))).astype(f32)
        page = cache_k.shape[2]
        lnew = jnp.pad(jnp.swapaxes(lfs[:, :n_heads].reshape(db, dseq, n_heads), 1, 2),
                       ((0, 0), (0, 0), (0, page - dseq)))
        attn_s = _sample_attn(page_table, q8, kos, vos, lnew,
                              cache_k[l].reshape(-1, HEAD_DIM), cache_v[l].reshape(-1, HEAD_DIM),
                              jnp.swapaxes(cache_logf[l], 1, 2), n_heads=n_heads, n_new=dseq)
        attn_s = attn_s[:, :dseq].reshape(t_s, a).astype(bf16)
        outs["ks"].append(kos.reshape(db, dseq, n_heads, HEAD_DIM))
        outs["vs"].append(vos.reshape(db, dseq, n_heads, HEAD_DIM))
        outs["fs"].append(lfs[:, :n_heads].reshape(db, dseq, n_heads))
        outs["cs"].append(u_s.reshape(db, dseq, c)[:, dseq - (CONV_K - 1):])
        x1, x1g, meta, counts, gw = _finish1(attn, conv, xp, attn_s, conv_s, xs, wo, g1, b1,
                                             wr2, wrh, br, alpha=alpha)

        pos, last_tile_row, n_tiles, tile_expert, tile_ord, next_expert, n_used = _route_tables(
            meta, counts, TM_MOE)
        n_chunks = d_model // LANES
        tm2 = min(TM_PROJ, t_s)
        xsort = _dispatch(x1g.reshape(t_all, n_chunks, LANES), _tile_pos(pos, tm2), last_tile_row,
                          n_used, n_tiles=n_tiles, tm_moe=TM_MOE)
        ys = _moe(xsort.reshape(-1, LANES), tile_expert, tile_ord, next_expert, n_used,
                  w_gate[l], w_up[l], w_down[l], tm=TM_MOE)
        ys = ys.reshape(-1, n_chunks, LANES)
        xp = _finish2(ys, _tile_pos(pos[:, :t_p], TM_PROJ), x1, gw, g2, b2, alpha=alpha, row0=0, t=t_p)
        xs = _finish2(ys, _tile_pos(pos[:, t_p:], tm2), x1, gw, g2, b2, alpha=alpha, row0=t_p, t=t_s)

    stack = lambda k: jnp.stack(outs[k])
    return (xp.reshape(x_prompt.shape), xs.reshape(x_sample.shape),
            stack("kp"), stack("vp"), stack("fp"), stack("cp"),
            stack("ks"), stack("vs"), stack("fs"), stack("cs"))
```

```python
import functools
import math

import numpy as np
import jax
import jax.numpy as jnp
from jax import lax
from jax.experimental import pallas as pl
from jax.experimental.pallas import tpu as pltpu

f32 = jnp.float32
bf16 = jnp.bfloat16

LANES = 128
HEAD_DIM = 128
LOG2E = 1.4426950408889634
LN_EPS = 1e-5
NEG = -1e30
CONV_K = 3
N_GROUPS = 4
EXPERTS_PER_GROUP = 8
N_EXPERTS = N_GROUPS * EXPERTS_PER_GROUP
VMEM_LIMIT = 56 * 1024 * 1024

TM_IN = 512
TM_PROJ = 256
TQ = 1024
N_SUB = 4
LOOP_BLOCKS = 2
PAGE_UNROLL = 16
TM_MOE = 256
ROW_PITCH_PAD = 1


def _dot(a, b):
    return jnp.dot(a, b, preferred_element_type=f32)


def _dot_nt(a, b):
    return lax.dot_general(a, b, (((1,), (1,)), ((), ())), preferred_element_type=f32)


def _lane_tile(x, n):
    return jnp.concatenate([x] * n, axis=-1)


def _split3(x):
    hi = x.astype(bf16)
    r1 = x - hi.astype(f32)
    mid = r1.astype(bf16)
    lo = (r1 - mid.astype(f32)).astype(bf16)
    return hi, mid, lo


def _params(sem=("arbitrary",)):
    return pltpu.CompilerParams(dimension_semantics=sem, vmem_limit_bytes=VMEM_LIMIT)


def _resident(shape):
    nd = len(shape)
    return pl.BlockSpec(shape, lambda *_: (0,) * nd, pipeline_mode=pl.Buffered(1))


def _qkv_kernel(x_ref, wq_ref, wk_ref, wv_ref, wf2_ref, wfh_ref, bf_ref, psel_ref,
                q_ref, ko_ref, vo_ref, kb_ref, vb_ref, lf_ref, kx_ref, carry_ref,
                *, tm, n_heads, qscale):
    i = pl.program_id(0)
    x = x_ref[...]
    xb = x.astype(bf16)
    q_ref[...] = (_dot_nt(xb, wq_ref[...]) * qscale).astype(bf16)
    k = _dot_nt(xb, wk_ref[...])
    kb_ref[...] = k.astype(bf16)
    v = _dot_nt(xb, wv_ref[...])
    vb_ref[...] = v.astype(bf16)
    for h in range(n_heads):
        ko_ref[pl.ds(h, tm, stride=n_heads), :] = k[:, h * HEAD_DIM:(h + 1) * HEAD_DIM]
        vo_ref[pl.ds(h, tm, stride=n_heads), :] = v[:, h * HEAD_DIM:(h + 1) * HEAD_DIM]

    xl = (x - xb.astype(f32)).astype(bf16)
    zf2 = _dot_nt(xb, wf2_ref[...])
    zf = zf2[:, :LANES] + zf2[:, LANES:] + _dot_nt(xl, wfh_ref[...]) + bf_ref[...]
    lf = jnp.minimum(zf, 0.0) - jnp.log1p(jnp.exp(-jnp.abs(zf)))
    lf_ref[...] = lf

    if kx_ref is not None:
        @pl.when(i == 0)
        def _():
            carry_ref[...] = jnp.zeros(carry_ref.shape, f32)

        row = lax.broadcasted_iota(jnp.int32, (tm, tm), 0)
        col = lax.broadcasted_iota(jnp.int32, (tm, tm), 1)
        tri = jnp.where(row >= col, 1.0, 0.0).astype(bf16)
        c3 = _dot(tri, jnp.concatenate(_split3(lf), axis=-1))
        cum = (c3[:, :LANES] + c3[:, LANES:2 * LANES] + c3[:, 2 * LANES:]) + carry_ref[0:1, :]
        carry_ref[...] = jnp.broadcast_to(cum[tm - 1:tm, :], carry_ref.shape)
        parts = jnp.concatenate(_split3(cum * LOG2E), axis=-1)
        kx_ref[...] = _dot(parts, psel_ref[...]).astype(bf16)


def _qkv_proj(x, wq, wk, wv, wf2, wfh, bfp, psel, *, n_heads, qscale, with_cum):
    t, d = x.shape
    a = wq.shape[0]
    tm = min(TM_IN, t)
    grid = (t // tm,)
    row_blk = lambda w: pl.BlockSpec((tm, w), lambda i: (i, 0))
    in_specs = [row_blk(d), _resident(wq.shape), _resident(wk.shape), _resident(wv.shape),
                _resident(wf2.shape), _resident(wfh.shape), _resident(bfp.shape)]
    args = [x, wq, wk, wv, wf2, wfh, bfp]
    out_shape = [jax.ShapeDtypeStruct((t, a), bf16),
                 jax.ShapeDtypeStruct((t * n_heads, HEAD_DIM), f32),
                 jax.ShapeDtypeStruct((t * n_heads, HEAD_DIM), f32),
                 jax.ShapeDtypeStruct((t, a), bf16),
                 jax.ShapeDtypeStruct((t, a), bf16),
                 jax.ShapeDtypeStruct((t, LANES), f32)]
    hd_blk = pl.BlockSpec((tm * n_heads, HEAD_DIM), lambda i: (i, 0))
    out_specs = [row_blk(a), hd_blk, hd_blk, row_blk(a), row_blk(a), row_blk(LANES)]
    scratch = []
    if with_cum:
        in_specs.append(_resident(psel.shape))
        args.append(psel)
        out_shape.append(jax.ShapeDtypeStruct((t, a), bf16))
        out_specs.append(row_blk(a))
        scratch.append(pltpu.VMEM((8, LANES), f32))

        def body(*refs):
            _qkv_kernel(*refs, tm=tm, n_heads=n_heads, qscale=qscale)
    else:
        def body(x_ref, wq_ref, wk_ref, wv_ref, wf2_ref, wfh_ref, bf_ref, *outs):
            _qkv_kernel(x_ref, wq_ref, wk_ref, wv_ref, wf2_ref, wfh_ref, bf_ref, None,
                        *outs, None, None, tm=tm, n_heads=n_heads, qscale=qscale)

    return pl.pallas_call(
        body, grid=grid, in_specs=in_specs, out_specs=out_specs, out_shape=out_shape,
        scratch_shapes=scratch, compiler_params=_params(), name="qkv_proj")(*args)


def _conv_kernel(x_ref, wb_ref, wc_ref, wh_ref, wconv_ref, p1_ref, p2_ref,
                 conv_ref, u_ref, ubuf, *, tm, group):
    i = pl.program_id(0)
    xb = x_ref[...].astype(bf16)
    gate_b = _dot_nt(xb, wb_ref[...])
    u = _dot_nt(xb, wc_ref[...]) * _dot_nt(xb, wh_ref[...])
    w0 = wconv_ref[0:1, :]
    w1 = wconv_ref[1:2, :]
    w2 = wconv_ref[2:3, :]
    if group is None:
        @pl.when(i == 0)
        def _():
            ubuf[0:8, :] = jnp.zeros((8, ubuf.shape[1]), f32)

        ubuf[8:8 + tm, :] = u
        u1 = ubuf[pl.ds(7, tm), :]
        u2 = ubuf[pl.ds(6, tm), :]
        ubuf[0:8, :] = u[tm - 8:tm, :]
        u_ref[...] = u[tm - 8:tm, :]
    else:
        ubuf[0:8, :] = jnp.zeros((8, ubuf.shape[1]), f32)
        ubuf[8:8 + tm, :] = u
        t_in = lax.broadcasted_iota(jnp.int32, (tm, 1), 0) % group
        u1 = jnp.where(t_in >= 1, ubuf[pl.ds(7, tm), :], p1_ref[...])
        u2 = jnp.where(t_in >= 2, ubuf[pl.ds(6, tm), :], p2_ref[...])
        u_ref[...] = u
    y = (u2 * w0 + u1 * w1) + u * w2
    conv_ref[...] = (gate_b * y).astype(bf16)


def _conv_proj(x, wb, wc, wh, wconv, p1=None, p2=None, *, group=None):
    t, d = x.shape
    c = wb.shape[0]
    tm = min(TM_IN, t)
    grid = (t // tm,)
    row_blk = lambda w: pl.BlockSpec((tm, w), lambda i: (i, 0))
    in_specs = [row_blk(d), _resident(wb.shape), _resident(wc.shape), _resident(wh.shape),
                _resident(wconv.shape)]
    args = [x, wb, wc, wh, wconv]
    if group is None:
        u_shape = jax.ShapeDtypeStruct((8, c), f32)
        u_spec = pl.BlockSpec((8, c), lambda i: (0, 0))

        def body(x_ref, wb_ref, wc_ref, wh_ref, wconv_ref, conv_ref, u_ref, ubuf):
            _conv_kernel(x_ref, wb_ref, wc_ref, wh_ref, wconv_ref, None, None,
                         conv_ref, u_ref, ubuf, tm=tm, group=None)
    else:
        in_specs += [row_blk(c), row_blk(c)]
        args += [p1, p2]
        u_shape = jax.ShapeDtypeStruct((t, c), f32)
        u_spec = row_blk(c)

        def body(*refs):
            _conv_kernel(*refs, tm=tm, group=group)

    return pl.pallas_call(
        body, grid=grid, in_specs=in_specs,
        out_specs=[row_blk(c), u_spec],
        out_shape=[jax.ShapeDtypeStruct((t, c), bf16), u_shape],
        scratch_shapes=[pltpu.VMEM((tm + 8, c), f32)],
        compiler_params=_params(), name="conv_proj")(*args)


def _prompt_attn_kernel(q_ref, k_ref, kx_ref, v_ref, o_ref, *scratch, tq, n_sub):
    m_scr, acc_scr = scratch[:n_sub], scratch[n_sub:2 * n_sub]
    s_a, s_b = scratch[2 * n_sub:3 * n_sub], scratch[3 * n_sub:]
    i = pl.program_id(1)
    ts = tq // n_sub
    ones_q = jnp.ones((ts, LANES), bf16)
    qa = [jnp.concatenate([q_ref[u * ts:(u + 1) * ts, :], ones_q], axis=-1) for u in range(n_sub)]
    ones_v = jnp.ones((tq, LANES), bf16)
    for u in range(n_sub):
        m_scr[u][...] = jnp.full(m_scr[u].shape, NEG, f32)
        acc_scr[u][...] = jnp.zeros(acc_scr[u].shape, f32)

    def scores(j, s_buf):
        ks = pl.multiple_of(j * tq, tq)
        ka = jnp.concatenate([k_ref[pl.ds(ks, tq), :], kx_ref[pl.ds(ks, tq), :]], axis=-1)
        for u in range(n_sub):
            s_buf[u][...] = _dot_nt(qa[u], ka)

    def accumulate(j, s_buf, masked):
        ks = pl.multiple_of(j * tq, tq)
        va = jnp.concatenate([v_ref[pl.ds(ks, tq), :], ones_v], axis=-1)
        for u in range(n_sub):
            s = s_buf[u][...]
            if masked:
                row = lax.broadcasted_iota(jnp.int32, (ts, tq), 0) + u * ts
                col = lax.broadcasted_iota(jnp.int32, (ts, tq), 1)
                s = jnp.where(col <= row, s, NEG)
            m_prev = m_scr[u][...]
            m_new = jnp.maximum(m_prev, jnp.max(s, axis=1, keepdims=True))
            p = jnp.exp2(s - _lane_tile(m_new, tq // LANES))
            alpha = jnp.exp2(m_prev - m_new)
            acc_scr[u][...] = acc_scr[u][...] * _lane_tile(alpha, 2) + _dot(p.astype(bf16), va)
            m_scr[u][...] = m_new

    def run_blocks(first, n, *, prefetch_next, mask_last):
        bufs = (s_a, s_b)
        for b in range(n):
            if b + 1 < n or prefetch_next:
                scores(first + b + 1, bufs[(b + 1) % 2])
            accumulate(first + b, bufs[b % 2], mask_last and b == n - 1)

    scores(0, s_a)

    def body(jj, c):
        run_blocks(LOOP_BLOCKS * jj, LOOP_BLOCKS, prefetch_next=True, mask_last=False)
        return c

    lax.fori_loop(0, i // LOOP_BLOCKS, body, 0)

    for rest in range(LOOP_BLOCKS):
        @pl.when(i % LOOP_BLOCKS == rest)
        def _():
            run_blocks(i - rest, rest + 1, prefetch_next=False, mask_last=True)

    for u in range(n_sub):
        acc = acc_scr[u][...]
        o_ref[u * ts:(u + 1) * ts, :] = (acc[:, :LANES] / acc[:, LANES:]).astype(o_ref.dtype)


def _prompt_attn(q, k, kx, v, *, n_heads):
    s = q.shape[0]
    tq = min(TQ, s)
    grid = (n_heads, s // tq)
    q_blk = pl.BlockSpec((tq, HEAD_DIM), lambda h, i: (i, h))
    kv_blk = pl.BlockSpec((s, HEAD_DIM), lambda h, i: (0, h))
    return pl.pallas_call(
        functools.partial(_prompt_attn_kernel, tq=tq, n_sub=N_SUB),
        grid=grid, in_specs=[q_blk, kv_blk, kv_blk, kv_blk], out_specs=q_blk,
        out_shape=jax.ShapeDtypeStruct(q.shape, bf16),
        scratch_shapes=([pltpu.VMEM((tq // N_SUB, LANES), f32)] * N_SUB
                        + [pltpu.VMEM((tq // N_SUB, 2 * LANES), f32)] * N_SUB
                        + [pltpu.VMEM((tq // N_SUB, tq), f32)] * (2 * N_SUB)),
        compiler_params=_params(("arbitrary", "arbitrary")), name="prompt_attn")(q, k, kx, v)


def _sample_attn_kernel(pt_ref, q_ref, knew_hbm, vnew_hbm, lnew_hbm, ck_hbm, cv_hbm, cl_hbm,
                        o_ref, kbuf, vbuf, lbuf, bias_scr, sbuf, sem,
                        *, n_pages, n_batch, n_heads, n_new, page):
    b = pl.program_id(0)
    slot = b % 2
    npg = n_pages + 1
    rows = page * n_heads
    new_rows = n_new * n_heads

    def copies(bb, sl):
        cps = []
        for p in range(n_pages):
            pg = pt_ref[bb * n_pages + p]
            src = pl.ds(pl.multiple_of(pg * rows, rows), rows)
            dst = pl.ds(p * rows, rows)
            cps.append(pltpu.make_async_copy(ck_hbm.at[src], kbuf.at[sl, dst], sem.at[sl, 0]))
            cps.append(pltpu.make_async_copy(cv_hbm.at[src], vbuf.at[sl, dst], sem.at[sl, 1]))
            cps.append(pltpu.make_async_copy(cl_hbm.at[pg], lbuf.at[sl, p], sem.at[sl, 2]))
        src = pl.ds(pl.multiple_of(bb * new_rows, new_rows), new_rows)
        dst = pl.ds(n_pages * rows, new_rows)
        cps.append(pltpu.make_async_copy(knew_hbm.at[src], kbuf.at[sl, dst], sem.at[sl, 0]))
        cps.append(pltpu.make_async_copy(vnew_hbm.at[src], vbuf.at[sl, dst], sem.at[sl, 1]))
        cps.append(pltpu.make_async_copy(lnew_hbm.at[bb], lbuf.at[sl, n_pages], sem.at[sl, 2]))
        return cps

    @pl.when(b == 0)
    def _():
        for sl in range(2):
            kbuf[sl, pl.ds(n_pages * rows, rows), :] = jnp.zeros((rows, HEAD_DIM), f32)
            vbuf[sl, pl.ds(n_pages * rows, rows), :] = jnp.zeros((rows, HEAD_DIM), f32)
        for cp in copies(0, 0):
            cp.start()

    @pl.when(b + 1 < n_batch)
    def _():
        for cp in copies(b + 1, 1 - slot):
            cp.start()

    kv_rows = pl.ds(0, n_pages * rows + new_rows)
    for j, region in enumerate((kbuf.at[slot, kv_rows], vbuf.at[slot, kv_rows], lbuf.at[slot])):
        pltpu.make_async_copy(region, region, sem.at[slot, j]).wait()

    row = lax.broadcasted_iota(jnp.int32, (page, page), 0)
    col = lax.broadcasted_iota(jnp.int32, (page, page), 1)
    after = jnp.where(row > col, 1.0, 0.0).astype(bf16)
    lf_all = lbuf[slot].reshape(npg * n_heads, page)
    hi, mid, lo = _split3(lf_all)
    suf = (_dot(hi, after) + _dot(mid, after)) + _dot(lo, after)
    tot = jnp.sum(lf_all, axis=1, keepdims=True)
    carry = jnp.zeros((n_heads, 1), f32)
    for p in range(npg - 1, -1, -1):
        rs = slice(p * n_heads, (p + 1) * n_heads)
        bias_scr[p] = (suf[rs, :] + carry) * LOG2E
        carry = carry + tot[rs, :]

    qb = q_ref[0].astype(bf16)
    qh = [qb[:, h * HEAD_DIM:(h + 1) * HEAD_DIM] for h in range(n_heads)]
    t_row = lax.broadcasted_iota(jnp.int32, (8, page), 0)
    lane = lax.broadcasted_iota(jnp.int32, (8, page), 1)

    def head_rows(buf, p, h):
        return buf[slot, pl.ds(p * rows + h, page, stride=n_heads), :].astype(bf16)

    def score_page(p, mvec, masked):
        out = []
        for h in range(n_heads):
            s = _dot_nt(qh[h], head_rows(kbuf, p, h)) + bias_scr[p, pl.ds(h, 1), :]
            if masked:
                s = jnp.where(lane <= t_row, s, NEG)
            sbuf[p, h] = s
            out.append(jnp.maximum(mvec[h], s))
        return tuple(out)

    unroll = math.gcd(n_pages, PAGE_UNROLL)
    mvec = lax.fori_loop(0, n_pages, functools.partial(score_page, masked=False),
                         tuple(jnp.full((8, page), NEG, f32) for _ in range(n_heads)),
                         unroll=unroll)
    mvec = score_page(n_pages, mvec, True)
    m = [jnp.max(mv, axis=1, keepdims=True) for mv in mvec]

    def value_page(p, carry):
        lvec, acc = carry
        lout, aout = [], []
        for h in range(n_heads):
            pr = jnp.exp2(sbuf[p, h] - m[h])
            lout.append(lvec[h] + pr)
            aout.append(acc[h] + _dot(pr.astype(bf16), head_rows(vbuf, p, h)))
        return tuple(lout), tuple(aout)

    zeros = tuple(jnp.zeros((8, page), f32) for _ in range(n_heads))
    lvec, acc = lax.fori_loop(0, n_pages, value_page, (zeros, zeros), unroll=unroll)
    lvec, acc = value_page(n_pages, (lvec, acc))
    for h in range(n_heads):
        l = jnp.sum(lvec[h], axis=1, keepdims=True)
        o_ref[0, :, h * HEAD_DIM:(h + 1) * HEAD_DIM] = acc[h] / l


def _sample_attn(page_table, q8, knew, vnew, lnew, ck, cv, cl, *, n_heads, n_new):
    n_batch, n_pages = page_table.shape
    page = cl.shape[-1]
    assert page == HEAD_DIM == LANES
    a = q8.shape[-1]
    rows = (n_pages + 1) * page * n_heads
    any_spec = pl.BlockSpec(memory_space=pl.ANY)
    q_blk = pl.BlockSpec((1, 8, a), lambda b, pt: (b, 0, 0))
    grid_spec = pltpu.PrefetchScalarGridSpec(
        num_scalar_prefetch=1, grid=(n_batch,),
        in_specs=[q_blk, any_spec, any_spec, any_spec, any_spec, any_spec, any_spec],
        out_specs=q_blk,
        scratch_shapes=[
            pltpu.VMEM((2, rows, HEAD_DIM), f32),
            pltpu.VMEM((2, rows, HEAD_DIM), f32),
            pltpu.VMEM((2, n_pages + 1, n_heads, page), f32),
            pltpu.VMEM((n_pages + 1, n_heads, page), f32),
            pltpu.VMEM((n_pages + 1, n_heads, 8, page), f32),
            pltpu.SemaphoreType.DMA((2, 3)),
        ])
    return pl.pallas_call(
        functools.partial(_sample_attn_kernel, n_pages=n_pages, n_batch=n_batch,
                          n_heads=n_heads, n_new=n_new, page=page),
        grid_spec=grid_spec, out_shape=jax.ShapeDtypeStruct(q8.shape, f32),
        compiler_params=_params(), name="sample_attn")(
            page_table.reshape(-1), q8, knew, vnew, lnew, ck, cv, cl)


def _layer_norm(y, g, b):
    mu = jnp.mean(y, axis=-1, keepdims=True)
    yc = y - mu
    var = jnp.mean(yc * yc, axis=-1, keepdims=True)
    return yc * lax.rsqrt(var + LN_EPS) * g + b


def _finish1_kernel(ap_ref, cp_ref, xp_ref, as_ref, cs_ref, xs_ref,
                    woa_ref, woc_ref, g_ref, b_ref, wr2_ref, wrh_ref, br_ref,
                    x1_ref, x1g_ref, meta_ref, cnt_ref, gw_ref, cnt_scr, *, alpha, nb_prompt, tm):
    is_prompt = pl.program_id(0) < nb_prompt
    attn = jnp.where(is_prompt, ap_ref[...], as_ref[...])
    conv = jnp.where(is_prompt, cp_ref[...], cs_ref[...])
    x = jnp.where(is_prompt, xp_ref[...], xs_ref[...])
    o = _dot(attn, woa_ref[...]) + _dot(conv, woc_ref[...])
    x1 = _layer_norm(alpha * x + o, g_ref[...], b_ref[...])
    x1_ref[...] = x1
    n_chunks = x1.shape[1] // LANES
    for c in range(n_chunks):
        x1g_ref[pl.ds(c, tm, stride=n_chunks), :] = x1[:, c * LANES:(c + 1) * LANES]

    x1b = x1.astype(bf16)
    x1l = (x1 - x1b.astype(f32)).astype(bf16)
    r2 = _dot(x1b, wr2_ref[...])
    lr = r2[:, :LANES] + r2[:, LANES:] + _dot(x1l, wrh_ref[...]) + br_ref[...]
    lane = lax.broadcasted_iota(jnp.int32, lr.shape, 1)

    def first_max(mask):
        vals = jnp.where(mask, lr, -jnp.inf)
        mx = jnp.max(vals, axis=-1, keepdims=True)
        idx = jnp.min(jnp.where(mask & (vals == mx), lane, LANES), axis=-1, keepdims=True)
        return mx, idx

    is_group = lane < N_GROUPS
    gmax, gidx = first_max(is_group)
    p_group = 1.0 / jnp.sum(jnp.where(is_group, jnp.exp(lr - gmax), 0.0), axis=-1, keepdims=True)
    lo = N_GROUPS + gidx * EXPERTS_PER_GROUP
    in_group = (lane >= lo) & (lane < lo + EXPERTS_PER_GROUP)
    e1, i1 = first_max(in_group)
    e2, i2 = first_max(in_group & (lane != i1))
    r = jnp.exp(e2 - e1)
    w1 = p_group / (1.0 + r)
    w2 = p_group * r / (1.0 + r)
    gw_ref[...] = jnp.where(lane == 0, w1, jnp.where(lane == 1, w2, 0.0))

    @pl.when(pl.program_id(0) == 0)
    def _():
        cnt_scr[...] = jnp.zeros(cnt_scr.shape, f32)

    id1, id2 = i1 - N_GROUPS, i2 - N_GROUPS
    sel1 = jnp.where(lane == id1, 1.0, 0.0)
    sel2 = jnp.where(lane == id2, 1.0, 0.0)
    row = lax.broadcasted_iota(jnp.int32, (tm, tm), 0)
    col = lax.broadcasted_iota(jnp.int32, (tm, tm), 1)
    before = jnp.where(col < row, 1.0, 0.0).astype(bf16)
    seen = cnt_scr[0:1, :]
    tot1 = jnp.sum(sel1, axis=0, keepdims=True)
    rank1 = jnp.sum(sel1 * (seen + _dot(before, sel1.astype(bf16))), axis=-1, keepdims=True)
    rank2 = jnp.sum(sel2 * ((seen + tot1) + _dot(before, sel2.astype(bf16))), axis=-1, keepdims=True)
    seen = (seen + tot1) + jnp.sum(sel2, axis=0, keepdims=True)
    cnt_scr[...] = jnp.broadcast_to(seen, cnt_scr.shape)
    cnt_ref[...] = jnp.broadcast_to(seen, cnt_ref.shape)
    meta = jnp.where(lane == 0, id1.astype(f32), jnp.where(lane == 1, id2.astype(f32),
                     jnp.where(lane == 2, rank1, jnp.where(lane == 3, rank2, 0.0))))
    meta_ref[...] = meta.T[0:8, :]


def _finish1(attn_p, conv_p, x_p, attn_s, conv_s, x_s, wo, g, b, wr2, wrh, br, *, alpha):
    (t_p, d), t_s = x_p.shape, x_s.shape[0]
    a = attn_p.shape[1]
    assert wo.shape[0] == 2 * a
    tm = min(TM_PROJ, t_s)
    nb_p, nb_s = t_p // tm, t_s // tm
    t_all = t_p + t_s
    p_blk = lambda w: pl.BlockSpec((tm, w), lambda i: (jnp.minimum(i, nb_p - 1), 0))
    s_blk = lambda w: pl.BlockSpec((tm, w), lambda i: (jnp.maximum(i - nb_p, 0), 0))
    out_blk = lambda w: pl.BlockSpec((tm, w), lambda i: (i, 0))
    in_specs = [p_blk(a), p_blk(a), p_blk(d), s_blk(a), s_blk(a), s_blk(d),
                pl.BlockSpec((a, d), lambda i: (0, 0), pipeline_mode=pl.Buffered(1)),
                pl.BlockSpec((a, d), lambda i: (1, 0), pipeline_mode=pl.Buffered(1)),
                _resident(g.shape), _resident(b.shape), _resident(wr2.shape),
                _resident(wrh.shape), _resident(br.shape)]
    n_chunks = d // LANES
    return pl.pallas_call(
        functools.partial(_finish1_kernel, alpha=alpha, nb_prompt=nb_p, tm=tm),
        grid=(nb_p + nb_s,), in_specs=in_specs,
        out_specs=[out_blk(d), pl.BlockSpec((tm * n_chunks, LANES), lambda i: (i, 0)),
                   pl.BlockSpec((8, tm), lambda i: (0, i)),
                   pl.BlockSpec((8, LANES), lambda i: (0, 0)), out_blk(LANES)],
        out_shape=[jax.ShapeDtypeStruct((t_all, d), f32),
                   jax.ShapeDtypeStruct((t_all * n_chunks, LANES), f32),
                   jax.ShapeDtypeStruct((8, t_all), f32),
                   jax.ShapeDtypeStruct((8, LANES), f32),
                   jax.ShapeDtypeStruct((t_all, LANES), f32)],
        scratch_shapes=[pltpu.VMEM((8, LANES), f32)],
        compiler_params=_params(), name="finish1")(
            attn_p, conv_p, x_p, attn_s, conv_s, x_s, wo, wo, g, b, wr2, wrh, br)


def _dispatch_kernel(lt_ref, nu_ref, pos_ref, x_ref, xs_hbm, zbuf, sem, zsem, *, tm, tm_moe, n_tiles):
    s = pl.program_id(0)

    @pl.when(s == 0)
    def _():
        zbuf[...] = jnp.zeros(zbuf.shape, f32)

        def zero_tile(row0):
            return pltpu.make_async_copy(zbuf, xs_hbm.at[pl.ds(row0, tm_moe)], zsem)

        def each_zero_tile(act):
            for e in range(N_EXPERTS):
                @pl.when(lt_ref[e] >= 0)
                def _():
                    act(zero_tile(lt_ref[e]))

            def unused(t, c):
                act(zero_tile(t * tm_moe))
                return c

            lax.fori_loop(nu_ref[0], n_tiles, unused, 0)

        each_zero_tile(lambda cp: cp.start())
        each_zero_tile(lambda cp: cp.wait())

    def issue(r, c):
        pltpu.make_async_copy(x_ref.at[r], xs_hbm.at[pos_ref[0, 0, r]], sem).start(priority=0)
        pltpu.make_async_copy(x_ref.at[r], xs_hbm.at[pos_ref[0, 0, tm + r]], sem).start(priority=1)
        return c

    lax.fori_loop(0, tm, issue, 0, unroll=8)
    done = xs_hbm.at[pl.ds(0, 2 * tm)]
    pltpu.make_async_copy(done, done, sem).wait()


def _dispatch(x1g, pos, last_tile_row, n_used, *, n_tiles, tm_moe):
    t, n_chunks, _ = x1g.shape
    n_steps, _, two_tm = pos.shape
    tm = two_tm // 2
    grid_spec = pltpu.PrefetchScalarGridSpec(
        num_scalar_prefetch=2, grid=(n_steps,),
        in_specs=[pl.BlockSpec((1, 1, two_tm), lambda s, *_: (s, 0, 0), memory_space=pltpu.SMEM),
                  pl.BlockSpec((tm, n_chunks, LANES), lambda s, *_: (s, 0, 0))],
        out_specs=pl.BlockSpec(memory_space=pl.ANY),
        scratch_shapes=[pltpu.VMEM((tm_moe, n_chunks, LANES), f32),
                        pltpu.SemaphoreType.DMA(()), pltpu.SemaphoreType.DMA(())])
    return pl.pallas_call(
        functools.partial(_dispatch_kernel, tm=tm, tm_moe=tm_moe, n_tiles=n_tiles),
        grid_spec=grid_spec,
        out_shape=jax.ShapeDtypeStruct((n_tiles * tm_moe, n_chunks, LANES), f32),
        compiler_params=_params(), name="dispatch")(last_tile_row, n_used, pos, x1g)


def _moe_kernel(te_ref, ord_ref, nxt_ref, nu_ref, x_ref, wg_hbm, wu_hbm, wd_hbm, y_ref,
                wgf, wuf, wdf, wgb, wub, wdb, wsem, *, tm, n_chunks):
    t = pl.program_id(0)
    n_used = nu_ref[0]

    def weight_copies(e, wslot):
        return (pltpu.make_async_copy(wg_hbm.at[e], wgf.at[wslot], wsem.at[wslot, 0]),
                pltpu.make_async_copy(wu_hbm.at[e], wuf.at[wslot], wsem.at[wslot, 1]),
                pltpu.make_async_copy(wd_hbm.at[e], wdf.at[wslot], wsem.at[wslot, 2]))

    @pl.when(t == 0)
    def _():
        for cp in weight_copies(te_ref[0], 0):
            cp.start(priority=1)

    @pl.when(t < n_used)
    def _():
        e = te_ref[t]
        new_expert = (t == 0) | (e != te_ref[jnp.maximum(t - 1, 0)])

        @pl.when(new_expert)
        def _():
            wslot = ord_ref[t] % 2
            for cp in weight_copies(e, wslot):
                cp.wait()
            nxt = nxt_ref[e]

            @pl.when(nxt >= 0)
            def _():
                for cp in weight_copies(nxt, 1 - wslot):
                    cp.start(priority=1)

            wgb[...] = wgf[wslot].astype(bf16)
            wub[...] = wuf[wslot].astype(bf16)
            wdb[...] = wdf[wslot].astype(bf16)

        xb = jnp.concatenate([x_ref[pl.ds(c, tm, stride=n_chunks), :] for c in range(n_chunks)],
                             axis=-1).astype(bf16)
        hg = _dot(xb, wgb[...])
        hu = _dot(xb, wub[...])
        act = hg * (1.0 / (1.0 + jnp.exp(-hg))) * hu
        y = _dot(act.astype(bf16), wdb[...])
        for c in range(n_chunks):
            y_ref[pl.ds(c, tm, stride=n_chunks), :] = y[:, c * LANES:(c + 1) * LANES]

    @pl.when(t >= n_used)
    def _():
        y_ref[...] = jnp.zeros(y_ref.shape, f32)


def _moe(xs, tile_expert, tile_ord, next_expert, n_used, wg, wu, wd, *, tm):
    _, d, f = wg.shape
    n_chunks = d // LANES
    n_tiles = xs.shape[0] // (tm * n_chunks)
    any_spec = pl.BlockSpec(memory_space=pl.ANY)

    def x_idx(t, te, od, nx, nu):
        return (jnp.minimum(t, jnp.maximum(nu[0] - 1, 0)), 0)

    grid_spec = pltpu.PrefetchScalarGridSpec(
        num_scalar_prefetch=4, grid=(n_tiles,),
        in_specs=[pl.BlockSpec((tm * n_chunks, LANES), x_idx), any_spec, any_spec, any_spec],
        out_specs=pl.BlockSpec((tm * n_chunks, LANES), lambda t, *_: (t, 0)),
        scratch_shapes=[pltpu.VMEM((2, d, f), f32), pltpu.VMEM((2, d, f), f32),
                        pltpu.VMEM((2, f, d), f32),
                        pltpu.VMEM((d, f), bf16), pltpu.VMEM((d, f), bf16), pltpu.VMEM((f, d), bf16),
                        pltpu.SemaphoreType.DMA((2, 3))])
    return pl.pallas_call(
        functools.partial(_moe_kernel, tm=tm, n_chunks=n_chunks),
        grid_spec=grid_spec, out_shape=jax.ShapeDtypeStruct(xs.shape, f32),
        compiler_params=_params(), name="moe")(
            tile_expert, tile_ord, next_expert, n_used, xs, wg, wu, wd)


def _finish2_kernel(idx_ref, ys_hbm, x1_ref, gw_ref, g_ref, b_ref, o_ref, gbuf, sem,
                    *, tm, n_steps, alpha):
    s = pl.program_id(0)
    n_chunks = ys_hbm.shape[1]
    pitch = n_chunks + ROW_PITCH_PAD

    @pl.when(s < n_steps)
    def _():
        slot = s % 2

        def issue(r, c):
            for k in range(2):
                rr = k * tm + r
                pltpu.make_async_copy(ys_hbm.at[idx_ref[0, 0, rr]],
                                      gbuf.at[slot, pl.ds(rr * pitch, n_chunks)],
                                      sem.at[slot]).start(priority=k)
            return c

        lax.fori_loop(0, tm, issue, 0, unroll=4)

    @pl.when(s >= 1)
    def _():
        slot = (s - 1) % 2
        landed = gbuf.at[slot, pl.ds(0, 2 * tm * n_chunks)]
        pltpu.make_async_copy(landed, landed, sem.at[slot]).wait()

        def rows(first):
            return jnp.concatenate(
                [gbuf[slot, pl.ds(first * pitch + c, tm, stride=pitch), :] for c in range(n_chunks)],
                axis=-1)

        gw = gw_ref[...]
        f = gw[:, 0:1] * rows(0) + gw[:, 1:2] * rows(tm)
        o_ref[...] = _layer_norm(alpha * x1_ref[...] + f, g_ref[...], b_ref[...])


def _finish2(ys, pos, x1, gw, g, b, *, alpha, row0, t):
    n_steps, _, two_tm = pos.shape
    tm = two_tm // 2
    d = x1.shape[1]
    n_chunks = d // LANES
    blk0 = row0 // tm
    last = n_steps - 1
    in_blk = lambda w: pl.BlockSpec((tm, w), lambda s: (jnp.maximum(s - 1, 0) + blk0, 0))
    return pl.pallas_call(
        functools.partial(_finish2_kernel, tm=tm, n_steps=n_steps, alpha=alpha),
        grid=(n_steps + 1,),
        in_specs=[pl.BlockSpec((1, 1, two_tm), lambda s: (jnp.minimum(s, last), 0, 0),
                               memory_space=pltpu.SMEM),
                  pl.BlockSpec(memory_space=pl.ANY),
                  in_blk(d), in_blk(LANES), _resident(g.shape), _resident(b.shape)],
        out_specs=pl.BlockSpec((tm, d), lambda s: (jnp.maximum(s - 1, 0), 0)),
        out_shape=jax.ShapeDtypeStruct((t, d), f32),
        scratch_shapes=[pltpu.VMEM((2, two_tm * (n_chunks + ROW_PITCH_PAD), LANES), f32),
                        pltpu.SemaphoreType.DMA((2,))],
        compiler_params=_params(), name="finish2")(pos, ys, x1, gw, g, b)


def _bias_selectors(n_heads, sign):
    sel = np.zeros((3, n_heads, n_heads * HEAD_DIM), np.float32)
    for p in range(3):
        for h in range(n_heads):
            sel[p, h, h * HEAD_DIM + p] = sign
    return sel


def _route_tables(meta, counts, tm):
    n_tok = meta.shape[1]
    e = meta[0:2].astype(jnp.int32)
    rank = meta[2:4].astype(jnp.int32)
    counts = counts[0, :N_EXPERTS].astype(jnp.int32)
    n_pairs = 2 * n_tok
    padded = ((counts + tm - 1) // tm) * tm
    ends = jnp.cumsum(padded)
    experts = jnp.arange(N_EXPERTS, dtype=jnp.int32)
    first_row = jnp.sum(jnp.where(e[..., None] == experts, ends - padded, 0), axis=-1)
    pos = first_row + rank
    n_tiles = (n_pairs + N_EXPERTS * (tm - 1)) // tm
    last_tile_row = jnp.where(counts > 0, ends - tm, -1).astype(jnp.int32)
    tile_start = jnp.arange(n_tiles, dtype=jnp.int32) * tm
    tile_expert = jnp.minimum(
        jnp.sum((ends[None, :] <= tile_start[:, None]).astype(jnp.int32), axis=1), N_EXPERTS - 1)
    n_used = (ends[-1] // tm).astype(jnp.int32).reshape(1)
    nonempty = counts > 0
    expert_ord = jnp.cumsum(nonempty.astype(jnp.int32)) - 1
    later = nonempty[None, :] & (experts[None, :] > experts[:, None])
    next_expert = jnp.min(jnp.where(later, experts[None, :], N_EXPERTS), axis=1)
    next_expert = jnp.where(next_expert < N_EXPERTS, next_expert, -1).astype(jnp.int32)
    tile_ord = expert_ord[tile_expert].astype(jnp.int32)
    return pos, last_tile_row, n_tiles, tile_expert, tile_ord, next_expert, n_used


def _tile_pos(pos, tm):
    t = pos.shape[1]
    return jnp.concatenate([pos[0].reshape(t // tm, 1, tm), pos[1].reshape(t // tm, 1, tm)], axis=2)


def kernel(x_prompt, x_sample, cache_k, cache_v, cache_logf, state_conv, page_table, w_in, b_f,
           w_conv, w_out, ln1_g, ln1_b, w_rg, b_rg, w_re, b_re, w_gate, w_up, w_down, ln2_g, ln2_b):
    depth, d_model, _ = w_in.shape
    n_batch_p, seq, _ = x_prompt.shape
    db, dseq, _ = x_sample.shape
    n_heads = cache_k.shape[3]
    a = n_heads * HEAD_DIM
    c = w_conv.shape[-1]
    assert n_batch_p == 1 and dseq <= 8 and w_conv.shape[1] == CONV_K
    alpha = (2.0 * depth) ** 0.25
    qscale = HEAD_DIM ** -0.5 * LOG2E
    t_p, t_s = seq, db * dseq
    t_all = t_p + t_s

    sel_neg = _bias_selectors(n_heads, -1.0)
    psel_k = jnp.asarray(np.concatenate(
        [np.pad(sel_neg[p], ((0, LANES - n_heads), (0, 0))) for p in range(3)], axis=0), bf16)

    xp = x_prompt.reshape(t_p, d_model)
    xs = x_sample.reshape(t_s, d_model)
    outs = {k: [] for k in ("kp", "vp", "fp", "cp", "ks", "vs", "fs", "cs")}
    for l in range(depth):
        wl = jnp.swapaxes(w_in[l], 0, 1)
        wq, wk, wv = (wl[j * a:(j + 1) * a].astype(bf16) for j in range(3))
        o = 3 * a
        wf = jnp.pad(wl[o:o + n_heads], ((0, LANES - n_heads), (0, 0)))
        wfh = wf.astype(bf16)
        wf2 = jnp.concatenate([wfh, (wf - wfh.astype(f32)).astype(bf16)], axis=0)
        bfp = jnp.pad(b_f[l], (0, LANES - n_heads)).reshape(1, LANES)
        o += n_heads
        wb, wc, wh = (wl[o + j * c:o + (j + 1) * c].astype(bf16) for j in range(3))
        wo = w_out[l].astype(bf16)
        wr = jnp.pad(jnp.concatenate([w_rg[l], w_re[l]], axis=1),
                     ((0, 0), (0, LANES - N_GROUPS - N_EXPERTS)))
        wrh = wr.astype(bf16)
        wr2 = jnp.concatenate([wrh, (wr - wrh.astype(f32)).astype(bf16)], axis=1)
        br = jnp.pad(jnp.concatenate([b_rg[l], b_re[l]]), (0, LANES - N_GROUPS - N_EXPERTS)).reshape(1, LANES)
        g1, b1 = ln1_g[l].reshape(1, -1), ln1_b[l].reshape(1, -1)
        g2, b2 = ln2_g[l].reshape(1, -1), ln2_b[l].reshape(1, -1)

        q, ko, vo, kb, vb, lf, kx = _qkv_proj(xp, wq, wk, wv, wf2, wfh, bfp, psel_k,
                                              n_heads=n_heads, qscale=qscale, with_cum=True)
        conv, utail = _conv_proj(xp, wb, wc, wh, w_conv[l])
        attn = _prompt_attn(q, kb, kx, vb, n_heads=n_heads)
        outs["kp"].append(ko.reshape(1, t_p, n_heads, HEAD_DIM))
        outs["vp"].append(vo.reshape(1, t_p, n_heads, HEAD_DIM))
        outs["fp"].append(lf[:, :n_heads].reshape(1, t_p, n_heads))
        outs["cp"].append(utail[8 - (CONV_K - 1):].reshape(1, CONV_K - 1, c))

        qs, kos, vos, _, _, lfs = _qkv_proj(xs, wq, wk, wv, wf2, wfh, bfp, None,
                                            n_heads=n_heads, qscale=qscale, with_cum=False)
        st = state_conv[l]
        zero = jnp.zeros((db, dseq, c), f32)
        p1 = zero.at[:, 0].set(st[:, 1]).reshape(t_s, c)
        p2 = zero.at[:, 0].set(st[:, 0]).at[:, 1].set(st[:, 1]).reshape(t_s, c)
        conv_s, u_s = _conv_proj(xs, wb, wc, wh, w_conv[l], p1, p2, group=dseq)
        q8 = jnp.pad(qs.reshape(db, dseq, a), ((0, 0), (0, 8 - dseq), (0, 0))).astype(f32)
        page = cache_k.shape[2]
        lnew = jnp.pad(jnp.swapaxes(lfs[:, :n_heads].reshape(db, dseq, n_heads), 1, 2),
                       ((0, 0), (0, 0), (0, page - dseq)))
        attn_s = _sample_attn(page_table, q8, kos, vos, lnew,
                              cache_k[l].reshape(-1, HEAD_DIM), cache_v[l].reshape(-1, HEAD_DIM),
                              jnp.swapaxes(cache_logf[l], 1, 2), n_heads=n_heads, n_new=dseq)
        attn_s = attn_s[:, :dseq].reshape(t_s, a).astype(bf16)
        outs["ks"].append(kos.reshape(db, dseq, n_heads, HEAD_DIM))
        outs["vs"].append(vos.reshape(db, dseq, n_heads, HEAD_DIM))
        outs["fs"].append(lfs[:, :n_heads].reshape(db, dseq, n_heads))
        outs["cs"].append(u_s.reshape(db, dseq, c)[:, dseq - (CONV_K - 1):])
        x1, x1g, meta, counts, gw = _finish1(attn, conv, xp, attn_s, conv_s, xs, wo, g1, b1,
                                             wr2, wrh, br, alpha=alpha)

        pos, last_tile_row, n_tiles, tile_expert, tile_ord, next_expert, n_used = _route_tables(
            meta, counts, TM_MOE)
        n_chunks = d_model // LANES
        tm2 = min(TM_PROJ, t_s)
        xsort = _dispatch(x1g.reshape(t_all, n_chunks, LANES), _tile_pos(pos, tm2), last_tile_row,
                          n_used, n_tiles=n_tiles, tm_moe=TM_MOE)
        ys = _moe(xsort.reshape(-1, LANES), tile_expert, tile_ord, next_expert, n_used,
                  w_gate[l], w_up[l], w_down[l], tm=TM_MOE)
        ys = ys.reshape(-1, n_chunks, LANES)
        xp = _finish2(ys, _tile_pos(pos[:, :t_p], TM_PROJ), x1, gw, g2, b2, alpha=alpha, row0=0, t=t_p)
        xs = _finish2(ys, _tile_pos(pos[:, t_p:], tm2), x1, gw, g2, b2, alpha=alpha, row0=t_p, t=t_s)

    stack = lambda k: jnp.stack(outs[k])
    return (xp.reshape(x_prompt.shape), xs.reshape(x_sample.shape),
            stack("kp"), stack("vp"), stack("fp"), stack("cp"),
            stack("ks"), stack("vs"), stack("fs"), stack("cs"))
```

```python
import functools
import math

import numpy as np
import jax
import jax.numpy as jnp
from jax import lax
from jax.experimental import pallas as pl
from jax.experimental.pallas import tpu as pltpu

f32 = jnp.float32
bf16 = jnp.bfloat16

LANES = 128
HEAD_DIM = 128
LOG2E = 1.4426950408889634
LN_EPS = 1e-5
NEG = -1e30
CONV_K = 3
N_GROUPS = 4
EXPERTS_PER_GROUP = 8
N_EXPERTS = N_GROUPS * EXPERTS_PER_GROUP
VMEM_LIMIT = 56 * 1024 * 1024

TM_IN = 512
TM_PROJ = 256
TQ = 1024
N_SUB = 4
LOOP_BLOCKS = 2
PAGE_UNROLL = 16
TM_MOE = 256
ROW_PITCH_PAD = 1


def _dot(a, b):
    return jnp.dot(a, b, preferred_element_type=f32)


def _dot_nt(a, b):
    return lax.dot_general(a, b, (((1,), (1,)), ((), ())), preferred_element_type=f32)


def _lane_tile(x, n):
    return jnp.concatenate([x] * n, axis=-1)


def _split3(x):
    hi = x.astype(bf16)
    r1 = x - hi.astype(f32)
    mid = r1.astype(bf16)
    lo = (r1 - mid.astype(f32)).astype(bf16)
    return hi, mid, lo


def _params(sem=("arbitrary",)):
    return pltpu.CompilerParams(dimension_semantics=sem, vmem_limit_bytes=VMEM_LIMIT)


def _resident(shape):
    nd = len(shape)
    return pl.BlockSpec(shape, lambda *_: (0,) * nd, pipeline_mode=pl.Buffered(1))


def _qkv_kernel(x_ref, wq_ref, wk_ref, wv_ref, wf2_ref, wfh_ref, bf_ref, psel_ref,
                q_ref, ko_ref, vo_ref, kb_ref, vb_ref, lf_ref, kx_ref, carry_ref,
                *, tm, n_heads, qscale):
    i = pl.program_id(0)
    x = x_ref[...]
    xb = x.astype(bf16)
    q_ref[...] = (_dot_nt(xb, wq_ref[...]) * qscale).astype(bf16)
    k = _dot_nt(xb, wk_ref[...])
    kb_ref[...] = k.astype(bf16)
    v = _dot_nt(xb, wv_ref[...])
    vb_ref[...] = v.astype(bf16)
    for h in range(n_heads):
        ko_ref[pl.ds(h, tm, stride=n_heads), :] = k[:, h * HEAD_DIM:(h + 1) * HEAD_DIM]
        vo_ref[pl.ds(h, tm, stride=n_heads), :] = v[:, h * HEAD_DIM:(h + 1) * HEAD_DIM]

    xl = (x - xb.astype(f32)).astype(bf16)
    zf2 = _dot_nt(xb, wf2_ref[...])
    zf = zf2[:, :LANES] + zf2[:, LANES:] + _dot_nt(xl, wfh_ref[...]) + bf_ref[...]
    lf = jnp.minimum(zf, 0.0) - jnp.log1p(jnp.exp(-jnp.abs(zf)))
    lf_ref[...] = lf

    if kx_ref is not None:
        @pl.when(i == 0)
        def _():
            carry_ref[...] = jnp.zeros(carry_ref.shape, f32)

        row = lax.broadcasted_iota(jnp.int32, (tm, tm), 0)
        col = lax.broadcasted_iota(jnp.int32, (tm, tm), 1)
        tri = jnp.where(row >= col, 1.0, 0.0).astype(bf16)
        c3 = _dot(tri, jnp.concatenate(_split3(lf), axis=-1))
        cum = (c3[:, :LANES] + c3[:, LANES:2 * LANES] + c3[:, 2 * LANES:]) + carry_ref[0:1, :]
        carry_ref[...] = jnp.broadcast_to(cum[tm - 1:tm, :], carry_ref.shape)
        parts = jnp.concatenate(_split3(cum * LOG2E), axis=-1)
        kx_ref[...] = _dot(parts, psel_ref[...]).astype(bf16)


def _qkv_proj(x, wq, wk, wv, wf2, wfh, bfp, psel, *, n_heads, qscale, with_cum):
    t, d = x.shape
    a = wq.shape[0]
    tm = min(TM_IN, t)
    grid = (t // tm,)
    row_blk = lambda w: pl.BlockSpec((tm, w), lambda i: (i, 0))
    in_specs = [row_blk(d), _resident(wq.shape), _resident(wk.shape), _resident(wv.shape),
                _resident(wf2.shape), _resident(wfh.shape), _resident(bfp.shape)]
    args = [x, wq, wk, wv, wf2, wfh, bfp]
    out_shape = [jax.ShapeDtypeStruct((t, a), bf16),
                 jax.ShapeDtypeStruct((t * n_heads, HEAD_DIM), f32),
                 jax.ShapeDtypeStruct((t * n_heads, HEAD_DIM), f32),
                 jax.ShapeDtypeStruct((t, a), bf16),
                 jax.ShapeDtypeStruct((t, a), bf16),
                 jax.ShapeDtypeStruct((t, LANES), f32)]
    hd_blk = pl.BlockSpec((tm * n_heads, HEAD_DIM), lambda i: (i, 0))
    out_specs = [row_blk(a), hd_blk, hd_blk, row_blk(a), row_blk(a), row_blk(LANES)]
    scratch = []
    if with_cum:
        in_specs.append(_resident(psel.shape))
        args.append(psel)
        out_shape.append(jax.ShapeDtypeStruct((t, a), bf16))
        out_specs.append(row_blk(a))
        scratch.append(pltpu.VMEM((8, LANES), f32))

        def body(*refs):
            _qkv_kernel(*refs, tm=tm, n_heads=n_heads, qscale=qscale)
    else:
        def body(x_ref, wq_ref, wk_ref, wv_ref, wf2_ref, wfh_ref, bf_ref, *outs):
            _qkv_kernel(x_ref, wq_ref, wk_ref, wv_ref, wf2_ref, wfh_ref, bf_ref, None,
                        *outs, None, None, tm=tm, n_heads=n_heads, qscale=qscale)

    return pl.pallas_call(
        body, grid=grid, in_specs=in_specs, out_specs=out_specs, out_shape=out_shape,
        scratch_shapes=scratch, compiler_params=_params(), name="qkv_proj")(*args)


def _conv_kernel(x_ref, wb_ref, wc_ref, wh_ref, wconv_ref, p1_ref, p2_ref,
                 conv_ref, u_ref, ubuf, *, tm, group):
    i = pl.program_id(0)
    xb = x_ref[...].astype(bf16)
    gate_b = _dot_nt(xb, wb_ref[...])
    u = _dot_nt(xb, wc_ref[...]) * _dot_nt(xb, wh_ref[...])
    w0 = wconv_ref[0:1, :]
    w1 = wconv_ref[1:2, :]
    w2 = wconv_ref[2:3, :]
    if group is None:
        @pl.when(i == 0)
        def _():
            ubuf[0:8, :] = jnp.zeros((8, ubuf.shape[1]), f32)

        ubuf[8:8 + tm, :] = u
        u1 = ubuf[pl.ds(7, tm), :]
        u2 = ubuf[pl.ds(6, tm), :]
        ubuf[0:8, :] = u[tm - 8:tm, :]
        u_ref[...] = u[tm - 8:tm, :]
    else:
        ubuf[0:8, :] = jnp.zeros((8, ubuf.shape[1]), f32)
        ubuf[8:8 + tm, :] = u
        t_in = lax.broadcasted_iota(jnp.int32, (tm, 1), 0) % group
        u1 = jnp.where(t_in >= 1, ubuf[pl.ds(7, tm), :], p1_ref[...])
        u2 = jnp.where(t_in >= 2, ubuf[pl.ds(6, tm), :], p2_ref[...])
        u_ref[...] = u
    y = (u2 * w0 + u1 * w1) + u * w2
    conv_ref[...] = (gate_b * y).astype(bf16)


def _conv_proj(x, wb, wc, wh, wconv, p1=None, p2=None, *, group=None):
    t, d = x.shape
    c = wb.shape[0]
    tm = min(TM_IN, t)
    grid = (t // tm,)
    row_blk = lambda w: pl.BlockSpec((tm, w), lambda i: (i, 0))
    in_specs = [row_blk(d), _resident(wb.shape), _resident(wc.shape), _resident(wh.shape),
                _resident(wconv.shape)]
    args = [x, wb, wc, wh, wconv]
    if group is None:
        u_shape = jax.ShapeDtypeStruct((8, c), f32)
        u_spec = pl.BlockSpec((8, c), lambda i: (0, 0))

        def body(x_ref, wb_ref, wc_ref, wh_ref, wconv_ref, conv_ref, u_ref, ubuf):
            _conv_kernel(x_ref, wb_ref, wc_ref, wh_ref, wconv_ref, None, None,
                         conv_ref, u_ref, ubuf, tm=tm, group=None)
    else:
        in_specs += [row_blk(c), row_blk(c)]
        args += [p1, p2]
        u_shape = jax.ShapeDtypeStruct((t, c), f32)
        u_spec = row_blk(c)

        def body(*refs):
            _conv_kernel(*refs, tm=tm, group=group)

    return pl.pallas_call(
        body, grid=grid, in_specs=in_specs,
        out_specs=[row_blk(c), u_spec],
        out_shape=[jax.ShapeDtypeStruct((t, c), bf16), u_shape],
        scratch_shapes=[pltpu.VMEM((tm + 8, c), f32)],
        compiler_params=_params(), name="conv_proj")(*args)


def _prompt_attn_kernel(q_ref, k_ref, kx_ref, v_ref, o_ref, *scratch, tq, n_sub):
    m_scr, acc_scr = scratch[:n_sub], scratch[n_sub:2 * n_sub]
    s_a, s_b = scratch[2 * n_sub:3 * n_sub], scratch[3 * n_sub:]
    i = pl.program_id(1)
    ts = tq // n_sub
    ones_q = jnp.ones((ts, LANES), bf16)
    qa = [jnp.concatenate([q_ref[u * ts:(u + 1) * ts, :], ones_q], axis=-1) for u in range(n_sub)]
    ones_v = jnp.ones((tq, LANES), bf16)
    for u in range(n_sub):
        m_scr[u][...] = jnp.full(m_scr[u].shape, NEG, f32)
        acc_scr[u][...] = jnp.zeros(acc_scr[u].shape, f32)

    def n_keys(u, diagonal):
        return (u + 1) * ts if diagonal else tq

    def scores(j, s_buf, diagonal=False):
        ks = pl.multiple_of(j * tq, tq)
        ka = jnp.concatenate([k_ref[pl.ds(ks, tq), :], kx_ref[pl.ds(ks, tq), :]], axis=-1)
        for u in range(n_sub):
            nk = n_keys(u, diagonal)
            s_buf[u][:, :nk] = _dot_nt(qa[u], ka[:nk])

    def accumulate(j, s_buf, diagonal):
        ks = pl.multiple_of(j * tq, tq)
        va = jnp.concatenate([v_ref[pl.ds(ks, tq), :], ones_v], axis=-1)
        for u in range(n_sub):
            nk = n_keys(u, diagonal)
            s = s_buf[u][:, :nk]
            if diagonal:
                row = lax.broadcasted_iota(jnp.int32, (ts, nk), 0) + u * ts
                col = lax.broadcasted_iota(jnp.int32, (ts, nk), 1)
                s = jnp.where(col <= row, s, NEG)
            m_prev = m_scr[u][...]
            m_new = jnp.maximum(m_prev, jnp.max(s, axis=1, keepdims=True))
            p = jnp.exp2(s - _lane_tile(m_new, nk // LANES))
            alpha = jnp.exp2(m_prev - m_new)
            acc_scr[u][...] = (acc_scr[u][...] * _lane_tile(alpha, 2)
                               + _dot(p.astype(bf16), va[:nk]))
            m_scr[u][...] = m_new

    def run_blocks(first, n, *, prefetch_next, mask_last):
        bufs = (s_a, s_b)
        for b in range(n):
            if b + 1 < n or prefetch_next:
                scores(first + b + 1, bufs[(b + 1) % 2], diagonal=mask_last and b + 1 == n - 1)
            accumulate(first + b, bufs[b % 2], mask_last and b == n - 1)

    scores(0, s_a)

    def body(jj, c):
        run_blocks(LOOP_BLOCKS * jj, LOOP_BLOCKS, prefetch_next=True, mask_last=False)
        return c

    lax.fori_loop(0, i // LOOP_BLOCKS, body, 0)

    for rest in range(LOOP_BLOCKS):
        @pl.when(i % LOOP_BLOCKS == rest)
        def _():
            run_blocks(i - rest, rest + 1, prefetch_next=False, mask_last=True)

    for u in range(n_sub):
        acc = acc_scr[u][...]
        o_ref[u * ts:(u + 1) * ts, :] = (acc[:, :LANES] / acc[:, LANES:]).astype(o_ref.dtype)


def _prompt_attn(q, k, kx, v, *, n_heads):
    s = q.shape[0]
    tq = min(TQ, s)
    grid = (n_heads, s // tq)
    q_blk = pl.BlockSpec((tq, HEAD_DIM), lambda h, i: (i, h))
    kv_blk = pl.BlockSpec((s, HEAD_DIM), lambda h, i: (0, h))
    return pl.pallas_call(
        functools.partial(_prompt_attn_kernel, tq=tq, n_sub=N_SUB),
        grid=grid, in_specs=[q_blk, kv_blk, kv_blk, kv_blk], out_specs=q_blk,
        out_shape=jax.ShapeDtypeStruct(q.shape, bf16),
        scratch_shapes=([pltpu.VMEM((tq // N_SUB, LANES), f32)] * N_SUB
                        + [pltpu.VMEM((tq // N_SUB, 2 * LANES), f32)] * N_SUB
                        + [pltpu.VMEM((tq // N_SUB, tq), f32)] * (2 * N_SUB)),
        compiler_params=_params(("arbitrary", "arbitrary")), name="prompt_attn")(q, k, kx, v)


def _sample_attn_kernel(pt_ref, q_ref, knew_hbm, vnew_hbm, lnew_hbm, ck_hbm, cv_hbm, cl_hbm,
                        o_ref, kbuf, vbuf, lbuf, bias_scr, sbuf, sem,
                        *, n_pages, n_batch, n_heads, n_new, page):
    b = pl.program_id(0)
    slot = b % 2
    npg = n_pages + 1
    rows = page * n_heads
    new_rows = n_new * n_heads

    def copies(bb, sl):
        cps = []
        for p in range(n_pages):
            pg = pt_ref[bb * n_pages + p]
            src = pl.ds(pl.multiple_of(pg * rows, rows), rows)
            dst = pl.ds(p * rows, rows)
            cps.append(pltpu.make_async_copy(ck_hbm.at[src], kbuf.at[sl, dst], sem.at[sl, 0]))
            cps.append(pltpu.make_async_copy(cv_hbm.at[src], vbuf.at[sl, dst], sem.at[sl, 1]))
            cps.append(pltpu.make_async_copy(cl_hbm.at[pg], lbuf.at[sl, p], sem.at[sl, 2]))
        src = pl.ds(pl.multiple_of(bb * new_rows, new_rows), new_rows)
        dst = pl.ds(n_pages * rows, new_rows)
        cps.append(pltpu.make_async_copy(knew_hbm.at[src], kbuf.at[sl, dst], sem.at[sl, 0]))
        cps.append(pltpu.make_async_copy(vnew_hbm.at[src], vbuf.at[sl, dst], sem.at[sl, 1]))
        cps.append(pltpu.make_async_copy(lnew_hbm.at[bb], lbuf.at[sl, n_pages], sem.at[sl, 2]))
        return cps

    @pl.when(b == 0)
    def _():
        for sl in range(2):
            kbuf[sl, pl.ds(n_pages * rows, rows), :] = jnp.zeros((rows, HEAD_DIM), f32)
            vbuf[sl, pl.ds(n_pages * rows, rows), :] = jnp.zeros((rows, HEAD_DIM), f32)
        for cp in copies(0, 0):
            cp.start()

    @pl.when(b + 1 < n_batch)
    def _():
        for cp in copies(b + 1, 1 - slot):
            cp.start()

    kv_rows = pl.ds(0, n_pages * rows + new_rows)
    for j, region in enumerate((kbuf.at[slot, kv_rows], vbuf.at[slot, kv_rows], lbuf.at[slot])):
        pltpu.make_async_copy(region, region, sem.at[slot, j]).wait()

    row = lax.broadcasted_iota(jnp.int32, (page, page), 0)
    col = lax.broadcasted_iota(jnp.int32, (page, page), 1)
    after = jnp.where(row > col, 1.0, 0.0).astype(bf16)
    lf_all = lbuf[slot].reshape(npg * n_heads, page)
    hi, mid, lo = _split3(lf_all)
    suf = (_dot(hi, after) + _dot(mid, after)) + _dot(lo, after)
    tot = jnp.sum(lf_all, axis=1, keepdims=True)
    carry = jnp.zeros((n_heads, 1), f32)
    for p in range(npg - 1, -1, -1):
        rs = slice(p * n_heads, (p + 1) * n_heads)
        bias_scr[p] = (suf[rs, :] + carry) * LOG2E
        carry = carry + tot[rs, :]

    qb = q_ref[0].astype(bf16)
    qh = [qb[:, h * HEAD_DIM:(h + 1) * HEAD_DIM] for h in range(n_heads)]
    t_row = lax.broadcasted_iota(jnp.int32, (8, page), 0)
    lane = lax.broadcasted_iota(jnp.int32, (8, page), 1)

    def head_rows(buf, p, h):
        return buf[slot, pl.ds(p * rows + h, page, stride=n_heads), :].astype(bf16)

    def score_page(p, mvec, masked):
        out = []
        for h in range(n_heads):
            s = _dot_nt(qh[h], head_rows(kbuf, p, h)) + bias_scr[p, pl.ds(h, 1), :]
            if masked:
                s = jnp.where(lane <= t_row, s, NEG)
            sbuf[p, h] = s
            out.append(jnp.maximum(mvec[h], s))
        return tuple(out)

    unroll = math.gcd(n_pages, PAGE_UNROLL)
    mvec = lax.fori_loop(0, n_pages, functools.partial(score_page, masked=False),
                         tuple(jnp.full((8, page), NEG, f32) for _ in range(n_heads)),
                         unroll=unroll)
    mvec = score_page(n_pages, mvec, True)
    m = [jnp.max(mv, axis=1, keepdims=True) for mv in mvec]

    def value_page(p, carry):
        lvec, acc = carry
        lout, aout = [], []
        for h in range(n_heads):
            pr = jnp.exp2(sbuf[p, h] - m[h])
            lout.append(lvec[h] + pr)
            aout.append(acc[h] + _dot(pr.astype(bf16), head_rows(vbuf, p, h)))
        return tuple(lout), tuple(aout)

    zeros = tuple(jnp.zeros((8, page), f32) for _ in range(n_heads))
    lvec, acc = lax.fori_loop(0, n_pages, value_page, (zeros, zeros), unroll=unroll)
    lvec, acc = value_page(n_pages, (lvec, acc))
    for h in range(n_heads):
        l = jnp.sum(lvec[h], axis=1, keepdims=True)
        o_ref[0, :, h * HEAD_DIM:(h + 1) * HEAD_DIM] = acc[h] / l


def _sample_attn(page_table, q8, knew, vnew, lnew, ck, cv, cl, *, n_heads, n_new):
    n_batch, n_pages = page_table.shape
    page = cl.shape[-1]
    assert page == HEAD_DIM == LANES
    a = q8.shape[-1]
    rows = (n_pages + 1) * page * n_heads
    any_spec = pl.BlockSpec(memory_space=pl.ANY)
    q_blk = pl.BlockSpec((1, 8, a), lambda b, pt: (b, 0, 0))
    grid_spec = pltpu.PrefetchScalarGridSpec(
        num_scalar_prefetch=1, grid=(n_batch,),
        in_specs=[q_blk, any_spec, any_spec, any_spec, any_spec, any_spec, any_spec],
        out_specs=q_blk,
        scratch_shapes=[
            pltpu.VMEM((2, rows, HEAD_DIM), f32),
            pltpu.VMEM((2, rows, HEAD_DIM), f32),
            pltpu.VMEM((2, n_pages + 1, n_heads, page), f32),
            pltpu.VMEM((n_pages + 1, n_heads, page), f32),
            pltpu.VMEM((n_pages + 1, n_heads, 8, page), f32),
            pltpu.SemaphoreType.DMA((2, 3)),
        ])
    return pl.pallas_call(
        functools.partial(_sample_attn_kernel, n_pages=n_pages, n_batch=n_batch,
                          n_heads=n_heads, n_new=n_new, page=page),
        grid_spec=grid_spec, out_shape=jax.ShapeDtypeStruct(q8.shape, f32),
        compiler_params=_params(), name="sample_attn")(
            page_table.reshape(-1), q8, knew, vnew, lnew, ck, cv, cl)


def _layer_norm(y, g, b):
    mu = jnp.mean(y, axis=-1, keepdims=True)
    yc = y - mu
    var = jnp.mean(yc * yc, axis=-1, keepdims=True)
    return yc * lax.rsqrt(var + LN_EPS) * g + b


def _finish1_kernel(ap_ref, cp_ref, xp_ref, as_ref, cs_ref, xs_ref,
                    woa_ref, woc_ref, g_ref, b_ref, wr2_ref, wrh_ref, br_ref,
                    x1_ref, x1g_ref, meta_ref, cnt_ref, gw_ref, cnt_scr, *, alpha, nb_prompt, tm):
    is_prompt = pl.program_id(0) < nb_prompt
    attn = jnp.where(is_prompt, ap_ref[...], as_ref[...])
    conv = jnp.where(is_prompt, cp_ref[...], cs_ref[...])
    x = jnp.where(is_prompt, xp_ref[...], xs_ref[...])
    o = _dot(attn, woa_ref[...]) + _dot(conv, woc_ref[...])
    x1 = _layer_norm(alpha * x + o, g_ref[...], b_ref[...])
    x1_ref[...] = x1
    n_chunks = x1.shape[1] // LANES
    for c in range(n_chunks):
        x1g_ref[pl.ds(c, tm, stride=n_chunks), :] = x1[:, c * LANES:(c + 1) * LANES]

    x1b = x1.astype(bf16)
    x1l = (x1 - x1b.astype(f32)).astype(bf16)
    r2 = _dot(x1b, wr2_ref[...])
    lr = r2[:, :LANES] + r2[:, LANES:] + _dot(x1l, wrh_ref[...]) + br_ref[...]
    lane = lax.broadcasted_iota(jnp.int32, lr.shape, 1)

    def first_max(mask):
        vals = jnp.where(mask, lr, -jnp.inf)
        mx = jnp.max(vals, axis=-1, keepdims=True)
        idx = jnp.min(jnp.where(mask & (vals == mx), lane, LANES), axis=-1, keepdims=True)
        return mx, idx

    is_group = lane < N_GROUPS
    gmax, gidx = first_max(is_group)
    p_group = 1.0 / jnp.sum(jnp.where(is_group, jnp.exp(lr - gmax), 0.0), axis=-1, keepdims=True)
    lo = N_GROUPS + gidx * EXPERTS_PER_GROUP
    in_group = (lane >= lo) & (lane < lo + EXPERTS_PER_GROUP)
    e1, i1 = first_max(in_group)
    e2, i2 = first_max(in_group & (lane != i1))
    r = jnp.exp(e2 - e1)
    w1 = p_group / (1.0 + r)
    w2 = p_group * r / (1.0 + r)
    gw_ref[...] = jnp.where(lane == 0, w1, jnp.where(lane == 1, w2, 0.0))

    @pl.when(pl.program_id(0) == 0)
    def _():
        cnt_scr[...] = jnp.zeros(cnt_scr.shape, f32)

    id1, id2 = i1 - N_GROUPS, i2 - N_GROUPS
    sel1 = jnp.where(lane == id1, 1.0, 0.0)
    sel2 = jnp.where(lane == id2, 1.0, 0.0)
    row = lax.broadcasted_iota(jnp.int32, (tm, tm), 0)
    col = lax.broadcasted_iota(jnp.int32, (tm, tm), 1)
    before = jnp.where(col < row, 1.0, 0.0).astype(bf16)
    seen = cnt_scr[0:1, :]
    tot1 = jnp.sum(sel1, axis=0, keepdims=True)
    rank1 = jnp.sum(sel1 * (seen + _dot(before, sel1.astype(bf16))), axis=-1, keepdims=True)
    rank2 = jnp.sum(sel2 * ((seen + tot1) + _dot(before, sel2.astype(bf16))), axis=-1, keepdims=True)
    seen = (seen + tot1) + jnp.sum(sel2, axis=0, keepdims=True)
    cnt_scr[...] = jnp.broadcast_to(seen, cnt_scr.shape)
    cnt_ref[...] = jnp.broadcast_to(seen, cnt_ref.shape)
    meta = jnp.where(lane == 0, id1.astype(f32), jnp.where(lane == 1, id2.astype(f32),
                     jnp.where(lane == 2, rank1, jnp.where(lane == 3, rank2, 0.0))))
    meta_ref[...] = meta.T[0:8, :]


def _finish1(attn_p, conv_p, x_p, attn_s, conv_s, x_s, wo, g, b, wr2, wrh, br, *, alpha):
    (t_p, d), t_s = x_p.shape, x_s.shape[0]
    a = attn_p.shape[1]
    assert wo.shape[0] == 2 * a
    tm = min(TM_PROJ, t_s)
    nb_p, nb_s = t_p // tm, t_s // tm
    t_all = t_p + t_s
    p_blk = lambda w: pl.BlockSpec((tm, w), lambda i: (jnp.minimum(i, nb_p - 1), 0))
    s_blk = lambda w: pl.BlockSpec((tm, w), lambda i: (jnp.maximum(i - nb_p, 0), 0))
    out_blk = lambda w: pl.BlockSpec((tm, w), lambda i: (i, 0))
    in_specs = [p_blk(a), p_blk(a), p_blk(d), s_blk(a), s_blk(a), s_blk(d),
                pl.BlockSpec((a, d), lambda i: (0, 0), pipeline_mode=pl.Buffered(1)),
                pl.BlockSpec((a, d), lambda i: (1, 0), pipeline_mode=pl.Buffered(1)),
                _resident(g.shape), _resident(b.shape), _resident(wr2.shape),
                _resident(wrh.shape), _resident(br.shape)]
    n_chunks = d // LANES
    return pl.pallas_call(
        functools.partial(_finish1_kernel, alpha=alpha, nb_prompt=nb_p, tm=tm),
        grid=(nb_p + nb_s,), in_specs=in_specs,
        out_specs=[out_blk(d), pl.BlockSpec((tm * n_chunks, LANES), lambda i: (i, 0)),
                   pl.BlockSpec((8, tm), lambda i: (0, i)),
                   pl.BlockSpec((8, LANES), lambda i: (0, 0)), out_blk(LANES)],
        out_shape=[jax.ShapeDtypeStruct((t_all, d), f32),
                   jax.ShapeDtypeStruct((t_all * n_chunks, LANES), f32),
                   jax.ShapeDtypeStruct((8, t_all), f32),
                   jax.ShapeDtypeStruct((8, LANES), f32),
                   jax.ShapeDtypeStruct((t_all, LANES), f32)],
        scratch_shapes=[pltpu.VMEM((8, LANES), f32)],
        compiler_params=_params(), name="finish1")(
            attn_p, conv_p, x_p, attn_s, conv_s, x_s, wo, wo, g, b, wr2, wrh, br)


def _dispatch_kernel(lt_ref, nu_ref, pos_ref, x_ref, xs_hbm, zbuf, sem, zsem, *, tm, tm_moe, n_tiles):
    s = pl.program_id(0)

    @pl.when(s == 0)
    def _():
        zbuf[...] = jnp.zeros(zbuf.shape, f32)

        def zero_tile(row0):
            return pltpu.make_async_copy(zbuf, xs_hbm.at[pl.ds(row0, tm_moe)], zsem)

        def each_zero_tile(act):
            for e in range(N_EXPERTS):
                @pl.when(lt_ref[e] >= 0)
                def _():
                    act(zero_tile(lt_ref[e]))

            def unused(t, c):
                act(zero_tile(t * tm_moe))
                return c

            lax.fori_loop(nu_ref[0], n_tiles, unused, 0)

        each_zero_tile(lambda cp: cp.start())
        each_zero_tile(lambda cp: cp.wait())

    def issue(r, c):
        pltpu.make_async_copy(x_ref.at[r], xs_hbm.at[pos_ref[0, 0, r]], sem).start(priority=0)
        pltpu.make_async_copy(x_ref.at[r], xs_hbm.at[pos_ref[0, 0, tm + r]], sem).start(priority=1)
        return c

    lax.fori_loop(0, tm, issue, 0, unroll=8)
    done = xs_hbm.at[pl.ds(0, 2 * tm)]
    pltpu.make_async_copy(done, done, sem).wait()


def _dispatch(x1g, pos, last_tile_row, n_used, *, n_tiles, tm_moe):
    t, n_chunks, _ = x1g.shape
    n_steps, _, two_tm = pos.shape
    tm = two_tm // 2
    grid_spec = pltpu.PrefetchScalarGridSpec(
        num_scalar_prefetch=2, grid=(n_steps,),
        in_specs=[pl.BlockSpec((1, 1, two_tm), lambda s, *_: (s, 0, 0), memory_space=pltpu.SMEM),
                  pl.BlockSpec((tm, n_chunks, LANES), lambda s, *_: (s, 0, 0))],
        out_specs=pl.BlockSpec(memory_space=pl.ANY),
        scratch_shapes=[pltpu.VMEM((tm_moe, n_chunks, LANES), f32),
                        pltpu.SemaphoreType.DMA(()), pltpu.SemaphoreType.DMA(())])
    return pl.pallas_call(
        functools.partial(_dispatch_kernel, tm=tm, tm_moe=tm_moe, n_tiles=n_tiles),
        grid_spec=grid_spec,
        out_shape=jax.ShapeDtypeStruct((n_tiles * tm_moe, n_chunks, LANES), f32),
        compiler_params=_params(), name="dispatch")(last_tile_row, n_used, pos, x1g)


def _moe_kernel(te_ref, ord_ref, nxt_ref, nu_ref, x_ref, wg_hbm, wu_hbm, wd_hbm, y_ref,
                wgf, wuf, wdf, wgb, wub, wdb, wsem, *, tm, n_chunks):
    t = pl.program_id(0)
    n_used = nu_ref[0]

    def weight_copies(e, wslot):
        return (pltpu.make_async_copy(wg_hbm.at[e], wgf.at[wslot], wsem.at[wslot, 0]),
                pltpu.make_async_copy(wu_hbm.at[e], wuf.at[wslot], wsem.at[wslot, 1]),
                pltpu.make_async_copy(wd_hbm.at[e], wdf.at[wslot], wsem.at[wslot, 2]))

    @pl.when(t == 0)
    def _():
        for cp in weight_copies(te_ref[0], 0):
            cp.start(priority=1)

    @pl.when(t < n_used)
    def _():
        e = te_ref[t]
        new_expert = (t == 0) | (e != te_ref[jnp.maximum(t - 1, 0)])

        @pl.when(new_expert)
        def _():
            wslot = ord_ref[t] % 2
            for cp in weight_copies(e, wslot):
                cp.wait()
            nxt = nxt_ref[e]

            @pl.when(nxt >= 0)
            def _():
                for cp in weight_copies(nxt, 1 - wslot):
                    cp.start(priority=1)

            wgb[...] = wgf[wslot].astype(bf16)
            wub[...] = wuf[wslot].astype(bf16)
            wdb[...] = wdf[wslot].astype(bf16)

        xb = jnp.concatenate([x_ref[pl.ds(c, tm, stride=n_chunks), :] for c in range(n_chunks)],
                             axis=-1).astype(bf16)
        hg = _dot(xb, wgb[...])
        hu = _dot(xb, wub[...])
        act = hg * (1.0 / (1.0 + jnp.exp(-hg))) * hu
        y = _dot(act.astype(bf16), wdb[...])
        for c in range(n_chunks):
            y_ref[pl.ds(c, tm, stride=n_chunks), :] = y[:, c * LANES:(c + 1) * LANES]

    @pl.when(t >= n_used)
    def _():
        y_ref[...] = jnp.zeros(y_ref.shape, f32)


def _moe(xs, tile_expert, tile_ord, next_expert, n_used, wg, wu, wd, *, tm):
    _, d, f = wg.shape
    n_chunks = d // LANES
    n_tiles = xs.shape[0] // (tm * n_chunks)
    any_spec = pl.BlockSpec(memory_space=pl.ANY)

    def x_idx(t, te, od, nx, nu):
        return (jnp.minimum(t, jnp.maximum(nu[0] - 1, 0)), 0)

    grid_spec = pltpu.PrefetchScalarGridSpec(
        num_scalar_prefetch=4, grid=(n_tiles,),
        in_specs=[pl.BlockSpec((tm * n_chunks, LANES), x_idx), any_spec, any_spec, any_spec],
        out_specs=pl.BlockSpec((tm * n_chunks, LANES), lambda t, *_: (t, 0)),
        scratch_shapes=[pltpu.VMEM((2, d, f), f32), pltpu.VMEM((2, d, f), f32),
                        pltpu.VMEM((2, f, d), f32),
                        pltpu.VMEM((d, f), bf16), pltpu.VMEM((d, f), bf16), pltpu.VMEM((f, d), bf16),
                        pltpu.SemaphoreType.DMA((2, 3))])
    return pl.pallas_call(
        functools.partial(_moe_kernel, tm=tm, n_chunks=n_chunks),
        grid_spec=grid_spec, out_shape=jax.ShapeDtypeStruct(xs.shape, f32),
        compiler_params=_params(), name="moe")(
            tile_expert, tile_ord, next_expert, n_used, xs, wg, wu, wd)


def _finish2_kernel(idx_ref, ys_hbm, x1_ref, gw_ref, g_ref, b_ref, o_ref, gbuf, sem,
                    *, tm, n_steps, alpha):
    s = pl.program_id(0)
    n_chunks = ys_hbm.shape[1]
    pitch = n_chunks + ROW_PITCH_PAD

    @pl.when(s < n_steps)
    def _():
        slot = s % 2

        def issue(r, c):
            for k in range(2):
                rr = k * tm + r
                pltpu.make_async_copy(ys_hbm.at[idx_ref[0, 0, rr]],
                                      gbuf.at[slot, pl.ds(rr * pitch, n_chunks)],
                                      sem.at[slot]).start(priority=k)
            return c

        lax.fori_loop(0, tm, issue, 0, unroll=4)

    @pl.when(s >= 1)
    def _():
        slot = (s - 1) % 2
        landed = gbuf.at[slot, pl.ds(0, 2 * tm * n_chunks)]
        pltpu.make_async_copy(landed, landed, sem.at[slot]).wait()

        def rows(first):
            return jnp.concatenate(
                [gbuf[slot, pl.ds(first * pitch + c, tm, stride=pitch), :] for c in range(n_chunks)],
                axis=-1)

        gw = gw_ref[...]
        f = gw[:, 0:1] * rows(0) + gw[:, 1:2] * rows(tm)
        o_ref[...] = _layer_norm(alpha * x1_ref[...] + f, g_ref[...], b_ref[...])


def _finish2(ys, pos, x1, gw, g, b, *, alpha, row0, t):
    n_steps, _, two_tm = pos.shape
    tm = two_tm // 2
    d = x1.shape[1]
    n_chunks = d // LANES
    blk0 = row0 // tm
    last = n_steps - 1
    in_blk = lambda w: pl.BlockSpec((tm, w), lambda s: (jnp.maximum(s - 1, 0) + blk0, 0))
    return pl.pallas_call(
        functools.partial(_finish2_kernel, tm=tm, n_steps=n_steps, alpha=alpha),
        grid=(n_steps + 1,),
        in_specs=[pl.BlockSpec((1, 1, two_tm), lambda s: (jnp.minimum(s, last), 0, 0),
                               memory_space=pltpu.SMEM),
                  pl.BlockSpec(memory_space=pl.ANY),
                  in_blk(d), in_blk(LANES), _resident(g.shape), _resident(b.shape)],
        out_specs=pl.BlockSpec((tm, d), lambda s: (jnp.maximum(s - 1, 0), 0)),
        out_shape=jax.ShapeDtypeStruct((t, d), f32),
        scratch_shapes=[pltpu.VMEM((2, two_tm * (n_chunks + ROW_PITCH_PAD), LANES), f32),
                        pltpu.SemaphoreType.DMA((2,))],
        compiler_params=_params(), name="finish2")(pos, ys, x1, gw, g, b)


def _bias_selectors(n_heads, sign):
    sel = np.zeros((3, n_heads, n_heads * HEAD_DIM), np.float32)
    for p in range(3):
        for h in range(n_heads):
            sel[p, h, h * HEAD_DIM + p] = sign
    return sel


def _route_tables(meta, counts, tm):
    n_tok = meta.shape[1]
    e = meta[0:2].astype(jnp.int32)
    rank = meta[2:4].astype(jnp.int32)
    counts = counts[0, :N_EXPERTS].astype(jnp.int32)
    n_pairs = 2 * n_tok
    padded = ((counts + tm - 1) // tm) * tm
    ends = jnp.cumsum(padded)
    experts = jnp.arange(N_EXPERTS, dtype=jnp.int32)
    first_row = jnp.sum(jnp.where(e[..., None] == experts, ends - padded, 0), axis=-1)
    pos = first_row + rank
    n_tiles = (n_pairs + N_EXPERTS * (tm - 1)) // tm
    last_tile_row = jnp.where(counts > 0, ends - tm, -1).astype(jnp.int32)
    tile_start = jnp.arange(n_tiles, dtype=jnp.int32) * tm
    tile_expert = jnp.minimum(
        jnp.sum((ends[None, :] <= tile_start[:, None]).astype(jnp.int32), axis=1), N_EXPERTS - 1)
    n_used = (ends[-1] // tm).astype(jnp.int32).reshape(1)
    nonempty = counts > 0
    expert_ord = jnp.cumsum(nonempty.astype(jnp.int32)) - 1
    later = nonempty[None, :] & (experts[None, :] > experts[:, None])
    next_expert = jnp.min(jnp.where(later, experts[None, :], N_EXPERTS), axis=1)
    next_expert = jnp.where(next_expert < N_EXPERTS, next_expert, -1).astype(jnp.int32)
    tile_ord = expert_ord[tile_expert].astype(jnp.int32)
    return pos, last_tile_row, n_tiles, tile_expert, tile_ord, next_expert, n_used


def _tile_pos(pos, tm):
    t = pos.shape[1]
    return jnp.concatenate([pos[0].reshape(t // tm, 1, tm), pos[1].reshape(t // tm, 1, tm)], axis=2)


def kernel(x_prompt, x_sample, cache_k, cache_v, cache_logf, state_conv, page_table, w_in, b_f,
           w_conv, w_out, ln1_g, ln1_b, w_rg, b_rg, w_re, b_re, w_gate, w_up, w_down, ln2_g, ln2_b):
    depth, d_model, _ = w_in.shape
    n_batch_p, seq, _ = x_prompt.shape
    db, dseq, _ = x_sample.shape
    n_heads = cache_k.shape[3]
    a = n_heads * HEAD_DIM
    c = w_conv.shape[-1]
    assert n_batch_p == 1 and dseq <= 8 and w_conv.shape[1] == CONV_K
    alpha = (2.0 * depth) ** 0.25
    qscale = HEAD_DIM ** -0.5 * LOG2E
    t_p, t_s = seq, db * dseq
    t_all = t_p + t_s

    sel_neg = _bias_selectors(n_heads, -1.0)
    psel_k = jnp.asarray(np.concatenate(
        [np.pad(sel_neg[p], ((0, LANES - n_heads), (0, 0))) for p in range(3)], axis=0), bf16)

    xp = x_prompt.reshape(t_p, d_model)
    xs = x_sample.reshape(t_s, d_model)
    outs = {k: [] for k in ("kp", "vp", "fp", "cp", "ks", "vs", "fs", "cs")}
    for l in range(depth):
        wl = jnp.swapaxes(w_in[l], 0, 1)
        wq, wk, wv = (wl[j * a:(j + 1) * a].astype(bf16) for j in range(3))
        o = 3 * a
        wf = jnp.pad(wl[o:o + n_heads], ((0, LANES - n_heads), (0, 0)))
        wfh = wf.astype(bf16)
        wf2 = jnp.concatenate([wfh, (wf - wfh.astype(f32)).astype(bf16)], axis=0)
        bfp = jnp.pad(b_f[l], (0, LANES - n_heads)).reshape(1, LANES)
        o += n_heads
        wb, wc, wh = (wl[o + j * c:o + (j + 1) * c].astype(bf16) for j in range(3))
        wo = w_out[l].astype(bf16)
        wr = jnp.pad(jnp.concatenate([w_rg[l], w_re[l]], axis=1),
                     ((0, 0), (0, LANES - N_GROUPS - N_EXPERTS)))
        wrh = wr.astype(bf16)
        wr2 = jnp.concatenate([wrh, (wr - wrh.astype(f32)).astype(bf16)], axis=1)
        br = jnp.pad(jnp.concatenate([b_rg[l], b_re[l]]), (0, LANES - N_GROUPS - N_EXPERTS)).reshape(1, LANES)
        g1, b1 = ln1_g[l].reshape(1, -1), ln1_b[l].reshape(1, -1)
        g2, b2 = ln2_g[l].reshape(1, -1), ln2_b[l].reshape(1, -1)

        q, ko, vo, kb, vb, lf, kx = _qkv_proj(xp, wq, wk, wv, wf2, wfh, bfp, psel_k,
                                              n_heads=n_heads, qscale=qscale, with_cum=True)
        conv, utail = _conv_proj(xp, wb, wc, wh, w_conv[l])
        attn = _prompt_attn(q, kb, kx, vb, n_heads=n_heads)
        outs["kp"].append(ko.reshape(1, t_p, n_heads, HEAD_DIM))
        outs["vp"].append(vo.reshape(1, t_p, n_heads, HEAD_DIM))
        outs["fp"].append(lf[:, :n_heads].reshape(1, t_p, n_heads))
        outs["cp"].append(utail[8 - (CONV_K - 1):].reshape(1, CONV_K - 1, c))

        qs, kos, vos, _, _, lfs = _qkv_proj(xs, wq, wk, wv, wf2, wfh, bfp, None,
                                            n_heads=n_heads, qscale=qscale, with_cum=False)
        st = state_conv[l]
        zero = jnp.zeros((db, dseq, c), f32)
        p1 = zero.at[:, 0].set(st[:, 1]).reshape(t_s, c)
        p2 = zero.at[:, 0].set(st[:, 0]).at[:, 1].set(st[:, 1]).reshape(t_s, c)
        conv_s, u_s = _conv_proj(xs, wb, wc, wh, w_conv[l], p1, p2, group=dseq)
        q8 = jnp.pad(qs.reshape(db, dseq, a), ((0, 0), (0, 8 - dseq), (0, 0))).astype(f32)
        page = cache_k.shape[2]
        lnew = jnp.pad(jnp.swapaxes(lfs[:, :n_heads].reshape(db, dseq, n_heads), 1, 2),
                       ((0, 0), (0, 0), (0, page - dseq)))
        attn_s = _sample_attn(page_table, q8, kos, vos, lnew,
                              cache_k[l].reshape(-1, HEAD_DIM), cache_v[l].reshape(-1, HEAD_DIM),
                              jnp.swapaxes(cache_logf[l], 1, 2), n_heads=n_heads, n_new=dseq)
        attn_s = attn_s[:, :dseq].reshape(t_s, a).astype(bf16)
        outs["ks"].append(kos.reshape(db, dseq, n_heads, HEAD_DIM))
        outs["vs"].append(vos.reshape(db, dseq, n_heads, HEAD_DIM))
        outs["fs"].append(lfs[:, :n_heads].reshape(db, dseq, n_heads))
        outs["cs"].append(u_s.reshape(db, dseq, c)[:, dseq - (CONV_K - 1):])
        x1, x1g, meta, counts, gw = _finish1(attn, conv, xp, attn_s, conv_s, xs, wo, g1, b1,
                                             wr2, wrh, br, alpha=alpha)

        pos, last_tile_row, n_tiles, tile_expert, tile_ord, next_expert, n_used = _route_tables(
            meta, counts, TM_MOE)
        n_chunks = d_model // LANES
        tm2 = min(TM_PROJ, t_s)
        xsort = _dispatch(x1g.reshape(t_all, n_chunks, LANES), _tile_pos(pos, tm2), last_tile_row,
                          n_used, n_tiles=n_tiles, tm_moe=TM_MOE)
        ys = _moe(xsort.reshape(-1, LANES), tile_expert, tile_ord, next_expert, n_used,
                  w_gate[l], w_up[l], w_down[l], tm=TM_MOE)
        ys = ys.reshape(-1, n_chunks, LANES)
        xp = _finish2(ys, _tile_pos(pos[:, :t_p], TM_PROJ), x1, gw, g2, b2, alpha=alpha, row0=0, t=t_p)
        xs = _finish2(ys, _tile_pos(pos[:, t_p:], tm2), x1, gw, g2, b2, alpha=alpha, row0=t_p, t=t_s)

    stack = lambda k: jnp.stack(outs[k])
    return (xp.reshape(x_prompt.shape), xs.reshape(x_sample.shape),
            stack("kp"), stack("vp"), stack("fp"), stack("cp"),
            stack("ks"), stack("vs"), stack("fs"), stack("cs"))
```
